```python
import math
import jax, jax.numpy as jnp
from jax import lax
import numpy as np

D_MODEL = 1024
BATCH = 4
SEQ = 4096
DEPTH = 2

CHUNK = 64
N_META = 16
PAD_FRONT = CHUNK - N_META

SSD_EXPAND = 2
SSD_INNER = SSD_EXPAND * D_MODEL
SSD_HEAD_DIM = 64
SSD_HEADS = SSD_INNER // SSD_HEAD_DIM
SSD_GROUPS = 8
SSD_HPG = SSD_HEADS // SSD_GROUPS
SSD_STATE = 128
SSD_CONV = 4
SSD_GN = SSD_GROUPS * SSD_STATE
SSD_XBC = SSD_INNER + 2 * SSD_GN

POOL_WIDTH = D_MODEL
POOL_WINDOWS = (2, 4, 8, 16)
POOL_GROUPS = len(POOL_WINDOWS)
POOL_GROUP_DIM = POOL_WIDTH // POOL_GROUPS

N_BRANCH = 2
COL_Z = 0
COL_XBC = COL_Z + SSD_INNER
COL_DT = COL_XBC + SSD_XBC
COL_POOL = COL_DT + SSD_HEADS
COL_GATE = COL_POOL + POOL_WIDTH
IN_COLS = COL_GATE + N_BRANCH * D_MODEL

N_EXPERTS = 16
N_EXPERT_GROUPS = 4
EXPERTS_PER_GROUP = N_EXPERTS // N_EXPERT_GROUPS
TOP_K = 2
D_EXPERT = 512

DN_ALPHA = (2.0 * DEPTH) ** 0.25
DN_BETA = (8.0 * DEPTH) ** -0.25
LN_EPS = 1e-5
RMS_EPS = 1e-5

kernel_name = "hybrid_ssd_pool_moe_deepnorm"


def layer_norm(x, g, b):
    xf = x.astype(jnp.float32)
    mu = jnp.mean(xf, -1, keepdims=True)
    var = jnp.mean(jnp.square(xf - mu), -1, keepdims=True)
    return ((xf - mu) * lax.rsqrt(var + LN_EPS) * g + b).astype(x.dtype)


def causal_dwconv(x, w, b):
    y = lax.conv_general_dilated(
        x, w[:, None, :].astype(x.dtype), window_strides=(1,), padding=[(SSD_CONV - 1, 0)],
        dimension_numbers=("NWC", "WIO", "NWC"), feature_group_count=x.shape[-1])
    return y + b


def ssd_chunked(xs, dA, Bm, Cm):
    b, L = xs.shape[:2]

    def pad(t):
        return jnp.pad(t, [(0, 0), (PAD_FRONT, 0)] + [(0, 0)] * (t.ndim - 2))

    xs, dA, Bm, Cm = pad(xs), pad(dA), pad(Bm), pad(Cm)
    nc = (L + PAD_FRONT) // CHUNK
    xs = xs.reshape(b, nc, CHUNK, SSD_GROUPS, SSD_HPG, SSD_HEAD_DIM)
    Bm = Bm.reshape(b, nc, CHUNK, SSD_GROUPS, SSD_STATE)
    Cm = Cm.reshape(b, nc, CHUNK, SSD_GROUPS, SSD_STATE)
    a_cs = jnp.cumsum(dA.astype(jnp.float32).reshape(b, nc, CHUNK, SSD_GROUPS, SSD_HPG), axis=2)

    li = jnp.arange(CHUNK)
    causal = (li[:, None] >= li[None, :])[None, None, :, :, None, None]
    seg = a_cs[:, :, :, None] - a_cs[:, :, None, :]
    decay = jnp.exp(jnp.where(causal, seg, -jnp.inf))
    cb = jnp.einsum("bclgn,bcsgn->bclsg", Cm, Bm)
    y_diag = jnp.einsum("bclsg,bclsgr,bcsgrp->bclgrp", cb, decay, xs)

    decay_to_end = jnp.exp(a_cs[:, :, -1:] - a_cs)
    states = jnp.einsum("bclgn,bclgr,bclgrp->bcgrpn", Bm, decay_to_end, xs)
    chunk_decay = jnp.exp(a_cs[:, :, -1])

    def step(h, inp):
        s_c, d_c = inp
        return h * d_c[..., None, None] + s_c, h

    h0 = jnp.zeros((b,) + states.shape[2:], states.dtype)
    _, prev = lax.scan(step, h0, (jnp.moveaxis(states, 1, 0), jnp.moveaxis(chunk_decay, 1, 0)))
    prev = jnp.moveaxis(prev, 0, 1)
    y_off = jnp.einsum("bclgn,bcgrpn,bclgr->bclgrp", Cm, prev, jnp.exp(a_cs))

    y = (y_diag + y_off).reshape(b, nc * CHUNK, SSD_GROUPS, SSD_HPG, SSD_HEAD_DIM)
    return y[:, PAD_FRONT:]


def ssd_branch(proj, conv_w, conv_b, dt_bias, a_log, d_skip, norm_w):
    b, L, _ = proj.shape
    z = proj[..., COL_Z:COL_XBC]
    xbc = jax.nn.silu(causal_dwconv(proj[..., COL_XBC:COL_DT], conv_w, conv_b))
    xh = xbc[..., :SSD_INNER].reshape(b, L, SSD_GROUPS, SSD_HPG, SSD_HEAD_DIM)
    Bm = xbc[..., SSD_INNER:SSD_INNER + SSD_GN].reshape(b, L, SSD_GROUPS, SSD_STATE)
    Cm = xbc[..., SSD_INNER + SSD_GN:].reshape(b, L, SSD_GROUPS, SSD_STATE)
    dt = jax.nn.softplus(proj[..., COL_DT:COL_POOL].astype(jnp.float32) + dt_bias)
    dt = dt.reshape(b, L, SSD_GROUPS, SSD_HPG)
    A = -jnp.exp(a_log.astype(jnp.float32)).reshape(SSD_GROUPS, SSD_HPG)
    y = ssd_chunked(xh * dt[..., None], dt * A, Bm, Cm)
    y = y + d_skip.reshape(SSD_GROUPS, SSD_HPG)[:, :, None] * xh
    y = y.reshape(b, L, SSD_INNER)
    yg = (y * jax.nn.silu(z.astype(jnp.float32))).reshape(b, L, SSD_GROUPS, -1)
    yg = yg * lax.rsqrt(jnp.mean(jnp.square(yg), -1, keepdims=True) + RMS_EPS)
    return (yg.reshape(b, L, SSD_INNER) * norm_w).astype(proj.dtype)


def pool_branch(p, w_group, scale):
    b, L, _ = p.shape
    pf = p.astype(jnp.float32).reshape(b, L, POOL_GROUPS, POOL_GROUP_DIM)
    cs = jnp.pad(jnp.cumsum(pf, axis=1), [(0, 0), (1, 0), (0, 0), (0, 0)])
    t = jnp.arange(L)[:, None]
    win = jnp.array(POOL_WINDOWS, dtype=jnp.int32)[None, :]
    start = jnp.maximum(t + 1 - win, 0)
    gidx = jnp.arange(POOL_GROUPS)[None, :]
    wsum = cs[:, 1:] - cs[:, start, gidx]
    cnt = jnp.minimum(t + 1, win).astype(jnp.float32)
    mixed = wsum / cnt[None, :, :, None] - pf
    out = jnp.einsum("blgc,gcd->blgd", mixed, w_group).reshape(b, L, POOL_WIDTH)
    return (out * scale).astype(p.dtype)


def moe_ffn(x, w_router, w_gate, w_up, w_down):
    b, L, d = x.shape
    t = x.reshape(-1, d)
    scores = jax.nn.softmax((t @ w_router).astype(jnp.float32), axis=-1)
    grp = scores.reshape(-1, N_EXPERT_GROUPS, EXPERTS_PER_GROUP)
    grp_score = lax.top_k(grp, TOP_K)[0].sum(-1)
    sel = jnp.argmax(grp_score, axis=-1)
    in_grp = (jnp.arange(N_EXPERTS) // EXPERTS_PER_GROUP)[None, :] == sel[:, None]
    topv, topi = lax.top_k(jnp.where(in_grp, scores, -1.0), TOP_K)
    topv = topv / jnp.sum(topv, -1, keepdims=True)
    comb = jnp.sum(jax.nn.one_hot(topi, N_EXPERTS, dtype=jnp.float32) * topv[..., None], axis=1)
    out = jnp.zeros(t.shape, jnp.float32)
    for e in range(N_EXPERTS):
        h = jax.nn.silu(t @ w_gate[e]) * (t @ w_up[e])
        out = out + comb[:, e:e + 1] * (h @ w_down[e])
    return out.astype(x.dtype).reshape(b, L, d)


def setup_inputs(seed: int = 0) -> dict:
    key = jax.random.key(seed)
    ks = jax.random.split(key, 24)
    f32 = jnp.float32

    def nrm(k, shape, scale):
        return jax.random.normal(k, shape, f32) * scale

    dt = jnp.exp(jax.random.uniform(ks[6], (DEPTH, SSD_HEADS), f32)
                 * (math.log(0.1) - math.log(1e-3)) + math.log(1e-3))
    return {
        "x": nrm(ks[0], (BATCH, SEQ, D_MODEL), 1.0),
        "meta_tokens": nrm(ks[1], (N_META, D_MODEL), 1.0),
        "ln_in_g": 1.0 + nrm(ks[2], (D_MODEL,), 0.02),
        "ln_in_b": nrm(ks[3], (D_MODEL,), 0.02),
        "w_router": nrm(ks[4], (D_MODEL, N_EXPERTS), D_MODEL ** -0.5),
        "w_in": nrm(ks[5], (DEPTH, D_MODEL, IN_COLS), D_MODEL ** -0.5),
        "conv_w": nrm(ks[7], (DEPTH, SSD_CONV, SSD_XBC), SSD_CONV ** -0.5),
        "conv_b": nrm(ks[8], (DEPTH, SSD_XBC), 0.02),
        "dt_bias": dt + jnp.log(-jnp.expm1(-dt)),
        "a_log": jnp.log(jax.random.uniform(ks[9], (DEPTH, SSD_HEADS), f32, 1.0, 16.0)),
        "d_skip": 1.0 + nrm(ks[10], (DEPTH, SSD_HEADS), 0.02),
        "ssd_norm_w": 1.0 + nrm(ks[11], (DEPTH, SSD_INNER), 0.02),
        "w_ssd_out": nrm(ks[12], (DEPTH, SSD_INNER, D_MODEL), SSD_INNER ** -0.5 * DN_BETA),
        "w_pool": nrm(ks[13], (DEPTH, POOL_GROUPS, POOL_GROUP_DIM, POOL_GROUP_DIM), POOL_GROUP_DIM ** -0.5 * DN_BETA),
        "pool_scale": 1.0 + nrm(ks[14], (DEPTH, POOL_WIDTH), 0.02),
        "b_gate": nrm(ks[15], (DEPTH, N_BRANCH * D_MODEL), 0.02),
        "w_out": nrm(ks[16], (DEPTH, D_MODEL, D_MODEL), D_MODEL ** -0.5 * DN_BETA),
        "ln1_g": 1.0 + nrm(ks[17], (DEPTH, D_MODEL), 0.02),
        "ln1_b": nrm(ks[18], (DEPTH, D_MODEL), 0.02),
        "w_exp_gate": nrm(ks[19], (DEPTH, N_EXPERTS, D_MODEL, D_EXPERT), D_MODEL ** -0.5),
        "w_exp_up": nrm(ks[20], (DEPTH, N_EXPERTS, D_MODEL, D_EXPERT), D_MODEL ** -0.5 * DN_BETA),
        "w_exp_down": nrm(ks[21], (DEPTH, N_EXPERTS, D_EXPERT, D_MODEL), D_EXPERT ** -0.5 * DN_BETA),
        "ln2_g": 1.0 + nrm(ks[22], (DEPTH, D_MODEL), 0.02),
        "ln2_b": nrm(ks[23], (DEPTH, D_MODEL), 0.02),
    }


def reference(x, meta_tokens, ln_in_g, ln_in_b, w_router, w_in, conv_w, conv_b, dt_bias, a_log,
              d_skip, ssd_norm_w, w_ssd_out, w_pool, pool_scale, b_gate, w_out, ln1_g, ln1_b,
              w_exp_gate, w_exp_up, w_exp_down, ln2_g, ln2_b):
    b = x.shape[0]
    meta = jnp.broadcast_to(meta_tokens[None].astype(x.dtype), (b, N_META, D_MODEL))
    h = layer_norm(jnp.concatenate([meta, x], axis=1), ln_in_g, ln_in_b)
    L = h.shape[1]
    for i in range(DEPTH):
        proj = h @ w_in[i]
        y_ssd = ssd_branch(proj, conv_w[i], conv_b[i], dt_bias[i], a_log[i], d_skip[i],
                           ssd_norm_w[i]) @ w_ssd_out[i]
        y_pool = pool_branch(proj[..., COL_POOL:COL_GATE], w_pool[i], pool_scale[i])
        gates = jax.nn.sigmoid(proj[..., COL_GATE:] + b_gate[i]).reshape(b, L, N_BRANCH, D_MODEL)
        mixed = (gates[:, :, 0] * y_ssd + gates[:, :, 1] * y_pool) @ w_out[i]
        h = layer_norm(DN_ALPHA * h + mixed, ln1_g[i], ln1_b[i])
        ffn = moe_ffn(h, w_router, w_exp_gate[i], w_exp_up[i], w_exp_down[i])
        h = layer_norm(DN_ALPHA * h + ffn, ln2_g[i], ln2_b[i])
    return h[:, N_META:]
```

```python
import functools

import jax
import jax.numpy as jnp
import numpy as np
from jax import lax
from jax.experimental import pallas as pl
from jax.experimental.pallas import tpu as pltpu

F32 = jnp.float32
BF16 = jnp.bfloat16

D_MODEL = 1024
CHUNK = 64
N_META = 16
PAD_FRONT = CHUNK - N_META
FRAME_HEAD = PAD_FRONT + N_META

SSD_INNER = 2048
SSD_HEAD_DIM = 64
SSD_HEADS = 32
SSD_GROUPS = 8
SSD_HPG = 4
SSD_STATE = 128
SSD_CONV = 4
SSD_GN = SSD_GROUPS * SSD_STATE
SSD_XBC = SSD_INNER + 2 * SSD_GN
SSD_GROUP_DIM = SSD_HPG * SSD_HEAD_DIM

POOL_WIDTH = 1024
POOL_WINDOWS = (2, 4, 8, 16)
POOL_GROUP_DIM = 256
POOL_HALO = 16

N_BRANCH = 2
COL_Z = 0
COL_XBC = COL_Z + SSD_INNER
COL_DT = COL_XBC + SSD_XBC
COL_POOL = COL_DT + SSD_HEADS
COL_GATE = COL_POOL + POOL_WIDTH
IN_COLS = COL_GATE + N_BRANCH * D_MODEL

N_EXPERTS = 16
N_EXPERT_GROUPS = 4
EXPERTS_PER_GROUP = 4
D_EXPERT = 512
PAIRS = ((0, 1), (0, 2), (0, 3), (1, 2), (1, 3), (2, 3))
N_CLASSES = N_EXPERT_GROUPS * len(PAIRS)

DEPTH = 2
DN_ALPHA = (2.0 * DEPTH) ** 0.25
LN_EPS = 1e-5
RMS_EPS = 1e-5

LANES = 128
SUBLANES = 8
VMEM_LIMIT = 56 * 1024 * 1024

PROJ_TN = 1024
PROJ_COLS = SSD_XBC + SSD_INNER + N_BRANCH * D_MODEL + POOL_WIDTH
PROJ_BLK_XBC = 0
PROJ_BLK_Z = SSD_XBC // SSD_INNER
PROJ_BLK_GATE = PROJ_BLK_Z + 1
PROJ_BLK_POOL = (SSD_XBC + SSD_INNER + N_BRANCH * D_MODEL) // POOL_WIDTH

EXPERT_TM = 128


def _pick_tile(n, target, mult):
    best = None
    for t in range(mult, min(n, target) + 1, mult):
        if n % t == 0:
            best = t
    assert best is not None, (n, target, mult)
    return best


def _params(*sem):
    return pltpu.CompilerParams(dimension_semantics=sem, vmem_limit_bytes=VMEM_LIMIT)


def _silu(x):
    return x * (1.0 / (1.0 + jnp.exp(-x)))


def _layer_norm(x, g, b):
    mu = jnp.mean(x, axis=-1, keepdims=True)
    xc = x - mu
    var = jnp.mean(xc * xc, axis=-1, keepdims=True)
    return xc * lax.rsqrt(var + LN_EPS) * g + b


def _ln_kernel(x_ref, g_ref, b_ref, h_ref, hb_ref):
    y = _layer_norm(x_ref[...], g_ref[...], b_ref[...])
    h_ref[...] = y
    hb_ref[...] = y.astype(BF16)


def _ln_rows(x, g, b):
    t, d = x.shape
    tm = _pick_tile(t, 1024, 16)
    row = pl.BlockSpec((tm, d), lambda i: (i, 0))
    vec = pl.BlockSpec((1, d), lambda i: (0, 0))
    return pl.pallas_call(
        _ln_kernel,
        grid=(t // tm,),
        in_specs=[row, vec, vec],
        out_specs=[row, row],
        out_shape=[jax.ShapeDtypeStruct((t, d), F32), jax.ShapeDtypeStruct((t, d), BF16)],
        compiler_params=_params("parallel"),
        name="ln_in",
    )(x, g.reshape(1, d), b.reshape(1, d))


def _in_proj_kernel(x_ref, w_ref, wdt_ref, o_ref, dt_ref):
    x = x_ref[...]
    o_ref[...] = jnp.dot(x, w_ref[...], preferred_element_type=F32).astype(BF16)

    @pl.when(pl.program_id(1) == 0)
    def _():
        dt_ref[...] = jnp.dot(x, wdt_ref[...], preferred_element_type=F32)


def _in_proj(hb, w_main, w_dt):
    t, d = hb.shape
    tm = _pick_tile(t, 1664, 16)
    return pl.pallas_call(
        _in_proj_kernel,
        grid=(t // tm, PROJ_COLS // PROJ_TN),
        in_specs=[
            pl.BlockSpec((tm, d), lambda i, j: (i, 0)),
            pl.BlockSpec((d, PROJ_TN), lambda i, j: (0, j)),
            pl.BlockSpec((d, LANES), lambda i, j: (0, 0)),
        ],
        out_specs=[
            pl.BlockSpec((tm, PROJ_TN), lambda i, j: (i, j)),
            pl.BlockSpec((tm, LANES), lambda i, j: (i, 0)),
        ],
        out_shape=[jax.ShapeDtypeStruct((t, PROJ_COLS), BF16), jax.ShapeDtypeStruct((t, LANES), F32)],
        compiler_params=_params("parallel", "arbitrary"),
        name="in_proj",
    )(hb, w_main, w_dt)


def _pair_select(lane_lo, col, h0):
    return jnp.where(lane_lo, col[:, h0:h0 + 1], col[:, h0 + 1:h0 + 2])


def _ssd_kernel(xbc_ref, z_ref, dt_ref, cw_ref, cb_ref, dtb_ref, a_ref, dsk_ref, nw_ref, o_ref,
                tail_ref, prev_ref, xh_ref, bm_ref, cm_ref, y_ref, *, tb):
    j = pl.program_id(1)

    @pl.when(j == 0)
    def _():
        tail_ref[...] = jnp.zeros_like(tail_ref)
        prev_ref[...] = jnp.zeros_like(prev_ref)

    row_i = lax.broadcasted_iota(jnp.int32, (CHUNK, LANES), 0)
    row_c = lax.broadcasted_iota(jnp.int32, (CHUNK, 1), 0)
    lane_i = lax.broadcasted_iota(jnp.int32, (CHUNK, LANES), 1)
    lane_lo = lane_i < SSD_HEAD_DIM
    causal2 = row_i >= jnp.where(lane_lo, lane_i, lane_i - SSD_HEAD_DIM)
    lane_lo_row = lax.broadcasted_iota(jnp.int32, (1, LANES), 1) < SSD_HEAD_DIM
    ct_w = 512

    def chunk(c, carry):
        r0 = pl.multiple_of(c * CHUNK, CHUNK)
        valid = (j * tb + r0 + row_c) >= PAD_FRONT

        for ct in range(SSD_XBC // ct_w):
            cs = slice(ct * ct_w, (ct + 1) * ct_w)
            cur = xbc_ref[pl.ds(r0, CHUNK), cs].astype(F32)
            cur = jnp.where(valid, cur, 0.0)
            ext = jnp.concatenate([tail_ref[:, cs], cur], axis=0)
            acc = cb_ref[:, cs] + ext[SUBLANES:SUBLANES + CHUNK] * cw_ref[SSD_CONV - 1:SSD_CONV, cs]
            for k in range(SSD_CONV - 1):
                off = SUBLANES - (SSD_CONV - 1) + k
                acc = acc + ext[off:off + CHUNK] * cw_ref[k:k + 1, cs]
            act = _silu(acc)
            tail_ref[:, cs] = cur[CHUNK - SUBLANES:CHUNK]
            if ct * ct_w < SSD_INNER:
                xh_ref[:, cs] = act
            elif ct * ct_w < SSD_INNER + SSD_GN:
                bm_ref[:, ct * ct_w - SSD_INNER:(ct + 1) * ct_w - SSD_INNER] = act.astype(BF16)
            else:
                o0 = ct * ct_w - SSD_INNER - SSD_GN
                cm_ref[:, o0:o0 + ct_w] = act.astype(BF16)

        dtr = dt_ref[pl.ds(r0, CHUNK), :] + dtb_ref[...]
        dt = jnp.maximum(dtr, 0.0) + jnp.log1p(jnp.exp(-jnp.abs(dtr)))
        dt = jnp.where(valid, dt, 0.0)
        a_cs = dt * a_ref[...]
        for s in (1, 2, 4, 8, 16, 32):
            a_cs = a_cs + jnp.where(row_i >= s, pltpu.roll(a_cs, s, axis=0), 0.0)
        a_last = a_cs[CHUNK - 1:CHUNK, :]
        dte = jnp.exp(a_last - a_cs)
        ea = jnp.exp(a_cs)
        cdec = jnp.exp(a_last)
        a_t = jnp.concatenate([a_cs, a_cs], axis=0).T

        for g in range(SSD_GROUPS):
            gs = slice(g * SSD_GROUP_DIM, (g + 1) * SSD_GROUP_DIM)
            ns = slice(g * SSD_STATE, (g + 1) * SSD_STATE)
            bm = bm_ref[:, ns]
            cm = cm_ref[:, ns]
            prev = prev_ref[g]
            cb2 = lax.dot_general(cm, jnp.concatenate([bm, bm], axis=0),
                                  (((1,), (1,)), ((), ())), preferred_element_type=F32)
            yoff = jnp.dot(cm, prev.astype(BF16), preferred_element_type=F32)
            xs_parts = []
            cdec_parts = []
            for q in range(SSD_HPG // 2):
                h0 = g * SSD_HPG + 2 * q
                ps = slice(g * SSD_GROUP_DIM + q * LANES, g * SSD_GROUP_DIM + (q + 1) * LANES)
                xh = xh_ref[:, ps]
                xs = xh * _pair_select(lane_lo, dt, h0)
                xs_b = xs.astype(BF16)
                seg = _pair_select(lane_lo, a_cs, h0) - jnp.where(lane_lo_row, a_t[h0:h0 + 1, :], a_t[h0 + 1:h0 + 2, :])
                decay = jnp.exp(jnp.where(causal2, seg, -jnp.inf))
                m = (cb2 * decay).astype(BF16)
                zero = jnp.zeros_like(xs_b)
                rhs = jnp.concatenate([jnp.where(lane_lo, xs_b, zero), jnp.where(lane_lo, zero, xs_b)], axis=0)
                y_diag = jnp.dot(m, rhs, preferred_element_type=F32)
                y_off = yoff[:, q * LANES:(q + 1) * LANES] * _pair_select(lane_lo, ea, h0)
                y_ref[:, ps] = y_diag + y_off + dsk_ref[:, ps] * xh
                xs_parts.append((xs * _pair_select(lane_lo, dte, h0)).astype(BF16))
                cdec_parts.append(jnp.where(lane_lo_row, cdec[:, h0:h0 + 1], cdec[:, h0 + 1:h0 + 2]))
            xs_dec = jnp.concatenate(xs_parts, axis=1)
            st = lax.dot_general(bm, xs_dec, (((0,), (0,)), ((), ())), preferred_element_type=F32)
            prev_ref[g] = prev * jnp.concatenate(cdec_parts, axis=1) + st

            zg = z_ref[pl.ds(r0, CHUNK), gs].astype(F32)
            yg = y_ref[:, gs] * _silu(zg)
            ms = jnp.mean(yg * yg, axis=-1, keepdims=True)
            o_ref[pl.ds(r0, CHUNK), gs] = (yg * lax.rsqrt(ms + RMS_EPS) * nw_ref[:, gs]).astype(BF16)
        return carry

    lax.fori_loop(0, tb // CHUNK, chunk, 0)


def _conv_ssd(proj, dt_raw, conv_w, conv_b, dt_bias, a_neg, d_skip_ch, norm_w, batch, frame):
    t = proj.shape[0]
    tb = _pick_tile(frame, 832, CHUNK)
    nblk = frame // tb

    def vec(n):
        return pl.BlockSpec((1, n), lambda b, j: (0, 0))

    return pl.pallas_call(
        functools.partial(_ssd_kernel, tb=tb),
        grid=(batch, nblk),
        in_specs=[
            pl.BlockSpec((tb, SSD_XBC), lambda b, j: (b * nblk + j, PROJ_BLK_XBC)),
            pl.BlockSpec((tb, SSD_INNER), lambda b, j: (b * nblk + j, PROJ_BLK_Z)),
            pl.BlockSpec((tb, LANES), lambda b, j: (b * nblk + j, 0)),
            pl.BlockSpec((SSD_CONV, SSD_XBC), lambda b, j: (0, 0)),
            vec(SSD_XBC), vec(LANES), vec(LANES), vec(SSD_INNER), vec(SSD_INNER),
        ],
        out_specs=pl.BlockSpec((tb, SSD_INNER), lambda b, j: (b * nblk + j, 0)),
        out_shape=jax.ShapeDtypeStruct((t, SSD_INNER), BF16),
        scratch_shapes=[
            pltpu.VMEM((SUBLANES, SSD_XBC), F32),
            pltpu.VMEM((SSD_GROUPS, SSD_STATE, SSD_GROUP_DIM), F32),
            pltpu.VMEM((CHUNK, SSD_INNER), F32),
            pltpu.VMEM((CHUNK, SSD_GN), BF16),
            pltpu.VMEM((CHUNK, SSD_GN), BF16),
            pltpu.VMEM((CHUNK, SSD_INNER), F32),
        ],
        compiler_params=_params("parallel", "arbitrary"),
        name="conv_ssd",
    )(proj, proj, dt_raw, conv_w, conv_b, dt_bias, a_neg, d_skip_ch, norm_w)


def _route(logits):
    mx = jnp.max(logits, axis=0, keepdims=True)
    ex = jnp.exp(logits - mx)
    sc = ex / jnp.sum(ex, axis=0, keepdims=True)
    rows = [sc[e:e + 1, :] for e in range(N_EXPERTS)]
    best = None
    for g in range(N_EXPERT_GROUPS):
        a, b, c, d = rows[4 * g:4 * g + 4]
        hi1, lo1 = jnp.maximum(a, b), jnp.minimum(a, b)
        hi2, lo2 = jnp.maximum(c, d), jnp.minimum(c, d)
        gsc = jnp.maximum(hi1, hi2) + jnp.maximum(jnp.minimum(hi1, hi2), jnp.maximum(lo1, lo2))
        if best is None:
            best, sel = gsc, jnp.zeros_like(gsc)
        else:
            better = gsc > best
            sel = jnp.where(better, float(g), sel)
            best = jnp.maximum(best, gsc)
    v = []
    for i in range(EXPERTS_PER_GROUP):
        vi = rows[i]
        for g in range(1, N_EXPERT_GROUPS):
            vi = jnp.where(sel == float(g), rows[4 * g + i], vi)
        v.append(vi)

    def first_max(u):
        m = jnp.maximum(jnp.maximum(u[0], u[1]), jnp.maximum(u[2], u[3]))
        idx = jnp.where(u[0] == m, 0.0, jnp.where(u[1] == m, 1.0, jnp.where(u[2] == m, 2.0, 3.0)))
        return m, idx

    m1, i1 = first_max(v)
    u = [jnp.where(i1 == float(i), -1.0, v[i]) for i in range(EXPERTS_PER_GROUP)]
    m2, i2 = first_max(u)
    den = m1 + m2
    w1, w2 = m1 / den, m2 / den
    lo = jnp.minimum(i1, i2)
    hi = jnp.maximum(i1, i2)
    w_lo = jnp.where(i1 < i2, w1, w2)
    w_hi = jnp.where(i1 < i2, w2, w1)
    pair = jnp.where(lo == 0.0, hi - 1.0, jnp.where(lo == 1.0, hi + 1.0, 5.0))
    cls = sel * float(len(PAIRS)) + pair
    zero = jnp.zeros_like(cls)
    return jnp.concatenate([cls, w_lo, w_hi, zero, zero, zero, zero, zero], axis=0)


def _mix_kernel(yn_ref, gate_ref, pool_ref, h_ref, wso_ref, wp_ref, wo_ref, bg_ref, ps_ref, g_ref, b_ref,
                wr_hi_ref, wr_lo_ref, h1_ref, route_ref, halo_ref, *, tb):
    j = pl.program_id(1)

    @pl.when(j == 0)
    def _():
        halo_ref[0:POOL_HALO, :] = jnp.zeros((POOL_HALO, POOL_WIDTH), F32)

    frow = j * tb + lax.broadcasted_iota(jnp.int32, (tb, 1), 0)
    valid = frow >= PAD_FRONT
    pf = jnp.where(valid, pool_ref[...].astype(F32), 0.0)
    halo_ref[POOL_HALO:POOL_HALO + tb, :] = pf
    tpos = (frow - PAD_FRONT + 1).astype(F32)

    y_ssd = jnp.dot(yn_ref[...], wso_ref[...], preferred_element_type=F32)
    gates = gate_ref[...].astype(F32) + bg_ref[...]
    gates = 1.0 / (1.0 + jnp.exp(-gates))
    mixed = gates[:, :D_MODEL] * y_ssd

    pools = []
    for gi, win in enumerate(POOL_WINDOWS):
        cs = slice(gi * POOL_GROUP_DIM, (gi + 1) * POOL_GROUP_DIM)
        s = halo_ref[:, cs]
        step = 1
        while step < win:
            s = s[step:] + s[:-step]
            step *= 2
        wsum = s[s.shape[0] - tb:]
        cnt = jnp.clip(tpos, 1.0, float(win))
        pooled = wsum / cnt - pf[:, cs]
        pools.append(jnp.dot(pooled.astype(BF16), wp_ref[gi], preferred_element_type=F32))
    y_pool = jnp.concatenate(pools, axis=1) * ps_ref[...]
    halo_ref[0:POOL_HALO, :] = pf[tb - POOL_HALO:tb]
    mixed = mixed + gates[:, D_MODEL:] * y_pool

    out = jnp.dot(mixed.astype(BF16), wo_ref[...], preferred_element_type=F32)
    h1 = _layer_norm(DN_ALPHA * h_ref[...] + out, g_ref[...], b_ref[...])
    h1_ref[...] = h1

    h_hi = h1.astype(BF16)
    h_lo = (h1 - h_hi.astype(F32)).astype(BF16)
    nt = (((1,), (1,)), ((), ()))
    logits = (lax.dot_general(wr_hi_ref[...], h_hi, nt, preferred_element_type=F32)
              + lax.dot_general(wr_lo_ref[...], h_hi, nt, preferred_element_type=F32)
              + lax.dot_general(wr_hi_ref[...], h_lo, nt, preferred_element_type=F32))
    route_ref[0] = _route(logits)


def _mix(yn, proj, h, w_ssd_out, w_pool, w_out, b_gate, pool_scale, ln_g, ln_b, wr_hi, wr_lo, batch, frame):
    t = yn.shape[0]
    tb = _pick_tile(frame, 320, CHUNK)
    nblk = frame // tb
    d = D_MODEL

    def full(shape):
        return pl.BlockSpec(shape, lambda b, j: (0,) * len(shape))

    return pl.pallas_call(
        functools.partial(_mix_kernel, tb=tb),
        grid=(batch, nblk),
        in_specs=[
            pl.BlockSpec((tb, SSD_INNER), lambda b, j: (b * nblk + j, 0)),
            pl.BlockSpec((tb, N_BRANCH * d), lambda b, j: (b * nblk + j, PROJ_BLK_GATE)),
            pl.BlockSpec((tb, POOL_WIDTH), lambda b, j: (b * nblk + j, PROJ_BLK_POOL)),
            pl.BlockSpec((tb, d), lambda b, j: (b * nblk + j, 0)),
            full((SSD_INNER, d)), full((len(POOL_WINDOWS), POOL_GROUP_DIM, POOL_GROUP_DIM)), full((d, d)),
            full((1, N_BRANCH * d)), full((1, POOL_WIDTH)), full((1, d)), full((1, d)),
            full((N_EXPERTS, d)), full((N_EXPERTS, d)),
        ],
        out_specs=[
            pl.BlockSpec((tb, d), lambda b, j: (b * nblk + j, 0)),
            pl.BlockSpec((1, SUBLANES, tb), lambda b, j: (b * nblk + j, 0, 0)),
        ],
        out_shape=[jax.ShapeDtypeStruct((t, d), F32),
                   jax.ShapeDtypeStruct((batch * nblk, SUBLANES, tb), F32)],
        scratch_shapes=[pltpu.VMEM((POOL_HALO + tb, POOL_WIDTH), F32)],
        compiler_params=_params("parallel", "arbitrary"),
        name="mix",
    )(yn, proj, proj, h, w_ssd_out, w_pool, w_out, b_gate, pool_scale, ln_g, ln_b, wr_hi, wr_lo)


def _gather_rows(src_hbm, idx_ref, base, buf, sem, n):
    def body(r, carry):
        pltpu.make_async_copy(src_hbm.at[pl.ds(idx_ref[base + r], 1)], buf.at[pl.ds(r, 1)], sem).start()
        return carry
    lax.fori_loop(0, n, body, 0, unroll=8)


def _wait_rows(src_hbm, buf, sem, n):
    def body(r, carry):
        pltpu.make_async_copy(src_hbm.at[pl.ds(0, 1)], buf.at[pl.ds(r, 1)], sem).wait()
        return carry
    lax.fori_loop(0, n, body, 0, unroll=8)


def _expert_kernel(ea_ref, eb_ref, nu_ref, src_ref, h_hbm, wts_ref, wga_ref, wua_ref, wda_ref,
                   wgb_ref, wub_ref, wdb_ref, o_ref, xbuf, sem):
    i = pl.program_id(0)
    n_used = nu_ref[0]
    slot = i % 2

    @pl.when(i == 0)
    def _():
        _gather_rows(h_hbm, src_ref, 0, xbuf.at[0], sem.at[0], EXPERT_TM)

    @pl.when(i + 1 < n_used)
    def _():
        _gather_rows(h_hbm, src_ref, (i + 1) * EXPERT_TM, xbuf.at[1 - slot], sem.at[1 - slot], EXPERT_TM)

    @pl.when(i < n_used)
    def _():
        _wait_rows(h_hbm, xbuf.at[slot], sem.at[slot], EXPERT_TM)
        x = xbuf[slot].astype(BF16)

        def mlp(wg, wu, wd):
            hid = _silu(jnp.dot(x, wg[0], preferred_element_type=F32)) * jnp.dot(x, wu[0], preferred_element_type=F32)
            return jnp.dot(hid.astype(BF16), wd[0], preferred_element_type=F32)

        wts = wts_ref[...]
        o_ref[...] = wts[:, 0:1] * mlp(wga_ref, wua_ref, wda_ref) + wts[:, 1:2] * mlp(wgb_ref, wub_ref, wdb_ref)

    @pl.when(i >= n_used)
    def _():
        o_ref[...] = jnp.zeros_like(o_ref)


def _experts(h1, tile_ea, tile_eb, n_used, slot_src, slot_w, w_gate, w_up, w_down):
    t, d = h1.shape
    n_tiles = tile_ea.shape[0]
    n_slots = n_tiles * EXPERT_TM

    def wspec(shape, which):
        if which == 0:
            return pl.BlockSpec(shape, lambda i, ea, eb, nu, src: (ea[i], 0, 0))
        return pl.BlockSpec(shape, lambda i, ea, eb, nu, src: (eb[i], 0, 0))

    up = (1, d, D_EXPERT)
    down = (1, D_EXPERT, d)
    grid_spec = pltpu.PrefetchScalarGridSpec(
        num_scalar_prefetch=4,
        grid=(n_tiles,),
        in_specs=[
            pl.BlockSpec(memory_space=pl.ANY),
            pl.BlockSpec((EXPERT_TM, LANES), lambda i, ea, eb, nu, src: (i, 0)),
            wspec(up, 0), wspec(up, 0), wspec(down, 0),
            wspec(up, 1), wspec(up, 1), wspec(down, 1),
        ],
        out_specs=pl.BlockSpec((EXPERT_TM, d), lambda i, ea, eb, nu, src: (i, 0)),
        scratch_shapes=[pltpu.VMEM((2, EXPERT_TM, d), F32), pltpu.SemaphoreType.DMA((2,))],
    )
    return pl.pallas_call(
        _expert_kernel,
        grid_spec=grid_spec,
        out_shape=jax.ShapeDtypeStruct((n_slots, d), F32),
        compiler_params=_params("arbitrary"),
        name="experts",
    )(tile_ea, tile_eb, n_used, slot_src, h1, slot_w, w_gate, w_up, w_down, w_gate, w_up, w_down)


def _combine_kernel(pos_ref, h_ref, y_hbm, g_ref, b_ref, h2_ref, hb_ref, ybuf, sem, *, tm):
    i = pl.program_id(0)
    n = pl.num_programs(0)
    slot = i % 2

    @pl.when(i == 0)
    def _():
        _gather_rows(y_hbm, pos_ref, 0, ybuf.at[0], sem.at[0], tm)

    @pl.when(i + 1 < n)
    def _():
        _gather_rows(y_hbm, pos_ref, (i + 1) * tm, ybuf.at[1 - slot], sem.at[1 - slot], tm)

    _wait_rows(y_hbm, ybuf.at[slot], sem.at[slot], tm)
    h2 = _layer_norm(DN_ALPHA * h_ref[...] + ybuf[slot], g_ref[...], b_ref[...])
    h2_ref[...] = h2
    hb_ref[...] = h2.astype(BF16)


def _combine(h1, y_slots, pos, ln_g, ln_b):
    t, d = h1.shape
    tm = _pick_tile(t, 256, 16)
    row = pl.BlockSpec((tm, d), lambda i, pos: (i, 0))
    vec = pl.BlockSpec((1, d), lambda i, pos: (0, 0))
    grid_spec = pltpu.PrefetchScalarGridSpec(
        num_scalar_prefetch=1,
        grid=(t // tm,),
        in_specs=[row, pl.BlockSpec(memory_space=pl.ANY), vec, vec],
        out_specs=[row, row],
        scratch_shapes=[pltpu.VMEM((2, tm, d), F32), pltpu.SemaphoreType.DMA((2,))],
    )
    return pl.pallas_call(
        functools.partial(_combine_kernel, tm=tm),
        grid_spec=grid_spec,
        out_shape=[jax.ShapeDtypeStruct((t, d), F32), jax.ShapeDtypeStruct((t, d), BF16)],
        compiler_params=_params("arbitrary"),
        name="combine",
    )(pos, h1, y_slots, ln_g, ln_b)


_PAIR_LO = np.array([p[0] for p in PAIRS], np.int32)
_PAIR_HI = np.array([p[1] for p in PAIRS], np.int32)


def _dispatch_plan(route, t):
    route = jnp.transpose(route, (1, 0, 2)).reshape(SUBLANES, t)
    cls = route[0].astype(jnp.int32)
    onehot = (cls[:, None] == jnp.arange(N_CLASSES, dtype=jnp.int32)[None, :]).astype(jnp.int32)
    csum = jnp.cumsum(onehot, axis=0)
    counts = csum[-1]
    rank = jnp.sum(onehot * csum, axis=1) - 1
    tiles_per = (counts + EXPERT_TM - 1) // EXPERT_TM
    tile_end = jnp.cumsum(tiles_per)
    cls_start = (tile_end - tiles_per) * EXPERT_TM
    pos = cls_start[cls] + rank
    n_tiles = -(-t // EXPERT_TM) + N_CLASSES
    n_used = tile_end[-1]
    tile_ids = jnp.minimum(jnp.arange(n_tiles, dtype=jnp.int32), n_used - 1)
    tile_cls = jnp.sum((tile_ids[:, None] >= tile_end[None, :]).astype(jnp.int32), axis=1)
    grp = tile_cls // len(PAIRS)
    pair = tile_cls % len(PAIRS)
    tile_ea = grp * EXPERTS_PER_GROUP + jnp.asarray(_PAIR_LO)[pair]
    tile_eb = grp * EXPERTS_PER_GROUP + jnp.asarray(_PAIR_HI)[pair]
    n_slots = n_tiles * EXPERT_TM
    slot_src = jnp.zeros((n_slots,), jnp.int32).at[pos].set(jnp.arange(t, dtype=jnp.int32))
    slot_w = jnp.zeros((n_slots, LANES), F32).at[pos, 0].set(route[1]).at[pos, 1].set(route[2])
    return pos, tile_ea, tile_eb, n_used.reshape(1).astype(jnp.int32), slot_src, slot_w


def kernel(x, meta_tokens, ln_in_g, ln_in_b, w_router, w_in, conv_w, conv_b, dt_bias, a_log, d_skip,
           ssd_norm_w, w_ssd_out, w_pool, pool_scale, b_gate, w_out, ln1_g, ln1_b, w_exp_gate, w_exp_up,
           w_exp_down, ln2_g, ln2_b):
    batch, seq, d = x.shape
    frame = FRAME_HEAD + seq
    assert d == D_MODEL and frame % CHUNK == 0
    t = batch * frame
    depth = w_in.shape[0]

    head = jnp.concatenate([jnp.zeros((PAD_FRONT, d), x.dtype), meta_tokens.astype(x.dtype)], axis=0)
    tokens = jnp.concatenate([jnp.broadcast_to(head[None], (batch, FRAME_HEAD, d)), x], axis=1).reshape(t, d)
    h, hb = _ln_rows(tokens, ln_in_g, ln_in_b)

    wr_t = w_router.T
    wr_hi = wr_t.astype(BF16)
    wr_lo = (wr_t - wr_hi.astype(F32)).astype(BF16)

    def pad_lanes(v):
        return jnp.pad(v, (0, LANES - v.shape[0])).reshape(1, LANES)

    for l in range(depth):
        w = w_in[l]
        w_main = jnp.concatenate(
            [w[:, COL_XBC:COL_DT], w[:, COL_Z:COL_XBC], w[:, COL_GATE:], w[:, COL_POOL:COL_GATE]], axis=1).astype(BF16)
        w_dt = jnp.pad(w[:, COL_DT:COL_POOL], ((0, 0), (0, LANES - SSD_HEADS))).astype(BF16)
        proj, dt_raw = _in_proj(hb, w_main, w_dt)

        yn = _conv_ssd(
            proj, dt_raw, conv_w[l], conv_b[l].reshape(1, SSD_XBC), pad_lanes(dt_bias[l]),
            pad_lanes(-jnp.exp(a_log[l])), jnp.repeat(d_skip[l], SSD_HEAD_DIM).reshape(1, SSD_INNER),
            ssd_norm_w[l].reshape(1, SSD_INNER), batch, frame)

        h1, route = _mix(
            yn, proj, h, w_ssd_out[l].astype(BF16), w_pool[l].astype(BF16), w_out[l].astype(BF16),
            b_gate[l].reshape(1, N_BRANCH * d), pool_scale[l].reshape(1, POOL_WIDTH),
            ln1_g[l].reshape(1, d), ln1_b[l].reshape(1, d), wr_hi, wr_lo, batch, frame)

        pos, tile_ea, tile_eb, n_used, slot_src, slot_w = _dispatch_plan(route, t)
        y_slots = _experts(h1, tile_ea, tile_eb, n_used, slot_src, slot_w,
                           w_exp_gate[l].astype(BF16), w_exp_up[l].astype(BF16), w_exp_down[l].astype(BF16))
        h, hb = _combine(h1, y_slots, pos, ln2_g[l].reshape(1, d), ln2_b[l].reshape(1, d))

    return h.reshape(batch, frame, d)[:, FRAME_HEAD:]
```

```python
import functools

import jax
import jax.numpy as jnp
import numpy as np
from jax import lax
from jax.experimental import pallas as pl
from jax.experimental.pallas import tpu as pltpu

F32 = jnp.float32
BF16 = jnp.bfloat16

D_MODEL = 1024
CHUNK = 64
N_META = 16
PAD_FRONT = CHUNK - N_META
FRAME_HEAD = PAD_FRONT + N_META

SSD_INNER = 2048
SSD_HEAD_DIM = 64
SSD_HEADS = 32
SSD_GROUPS = 8
SSD_HPG = 4
SSD_STATE = 128
SSD_CONV = 4
SSD_GN = SSD_GROUPS * SSD_STATE
SSD_XBC = SSD_INNER + 2 * SSD_GN
SSD_GROUP_DIM = SSD_HPG * SSD_HEAD_DIM

POOL_WIDTH = 1024
POOL_WINDOWS = (2, 4, 8, 16)
POOL_GROUP_DIM = 256
POOL_HALO = 16

N_BRANCH = 2
COL_Z = 0
COL_XBC = COL_Z + SSD_INNER
COL_DT = COL_XBC + SSD_XBC
COL_POOL = COL_DT + SSD_HEADS
COL_GATE = COL_POOL + POOL_WIDTH
IN_COLS = COL_GATE + N_BRANCH * D_MODEL

N_EXPERTS = 16
N_EXPERT_GROUPS = 4
EXPERTS_PER_GROUP = 4
D_EXPERT = 512
PAIRS = ((0, 1), (0, 2), (0, 3), (1, 2), (1, 3), (2, 3))
N_CLASSES = N_EXPERT_GROUPS * len(PAIRS)

DEPTH = 2
DN_ALPHA = (2.0 * DEPTH) ** 0.25
LN_EPS = 1e-5
RMS_EPS = 1e-5

LANES = 128
SUBLANES = 8
VMEM_LIMIT = 56 * 1024 * 1024

PROJ_TN = 1024
PROJ_COLS = SSD_XBC + SSD_INNER + N_BRANCH * D_MODEL + POOL_WIDTH
PROJ_BLK_XBC = 0
PROJ_BLK_Z = SSD_XBC // SSD_INNER
PROJ_BLK_GATE = PROJ_BLK_Z + 1
PROJ_BLK_POOL = (SSD_XBC + SSD_INNER + N_BRANCH * D_MODEL) // POOL_WIDTH

EXPERT_TM = 128


def _pick_tile(n, target, mult):
    best = None
    for t in range(mult, min(n, target) + 1, mult):
        if n % t == 0:
            best = t
    assert best is not None, (n, target, mult)
    return best


def _params(*sem):
    return pltpu.CompilerParams(dimension_semantics=sem, vmem_limit_bytes=VMEM_LIMIT)


def _sigmoid(x):
    return 0.5 + 0.5 * jnp.tanh(0.5 * x)


def _silu(x):
    hx = 0.5 * x
    return hx + hx * jnp.tanh(hx)


def _layer_norm(x, g, b):
    mu = jnp.mean(x, axis=-1, keepdims=True)
    xc = x - mu
    var = jnp.mean(xc * xc, axis=-1, keepdims=True)
    return xc * lax.rsqrt(var + LN_EPS) * g + b


def _ln_kernel(x_ref, m_ref, g_ref, b_ref, h_ref, hb_ref):
    y = _layer_norm(x_ref[...], g_ref[...], b_ref[...])
    h_ref[...] = y
    hb_ref[...] = (y * m_ref[...]).astype(BF16)


def _ln_rows(x, rowmask, g, b):
    t, d = x.shape
    tm = _pick_tile(t, 1024, 16)
    row = pl.BlockSpec((tm, d), lambda i: (i, 0))
    vec = pl.BlockSpec((1, d), lambda i: (0, 0))
    return pl.pallas_call(
        _ln_kernel,
        grid=(t // tm,),
        in_specs=[row, pl.BlockSpec((tm, 1), lambda i: (i, 0)), vec, vec],
        out_specs=[row, row],
        out_shape=[jax.ShapeDtypeStruct((t, d), F32), jax.ShapeDtypeStruct((t, d), BF16)],
        compiler_params=_params("parallel"),
        name="ln_in",
    )(x, rowmask, g.reshape(1, d), b.reshape(1, d))


def _in_proj_kernel(x_ref, w_ref, wdt_ref, o_ref, dt_ref):
    x = x_ref[...]
    o_ref[...] = jnp.dot(x, w_ref[...], preferred_element_type=F32).astype(BF16)

    @pl.when(pl.program_id(1) == 0)
    def _():
        dt_ref[...] = jnp.dot(x, wdt_ref[...], preferred_element_type=F32)


def _in_proj(hb, w_main, w_dt):
    t, d = hb.shape
    tm = _pick_tile(t, 1664, 16)
    return pl.pallas_call(
        _in_proj_kernel,
        grid=(t // tm, PROJ_COLS // PROJ_TN),
        in_specs=[
            pl.BlockSpec((tm, d), lambda i, j: (i, 0)),
            pl.BlockSpec((d, PROJ_TN), lambda i, j: (0, j)),
            pl.BlockSpec((d, LANES), lambda i, j: (0, 0)),
        ],
        out_specs=[
            pl.BlockSpec((tm, PROJ_TN), lambda i, j: (i, j)),
            pl.BlockSpec((tm, LANES), lambda i, j: (i, 0)),
        ],
        out_shape=[jax.ShapeDtypeStruct((t, PROJ_COLS), BF16), jax.ShapeDtypeStruct((t, LANES), F32)],
        compiler_params=_params("parallel", "arbitrary"),
        name="in_proj",
    )(hb, w_main, w_dt)


def _pair_select(lane_lo, col, h0):
    return jnp.where(lane_lo, col[:, h0:h0 + 1], col[:, h0 + 1:h0 + 2])


def _ssd_kernel(xbc_ref, z_ref, dt_ref, cw_ref, cb_ref, dtb_ref, a_ref, dsk_ref, nw_ref, o_ref,
                tail_ref, prev_ref, xh_ref, bm_ref, cm_ref, y_ref, *, tb):
    j = pl.program_id(1)

    @pl.when(j == 0)
    def _():
        tail_ref[...] = jnp.zeros_like(tail_ref)
        prev_ref[...] = jnp.zeros_like(prev_ref)

    row_i = lax.broadcasted_iota(jnp.int32, (CHUNK, LANES), 0)
    row_c = lax.broadcasted_iota(jnp.int32, (CHUNK, 1), 0)
    lane_i = lax.broadcasted_iota(jnp.int32, (CHUNK, LANES), 1)
    lane_lo = lane_i < SSD_HEAD_DIM
    causal2 = row_i >= jnp.where(lane_lo, lane_i, lane_i - SSD_HEAD_DIM)
    lane_lo_row = lax.broadcasted_iota(jnp.int32, (1, LANES), 1) < SSD_HEAD_DIM
    ct_w = 512

    def chunk(c, carry):
        r0 = pl.multiple_of(c * CHUNK, CHUNK)
        valid = (j * tb + r0 + row_c) >= PAD_FRONT

        for ct in range(SSD_XBC // ct_w):
            cs = slice(ct * ct_w, (ct + 1) * ct_w)
            cur = xbc_ref[pl.ds(r0, CHUNK), cs].astype(F32)
            ext = jnp.concatenate([tail_ref[:, cs], cur], axis=0)
            acc = cb_ref[:, cs] + ext[SUBLANES:SUBLANES + CHUNK] * cw_ref[SSD_CONV - 1:SSD_CONV, cs]
            for k in range(SSD_CONV - 1):
                off = SUBLANES - (SSD_CONV - 1) + k
                acc = acc + ext[off:off + CHUNK] * cw_ref[k:k + 1, cs]
            act = _silu(acc)
            tail_ref[:, cs] = cur[CHUNK - SUBLANES:CHUNK]
            if ct * ct_w < SSD_INNER:
                xh_ref[:, cs] = act
            elif ct * ct_w < SSD_INNER + SSD_GN:
                bm_ref[:, ct * ct_w - SSD_INNER:(ct + 1) * ct_w - SSD_INNER] = act.astype(BF16)
            else:
                o0 = ct * ct_w - SSD_INNER - SSD_GN
                cm_ref[:, o0:o0 + ct_w] = act.astype(BF16)

        dtr = dt_ref[pl.ds(r0, CHUNK), :] + dtb_ref[...]
        dt = jnp.maximum(dtr, 0.0) + jnp.log1p(jnp.exp(-jnp.abs(dtr)))
        dt = jnp.where(valid, dt, 0.0)
        a_cs = dt * a_ref[...]
        for s in (1, 2, 4, 8, 16, 32):
            a_cs = a_cs + jnp.where(row_i >= s, pltpu.roll(a_cs, s, axis=0), 0.0)
        a_last = a_cs[CHUNK - 1:CHUNK, :]
        dtw = dt * jnp.exp(a_last - a_cs)
        cdec = jnp.exp(a_last)
        a_t = jnp.concatenate([a_cs, a_cs], axis=0).T
        dt_t = jnp.concatenate([dt, dt], axis=0).T

        for g in range(SSD_GROUPS):
            gs = slice(g * SSD_GROUP_DIM, (g + 1) * SSD_GROUP_DIM)
            ns = slice(g * SSD_STATE, (g + 1) * SSD_STATE)
            bm = bm_ref[:, ns]
            cm = cm_ref[:, ns]
            prev = prev_ref[g]
            cb2 = lax.dot_general(cm, jnp.concatenate([bm, bm], axis=0),
                                  (((1,), (1,)), ((), ())), preferred_element_type=F32)
            yoff = jnp.dot(cm, prev.astype(BF16), preferred_element_type=F32)
            xs_parts = []
            cdec_parts = []
            for q in range(SSD_HPG // 2):
                h0 = g * SSD_HPG + 2 * q
                ps = slice(g * SSD_GROUP_DIM + q * LANES, g * SSD_GROUP_DIM + (q + 1) * LANES)
                xh = xh_ref[:, ps]
                xh_b = xh.astype(BF16)
                a_col = _pair_select(lane_lo, a_cs, h0)
                a_row = jnp.where(lane_lo_row, a_t[h0:h0 + 1, :], a_t[h0 + 1:h0 + 2, :])
                dt_row = jnp.where(lane_lo_row, dt_t[h0:h0 + 1, :], dt_t[h0 + 1:h0 + 2, :])
                decay = jnp.exp(jnp.where(causal2, a_col - a_row, -jnp.inf))
                m = (cb2 * decay * dt_row).astype(BF16)
                zero = jnp.zeros_like(xh_b)
                rhs = jnp.concatenate([jnp.where(lane_lo, xh_b, zero), jnp.where(lane_lo, zero, xh_b)], axis=0)
                y_diag = jnp.dot(m, rhs, preferred_element_type=F32)
                y_off = yoff[:, q * LANES:(q + 1) * LANES] * jnp.exp(a_col)
                y_ref[:, ps] = y_diag + y_off + dsk_ref[:, ps] * xh
                xs_parts.append((xh * _pair_select(lane_lo, dtw, h0)).astype(BF16))
                cdec_parts.append(jnp.where(lane_lo_row, cdec[:, h0:h0 + 1], cdec[:, h0 + 1:h0 + 2]))
            xs_dec = jnp.concatenate(xs_parts, axis=1)
            st = lax.dot_general(bm, xs_dec, (((0,), (0,)), ((), ())), preferred_element_type=F32)
            prev_ref[g] = prev * jnp.concatenate(cdec_parts, axis=1) + st

            zg = z_ref[pl.ds(r0, CHUNK), gs].astype(F32)
            yg = y_ref[:, gs] * _silu(zg)
            ms = jnp.mean(yg * yg, axis=-1, keepdims=True)
            o_ref[pl.ds(r0, CHUNK), gs] = (yg * lax.rsqrt(ms + RMS_EPS) * nw_ref[:, gs]).astype(BF16)
        return carry

    lax.fori_loop(0, tb // CHUNK, chunk, 0)


def _conv_ssd(proj, dt_raw, conv_w, conv_b, dt_bias, a_neg, d_skip_ch, norm_w, batch, frame):
    t = proj.shape[0]
    tb = _pick_tile(frame, 832, CHUNK)
    nblk = frame // tb

    def vec(n):
        return pl.BlockSpec((1, n), lambda b, j: (0, 0))

    return pl.pallas_call(
        functools.partial(_ssd_kernel, tb=tb),
        grid=(batch, nblk),
        in_specs=[
            pl.BlockSpec((tb, SSD_XBC), lambda b, j: (b * nblk + j, PROJ_BLK_XBC)),
            pl.BlockSpec((tb, SSD_INNER), lambda b, j: (b * nblk + j, PROJ_BLK_Z)),
            pl.BlockSpec((tb, LANES), lambda b, j: (b * nblk + j, 0)),
            pl.BlockSpec((SSD_CONV, SSD_XBC), lambda b, j: (0, 0)),
            vec(SSD_XBC), vec(LANES), vec(LANES), vec(SSD_INNER), vec(SSD_INNER),
        ],
        out_specs=pl.BlockSpec((tb, SSD_INNER), lambda b, j: (b * nblk + j, 0)),
        out_shape=jax.ShapeDtypeStruct((t, SSD_INNER), BF16),
        scratch_shapes=[
            pltpu.VMEM((SUBLANES, SSD_XBC), F32),
            pltpu.VMEM((SSD_GROUPS, SSD_STATE, SSD_GROUP_DIM), F32),
            pltpu.VMEM((CHUNK, SSD_INNER), F32),
            pltpu.VMEM((CHUNK, SSD_GN), BF16),
            pltpu.VMEM((CHUNK, SSD_GN), BF16),
            pltpu.VMEM((CHUNK, SSD_INNER), F32),
        ],
        compiler_params=_params("parallel", "arbitrary"),
        name="conv_ssd",
    )(proj, proj, dt_raw, conv_w, conv_b, dt_bias, a_neg, d_skip_ch, norm_w)


def _route(logits):
    mx = jnp.max(logits, axis=0, keepdims=True)
    ex = jnp.exp(logits - mx)
    sc = ex / jnp.sum(ex, axis=0, keepdims=True)
    rows = [sc[e:e + 1, :] for e in range(N_EXPERTS)]
    best = None
    for g in range(N_EXPERT_GROUPS):
        a, b, c, d = rows[4 * g:4 * g + 4]
        hi1, lo1 = jnp.maximum(a, b), jnp.minimum(a, b)
        hi2, lo2 = jnp.maximum(c, d), jnp.minimum(c, d)
        gsc = jnp.maximum(hi1, hi2) + jnp.maximum(jnp.minimum(hi1, hi2), jnp.maximum(lo1, lo2))
        if best is None:
            best, sel = gsc, jnp.zeros_like(gsc)
        else:
            better = gsc > best
            sel = jnp.where(better, float(g), sel)
            best = jnp.maximum(best, gsc)
    v = []
    for i in range(EXPERTS_PER_GROUP):
        vi = rows[i]
        for g in range(1, N_EXPERT_GROUPS):
            vi = jnp.where(sel == float(g), rows[4 * g + i], vi)
        v.append(vi)

    def first_max(u):
        m = jnp.maximum(jnp.maximum(u[0], u[1]), jnp.maximum(u[2], u[3]))
        idx = jnp.where(u[0] == m, 0.0, jnp.where(u[1] == m, 1.0, jnp.where(u[2] == m, 2.0, 3.0)))
        return m, idx

    m1, i1 = first_max(v)
    u = [jnp.where(i1 == float(i), -1.0, v[i]) for i in range(EXPERTS_PER_GROUP)]
    m2, i2 = first_max(u)
    den = m1 + m2
    w1, w2 = m1 / den, m2 / den
    lo = jnp.minimum(i1, i2)
    hi = jnp.maximum(i1, i2)
    w_lo = jnp.where(i1 < i2, w1, w2)
    w_hi = jnp.where(i1 < i2, w2, w1)
    pair = jnp.where(lo == 0.0, hi - 1.0, jnp.where(lo == 1.0, hi + 1.0, 5.0))
    cls = sel * float(len(PAIRS)) + pair
    zero = jnp.zeros_like(cls)
    return jnp.concatenate([cls, w_lo, w_hi, zero, zero, zero, zero, zero], axis=0)


def _mix_kernel(yn_ref, gate_ref, pool_ref, h_ref, wso_ref, wp_ref, wo_ref, bg_ref, ps_ref, g_ref, b_ref,
                wr_hi_ref, wr_lo_ref, h1_ref, route_ref, halo_ref, *, tb):
    j = pl.program_id(1)

    @pl.when(j == 0)
    def _():
        halo_ref[0:POOL_HALO, :] = jnp.zeros((POOL_HALO, POOL_WIDTH), F32)

    frow = j * tb + lax.broadcasted_iota(jnp.int32, (tb, 1), 0)
    pf = pool_ref[...].astype(F32)
    halo_ref[POOL_HALO:POOL_HALO + tb, :] = pf
    tpos = (frow - PAD_FRONT + 1).astype(F32)

    y_ssd = jnp.dot(yn_ref[...], wso_ref[...], preferred_element_type=F32)
    gates = _sigmoid(gate_ref[...].astype(F32) + bg_ref[...])
    mixed = gates[:, :D_MODEL] * y_ssd

    pools = []
    for gi, win in enumerate(POOL_WINDOWS):
        cs = slice(gi * POOL_GROUP_DIM, (gi + 1) * POOL_GROUP_DIM)
        s = halo_ref[:, cs]
        step = 1
        while step < win:
            s = s[step:] + s[:-step]
            step *= 2
        wsum = s[s.shape[0] - tb:]
        cnt = jnp.clip(tpos, 1.0, float(win))
        pooled = wsum / cnt - pf[:, cs]
        pools.append(jnp.dot(pooled.astype(BF16), wp_ref[gi], preferred_element_type=F32))
    y_pool = jnp.concatenate(pools, axis=1) * ps_ref[...]
    halo_ref[0:POOL_HALO, :] = pf[tb - POOL_HALO:tb]
    mixed = mixed + gates[:, D_MODEL:] * y_pool

    out = jnp.dot(mixed.astype(BF16), wo_ref[...], preferred_element_type=F32)
    h1 = _layer_norm(DN_ALPHA * h_ref[...] + out, g_ref[...], b_ref[...])
    h1_ref[...] = h1

    h_hi = h1.astype(BF16)
    h_lo = (h1 - h_hi.astype(F32)).astype(BF16)
    nt = (((1,), (1,)), ((), ()))
    logits = (lax.dot_general(wr_hi_ref[...], h_hi, nt, preferred_element_type=F32)
              + lax.dot_general(wr_lo_ref[...], h_hi, nt, preferred_element_type=F32)
              + lax.dot_general(wr_hi_ref[...], h_lo, nt, preferred_element_type=F32))
    route_ref[0] = _route(logits)


def _mix(yn, proj, h, w_ssd_out, w_pool, w_out, b_gate, pool_scale, ln_g, ln_b, wr_hi, wr_lo, batch, frame):
    t = yn.shape[0]
    tb = _pick_tile(frame, 320, CHUNK)
    nblk = frame // tb
    d = D_MODEL

    def full(shape):
        return pl.BlockSpec(shape, lambda b, j: (0,) * len(shape))

    return pl.pallas_call(
        functools.partial(_mix_kernel, tb=tb),
        grid=(batch, nblk),
        in_specs=[
            pl.BlockSpec((tb, SSD_INNER), lambda b, j: (b * nblk + j, 0)),
            pl.BlockSpec((tb, N_BRANCH * d), lambda b, j: (b * nblk + j, PROJ_BLK_GATE)),
            pl.BlockSpec((tb, POOL_WIDTH), lambda b, j: (b * nblk + j, PROJ_BLK_POOL)),
            pl.BlockSpec((tb, d), lambda b, j: (b * nblk + j, 0)),
            full((SSD_INNER, d)), full((len(POOL_WINDOWS), POOL_GROUP_DIM, POOL_GROUP_DIM)), full((d, d)),
            full((1, N_BRANCH * d)), full((1, POOL_WIDTH)), full((1, d)), full((1, d)),
            full((N_EXPERTS, d)), full((N_EXPERTS, d)),
        ],
        out_specs=[
            pl.BlockSpec((tb, d), lambda b, j: (b * nblk + j, 0)),
            pl.BlockSpec((1, SUBLANES, tb), lambda b, j: (b * nblk + j, 0, 0)),
        ],
        out_shape=[jax.ShapeDtypeStruct((t, d), F32),
                   jax.ShapeDtypeStruct((batch * nblk, SUBLANES, tb), F32)],
        scratch_shapes=[pltpu.VMEM((POOL_HALO + tb, POOL_WIDTH), F32)],
        compiler_params=_params("parallel", "arbitrary"),
        name="mix",
    )(yn, proj, proj, h, w_ssd_out, w_pool, w_out, b_gate, pool_scale, ln_g, ln_b, wr_hi, wr_lo)


def _gather_rows(src_hbm, idx_ref, base, buf, sem, n):
    def body(r, carry):
        pltpu.make_async_copy(src_hbm.at[pl.ds(idx_ref[base + r], 1)], buf.at[pl.ds(r, 1)], sem).start()
        return carry
    lax.fori_loop(0, n, body, 0, unroll=8)


def _wait_rows(src_hbm, buf, sem, n):
    def body(r, carry):
        pltpu.make_async_copy(src_hbm.at[pl.ds(0, 1)], buf.at[pl.ds(r, 1)], sem).wait()
        return carry
    lax.fori_loop(0, n, body, 0, unroll=8)


def _scatter_rows(buf, idx_ref, base, dst_hbm, sem, n):
    def body(r, carry):
        pltpu.make_async_copy(buf.at[pl.ds(r, 1)], dst_hbm.at[pl.ds(idx_ref[base + r], 1)], sem).start()
        return carry
    lax.fori_loop(0, n, body, 0, unroll=8)


def _wait_scattered(buf, dst_hbm, sem, n):
    def body(r, carry):
        pltpu.make_async_copy(buf.at[pl.ds(r, 1)], dst_hbm.at[pl.ds(0, 1)], sem).wait()
        return carry
    lax.fori_loop(0, n, body, 0, unroll=8)


def _dispatch_kernel(pos_ref, zt_ref, zon_ref, nu_ref, h_ref, w_ref, x_hbm, stage, zbuf, sem, zsem, *,
                     tm, n, n_tiles):
    i = pl.program_id(0)
    slot = i % 2

    @pl.when(i == 0)
    def _():
        zbuf[...] = jnp.zeros_like(zbuf)

        def zero_tile(tile):
            row0 = pl.multiple_of(tile * EXPERT_TM, EXPERT_TM)
            cp = pltpu.make_async_copy(zbuf, x_hbm.at[pl.ds(row0, EXPERT_TM)], zsem)
            cp.start()
            cp.wait()

        for c in range(N_CLASSES):
            pl.when(zon_ref[c] > 0)(functools.partial(zero_tile, zt_ref[c]))
            pl.when(nu_ref[0] + c < n_tiles)(functools.partial(zero_tile, nu_ref[0] + c))

    @pl.when(i >= 2)
    def _():
        _wait_scattered(stage.at[slot], x_hbm, sem.at[slot], tm)

    stage[slot, :, 0:D_MODEL] = h_ref[...]
    stage[slot, :, D_MODEL:D_MODEL + LANES] = w_ref[...]
    _scatter_rows(stage.at[slot], pos_ref, i * tm, x_hbm, sem.at[slot], tm)

    @pl.when(i == n - 1)
    def _():
        if n >= 2:
            _wait_scattered(stage.at[1 - slot], x_hbm, sem.at[1 - slot], tm)
        _wait_scattered(stage.at[slot], x_hbm, sem.at[slot], tm)


def _dispatch(h1, wcols, pos, zero_tile, zero_on, n_used, n_slots):
    t, d = h1.shape
    tm = _pick_tile(t, 512, 16)
    grid_spec = pltpu.PrefetchScalarGridSpec(
        num_scalar_prefetch=4,
        grid=(t // tm,),
        in_specs=[pl.BlockSpec((tm, d), lambda i, *_: (i, 0)), pl.BlockSpec((tm, LANES), lambda i, *_: (i, 0))],
        out_specs=pl.BlockSpec(memory_space=pl.ANY),
        scratch_shapes=[pltpu.VMEM((2, tm, d + LANES), F32), pltpu.VMEM((EXPERT_TM, d + LANES), F32),
                        pltpu.SemaphoreType.DMA((2,)), pltpu.SemaphoreType.DMA(())],
    )
    return pl.pallas_call(
        functools.partial(_dispatch_kernel, tm=tm, n=t // tm, n_tiles=n_slots // EXPERT_TM),
        grid_spec=grid_spec,
        out_shape=jax.ShapeDtypeStruct((n_slots, d + LANES), F32),
        compiler_params=_params("arbitrary"),
        name="dispatch",
    )(pos, zero_tile, zero_on, n_used, h1, wcols)


def _expert_kernel(ea_ref, eb_ref, nu_ref, x_ref, wga_ref, wua_ref, wda_ref, wgb_ref, wub_ref, wdb_ref, o_ref):
    @pl.when(pl.program_id(0) < nu_ref[0])
    def _():
        x = x_ref[:, 0:D_MODEL].astype(BF16)
        wts = x_ref[:, D_MODEL:D_MODEL + LANES]

        def mlp(wg, wu, wd):
            hid = _silu(jnp.dot(x, wg[0], preferred_element_type=F32)) * jnp.dot(x, wu[0], preferred_element_type=F32)
            return jnp.dot(hid.astype(BF16), wd[0], preferred_element_type=F32)

        o_ref[...] = wts[:, 0:1] * mlp(wga_ref, wua_ref, wda_ref) + wts[:, 1:2] * mlp(wgb_ref, wub_ref, wdb_ref)

    @pl.when(pl.program_id(0) >= nu_ref[0])
    def _():
        o_ref[...] = jnp.zeros_like(o_ref)


def _experts(x_slots, tile_ea, tile_eb, n_used, w_gate, w_up, w_down):
    n_slots, dw = x_slots.shape
    d = D_MODEL
    n_tiles = n_slots // EXPERT_TM

    def wspec(shape, which):
        if which == 0:
            return pl.BlockSpec(shape, lambda i, ea, eb, nu: (ea[i], 0, 0))
        return pl.BlockSpec(shape, lambda i, ea, eb, nu: (eb[i], 0, 0))

    up = (1, d, D_EXPERT)
    down = (1, D_EXPERT, d)
    grid_spec = pltpu.PrefetchScalarGridSpec(
        num_scalar_prefetch=3,
        grid=(n_tiles,),
        in_specs=[
            pl.BlockSpec((EXPERT_TM, dw), lambda i, ea, eb, nu: (i, 0)),
            wspec(up, 0), wspec(up, 0), wspec(down, 0),
            wspec(up, 1), wspec(up, 1), wspec(down, 1),
        ],
        out_specs=pl.BlockSpec((EXPERT_TM, d), lambda i, ea, eb, nu: (i, 0)),
    )
    return pl.pallas_call(
        _expert_kernel,
        grid_spec=grid_spec,
        out_shape=jax.ShapeDtypeStruct((n_slots, d), F32),
        compiler_params=_params("arbitrary"),
        name="experts",
    )(tile_ea, tile_eb, n_used, x_slots, w_gate, w_up, w_down, w_gate, w_up, w_down)


def _combine_kernel(pos_ref, h_ref, m_ref, y_hbm, g_ref, b_ref, h2_ref, hb_ref, ybuf, sem, *, tm):
    i = pl.program_id(0)
    n = pl.num_programs(0)
    slot = i % 2

    @pl.when(i == 0)
    def _():
        _gather_rows(y_hbm, pos_ref, 0, ybuf.at[0], sem.at[0], tm)

    @pl.when(i + 1 < n)
    def _():
        _gather_rows(y_hbm, pos_ref, (i + 1) * tm, ybuf.at[1 - slot], sem.at[1 - slot], tm)

    _wait_rows(y_hbm, ybuf.at[slot], sem.at[slot], tm)
    h2 = _layer_norm(DN_ALPHA * h_ref[...] + ybuf[slot], g_ref[...], b_ref[...])
    h2_ref[...] = h2
    hb_ref[...] = (h2 * m_ref[...]).astype(BF16)


def _combine(h1, rowmask, y_slots, pos, ln_g, ln_b):
    t, d = h1.shape
    tm = _pick_tile(t, 256, 16)
    row = pl.BlockSpec((tm, d), lambda i, pos: (i, 0))
    vec = pl.BlockSpec((1, d), lambda i, pos: (0, 0))
    grid_spec = pltpu.PrefetchScalarGridSpec(
        num_scalar_prefetch=1,
        grid=(t // tm,),
        in_specs=[row, pl.BlockSpec((tm, 1), lambda i, pos: (i, 0)), pl.BlockSpec(memory_space=pl.ANY), vec, vec],
        out_specs=[row, row],
        scratch_shapes=[pltpu.VMEM((2, tm, d), F32), pltpu.SemaphoreType.DMA((2,))],
    )
    return pl.pallas_call(
        functools.partial(_combine_kernel, tm=tm),
        grid_spec=grid_spec,
        out_shape=[jax.ShapeDtypeStruct((t, d), F32), jax.ShapeDtypeStruct((t, d), BF16)],
        compiler_params=_params("arbitrary"),
        name="combine",
    )(pos, h1, rowmask, y_slots, ln_g, ln_b)


_PAIR_LO = np.array([p[0] for p in PAIRS], np.int32)
_PAIR_HI = np.array([p[1] for p in PAIRS], np.int32)


def _dispatch_plan(route, t):
    route = jnp.transpose(route, (1, 0, 2)).reshape(SUBLANES, t)
    cls = route[0].astype(jnp.int32)
    onehot = (cls[:, None] == jnp.arange(N_CLASSES, dtype=jnp.int32)[None, :]).astype(jnp.int32)
    csum = jnp.cumsum(onehot, axis=0)
    counts = csum[-1]
    rank = jnp.sum(onehot * csum, axis=1) - 1
    tiles_per = (counts + EXPERT_TM - 1) // EXPERT_TM
    tile_end = jnp.cumsum(tiles_per)
    cls_start = (tile_end - tiles_per) * EXPERT_TM
    pos = cls_start[cls] + rank
    n_tiles = -(-t // EXPERT_TM) + N_CLASSES
    n_used = tile_end[-1]
    tile_ids = jnp.minimum(jnp.arange(n_tiles, dtype=jnp.int32), n_used - 1)
    tile_cls = jnp.sum((tile_ids[:, None] >= tile_end[None, :]).astype(jnp.int32), axis=1)
    grp = tile_cls // len(PAIRS)
    pair = tile_cls % len(PAIRS)
    tile_ea = grp * EXPERTS_PER_GROUP + jnp.asarray(_PAIR_LO)[pair]
    tile_eb = grp * EXPERTS_PER_GROUP + jnp.asarray(_PAIR_HI)[pair]
    wcols = jnp.pad(route[1:3].T, ((0, 0), (0, LANES - 2)))
    zero_tile = jnp.maximum(tile_end - 1, 0).astype(jnp.int32)
    zero_on = (tiles_per > 0).astype(jnp.int32)
    return (pos.astype(jnp.int32), wcols, tile_ea, tile_eb, n_used.reshape(1).astype(jnp.int32),
            zero_tile, zero_on, n_tiles * EXPERT_TM)


def kernel(x, meta_tokens, ln_in_g, ln_in_b, w_router, w_in, conv_w, conv_b, dt_bias, a_log, d_skip,
           ssd_norm_w, w_ssd_out, w_pool, pool_scale, b_gate, w_out, ln1_g, ln1_b, w_exp_gate, w_exp_up,
           w_exp_down, ln2_g, ln2_b):
    batch, seq, d = x.shape
    frame = FRAME_HEAD + seq
    assert d == D_MODEL and frame % CHUNK == 0
    t = batch * frame
    depth = w_in.shape[0]

    head = jnp.concatenate([jnp.zeros((PAD_FRONT, d), x.dtype), meta_tokens.astype(x.dtype)], axis=0)
    tokens = jnp.concatenate([jnp.broadcast_to(head[None], (batch, FRAME_HEAD, d)), x], axis=1).reshape(t, d)
    rowmask = jnp.tile((jnp.arange(frame) >= PAD_FRONT).astype(F32), batch).reshape(t, 1)
    h, hb = _ln_rows(tokens, rowmask, ln_in_g, ln_in_b)

    wr_t = w_router.T
    wr_hi = wr_t.astype(BF16)
    wr_lo = (wr_t - wr_hi.astype(F32)).astype(BF16)

    def pad_lanes(v):
        return jnp.pad(v, (0, LANES - v.shape[0])).reshape(1, LANES)

    for l in range(depth):
        w = w_in[l]
        w_main = jnp.concatenate(
            [w[:, COL_XBC:COL_DT], w[:, COL_Z:COL_XBC], w[:, COL_GATE:], w[:, COL_POOL:COL_GATE]], axis=1).astype(BF16)
        w_dt = jnp.pad(w[:, COL_DT:COL_POOL], ((0, 0), (0, LANES - SSD_HEADS))).astype(BF16)
        proj, dt_raw = _in_proj(hb, w_main, w_dt)

        yn = _conv_ssd(
            proj, dt_raw, conv_w[l], conv_b[l].reshape(1, SSD_XBC), pad_lanes(dt_bias[l]),
            pad_lanes(-jnp.exp(a_log[l])), jnp.repeat(d_skip[l], SSD_HEAD_DIM).reshape(1, SSD_INNER),
            ssd_norm_w[l].reshape(1, SSD_INNER), batch, frame)

        h1, route = _mix(
            yn, proj, h, w_ssd_out[l].astype(BF16), w_pool[l].astype(BF16), w_out[l].astype(BF16),
            b_gate[l].reshape(1, N_BRANCH * d), pool_scale[l].reshape(1, POOL_WIDTH),
            ln1_g[l].reshape(1, d), ln1_b[l].reshape(1, d), wr_hi, wr_lo, batch, frame)

        pos, wcols, tile_ea, tile_eb, n_used, zero_tile, zero_on, n_slots = _dispatch_plan(route, t)
        x_slots = _dispatch(h1, wcols, pos, zero_tile, zero_on, n_used, n_slots)
        y_slots = _experts(x_slots, tile_ea, tile_eb, n_used,
                           w_exp_gate[l].astype(BF16), w_exp_up[l].astype(BF16), w_exp_down[l].astype(BF16))
        h, hb = _combine(h1, rowmask, y_slots, pos, ln2_g[l].reshape(1, d), ln2_b[l].reshape(1, d))

    return h.reshape(batch, frame, d)[:, FRAME_HEAD:]
```

```python
import functools

import jax
import jax.numpy as jnp
import numpy as np
from jax import lax
from jax.experimental import pallas as pl
from jax.experimental.pallas import tpu as pltpu

F32 = jnp.float32
BF16 = jnp.bfloat16

D_MODEL = 1024
CHUNK = 64
N_META = 16
PAD_FRONT = CHUNK - N_META
FRAME_HEAD = PAD_FRONT + N_META

SSD_INNER = 2048
SSD_HEAD_DIM = 64
SSD_HEADS = 32
SSD_GROUPS = 8
SSD_HPG = 4
SSD_STATE = 128
SSD_CONV = 4
SSD_GN = SSD_GROUPS * SSD_STATE
SSD_XBC = SSD_INNER + 2 * SSD_GN
SSD_GROUP_DIM = SSD_HPG * SSD_HEAD_DIM

POOL_WIDTH = 1024
POOL_WINDOWS = (2, 4, 8, 16)
POOL_GROUP_DIM = 256
POOL_HALO = 16

N_BRANCH = 2
COL_Z = 0
COL_XBC = COL_Z + SSD_INNER
COL_DT = COL_XBC + SSD_XBC
COL_POOL = COL_DT + SSD_HEADS
COL_GATE = COL_POOL + POOL_WIDTH
IN_COLS = COL_GATE + N_BRANCH * D_MODEL

N_EXPERTS = 16
N_EXPERT_GROUPS = 4
EXPERTS_PER_GROUP = 4
D_EXPERT = 512
PAIRS = ((0, 1), (0, 2), (0, 3), (1, 2), (1, 3), (2, 3))
N_CLASSES = N_EXPERT_GROUPS * len(PAIRS)

DEPTH = 2
DN_ALPHA = (2.0 * DEPTH) ** 0.25
LN_EPS = 1e-5
RMS_EPS = 1e-5

LANES = 128
SUBLANES = 8
VMEM_LIMIT = 56 * 1024 * 1024

PROJ_TN = 1024
PROJ_COLS = SSD_XBC + SSD_INNER + N_BRANCH * D_MODEL + POOL_WIDTH
PROJ_BLK_XBC = 0
PROJ_BLK_Z = SSD_XBC // SSD_INNER
PROJ_BLK_GATE = PROJ_BLK_Z + 1
PROJ_BLK_POOL = (SSD_XBC + SSD_INNER + N_BRANCH * D_MODEL) // POOL_WIDTH

EXPERT_TM = 256


def _pick_tile(n, target, mult):
    best = None
    for t in range(mult, min(n, target) + 1, mult):
        if n % t == 0:
            best = t
    assert best is not None, (n, target, mult)
    return best


def _params(*sem):
    return pltpu.CompilerParams(dimension_semantics=sem, vmem_limit_bytes=VMEM_LIMIT)


def _sigmoid(x):
    return 0.5 + 0.5 * jnp.tanh(0.5 * x)


def _silu(x):
    hx = 0.5 * x
    return hx + hx * jnp.tanh(hx)


def _layer_norm(x, g, b):
    mu = jnp.mean(x, axis=-1, keepdims=True)
    xc = x - mu
    var = jnp.mean(xc * xc, axis=-1, keepdims=True)
    return xc * lax.rsqrt(var + LN_EPS) * g + b


def _ln_kernel(x_ref, m_ref, g_ref, b_ref, h_ref, hb_ref):
    y = _layer_norm(x_ref[...], g_ref[...], b_ref[...])
    h_ref[...] = y
    hb_ref[...] = (y * m_ref[...]).astype(BF16)


def _ln_rows(x, rowmask, g, b):
    t, d = x.shape
    tm = _pick_tile(t, 1024, 16)
    row = pl.BlockSpec((tm, d), lambda i: (i, 0))
    vec = pl.BlockSpec((1, d), lambda i: (0, 0))
    return pl.pallas_call(
        _ln_kernel,
        grid=(t // tm,),
        in_specs=[row, pl.BlockSpec((tm, 1), lambda i: (i, 0)), vec, vec],
        out_specs=[row, row],
        out_shape=[jax.ShapeDtypeStruct((t, d), F32), jax.ShapeDtypeStruct((t, d), BF16)],
        compiler_params=_params("parallel"),
        name="ln_in",
    )(x, rowmask, g.reshape(1, d), b.reshape(1, d))


CONV_TILES = SSD_XBC // PROJ_TN


def _in_proj_kernel(x_ref, w_ref, wdt_ref, cw_ref, cb_ref, o_ref, dt_ref, tail_ref, *, slab):
    i = pl.program_id(0)
    j = pl.program_id(1)
    tm = x_ref.shape[0]

    @pl.when(j == 0)
    def _():
        dt_ref[...] = jnp.dot(x_ref[...], wdt_ref[...], preferred_element_type=F32)

    @pl.when(j >= CONV_TILES)
    def _():
        o_ref[...] = jnp.dot(x_ref[...], w_ref[...], preferred_element_type=F32).astype(BF16)

    @pl.when(j < CONV_TILES)
    def _():
        jt = jnp.minimum(j, CONV_TILES - 1)

        @pl.when(i == 0)
        def _():
            tail_ref[jt] = jnp.zeros((SUBLANES, PROJ_TN), F32)

        prev8 = tail_ref[jt]
        w = w_ref[...]
        sub = lax.broadcasted_iota(jnp.int32, (1, SUBLANES, PROJ_TN), 1)
        cw = [cw_ref[k:k + 1, :].reshape(1, 1, PROJ_TN) for k in range(SSD_CONV)]
        cb = cb_ref[...].reshape(1, 1, PROJ_TN)

        def shift_rows(v3, head8, s):
            r = pltpu.roll(v3, s, axis=1)
            above = jnp.concatenate([pltpu.roll(head8, s, axis=0)[None], r[:-1]], axis=0)
            return jnp.where(sub < s, above, r)

        for s in range(tm // slab):
            rows = slice(s * slab, (s + 1) * slab)
            acc = jnp.dot(x_ref[rows, :], w, preferred_element_type=F32)
            x0 = acc.reshape(slab // SUBLANES, SUBLANES, PROJ_TN)
            x2 = shift_rows(x0, prev8, 2)
            odd = cw[2] * x0 + cw[0] * x2
            odd_head = cw[2][0] * prev8 + cw[0][0] * pltpu.roll(prev8, 2, axis=0)
            conv = cb + cw[3] * x0 + cw[1] * x2 + shift_rows(odd, odd_head, 1)
            o_ref[rows, :] = _silu(conv).reshape(slab, PROJ_TN).astype(BF16)
            prev8 = acc[slab - SUBLANES:]
        tail_ref[jt] = prev8


def _in_proj(hb, w_main, w_dt, conv_w, conv_b):
    t, d = hb.shape
    tm = _pick_tile(t, 1664, 16)
    slab = _pick_tile(tm, 256, 16)

    def conv_col(i, j):
        return (0, jnp.minimum(j, CONV_TILES - 1))

    return pl.pallas_call(
        functools.partial(_in_proj_kernel, slab=slab),
        grid=(t // tm, PROJ_COLS // PROJ_TN),
        in_specs=[
            pl.BlockSpec((tm, d), lambda i, j: (i, 0)),
            pl.BlockSpec((d, PROJ_TN), lambda i, j: (0, j)),
            pl.BlockSpec((d, LANES), lambda i, j: (0, 0)),
            pl.BlockSpec((SSD_CONV, PROJ_TN), conv_col),
            pl.BlockSpec((1, PROJ_TN), conv_col),
        ],
        out_specs=[
            pl.BlockSpec((tm, PROJ_TN), lambda i, j: (i, j)),
            pl.BlockSpec((tm, LANES), lambda i, j: (i, 0)),
        ],
        out_shape=[jax.ShapeDtypeStruct((t, PROJ_COLS), BF16), jax.ShapeDtypeStruct((t, LANES), F32)],
        scratch_shapes=[pltpu.VMEM((CONV_TILES, SUBLANES, PROJ_TN), F32)],
        compiler_params=_params("arbitrary", "arbitrary"),
        name="in_proj",
    )(hb, w_main, w_dt, conv_w, conv_b)


def _pair_select(lane_lo, col, h0):
    return jnp.where(lane_lo, col[:, h0:h0 + 1], col[:, h0 + 1:h0 + 2])


def _ssd_kernel(xbc_ref, z_ref, dt_ref, dtb_ref, a_ref, dsk_ref, nw_ref, o_ref, prev_ref, y_ref, *, tb):
    j = pl.program_id(1)

    @pl.when(j == 0)
    def _():
        prev_ref[...] = jnp.zeros_like(prev_ref)

    row_i = lax.broadcasted_iota(jnp.int32, (CHUNK, LANES), 0)
    row_c = lax.broadcasted_iota(jnp.int32, (CHUNK, 1), 0)
    lane_i = lax.broadcasted_iota(jnp.int32, (CHUNK, LANES), 1)
    lane_lo = lane_i < SSD_HEAD_DIM
    causal2 = row_i >= jnp.where(lane_lo, lane_i, lane_i - SSD_HEAD_DIM)
    lane_lo_row = lax.broadcasted_iota(jnp.int32, (1, LANES), 1) < SSD_HEAD_DIM

    def chunk(c, carry):
        r0 = pl.multiple_of(c * CHUNK, CHUNK)
        rows = pl.ds(r0, CHUNK)
        valid = (j * tb + r0 + row_c) >= PAD_FRONT

        dtr = dt_ref[rows, :] + dtb_ref[...]
        dt = jnp.maximum(dtr, 0.0) + jnp.log1p(jnp.exp(-jnp.abs(dtr)))
        dt = jnp.where(valid, dt, 0.0)
        a_cs = dt * a_ref[...]
        for s in (1, 2, 4, 8, 16, 32):
            a_cs = a_cs + jnp.where(row_i >= s, pltpu.roll(a_cs, s, axis=0), 0.0)
        a_last = a_cs[CHUNK - 1:CHUNK, :]
        dtw = dt * jnp.exp(a_last - a_cs)
        cdec = jnp.exp(a_last)
        a_t = jnp.concatenate([a_cs, a_cs], axis=0).T
        dt_t = jnp.concatenate([dt, dt], axis=0).T

        def bmat(g):
            return xbc_ref[rows, SSD_INNER + g * SSD_STATE:SSD_INNER + (g + 1) * SSD_STATE]

        def cmat(g):
            o0 = SSD_INNER + SSD_GN + g * SSD_STATE
            return xbc_ref[rows, o0:o0 + SSD_STATE]

        cb2s, yoffs = [], []
        for g in range(SSD_GROUPS):
            bm, cm = bmat(g), cmat(g)
            cb2s.append(lax.dot_general(cm, jnp.concatenate([bm, bm], axis=0),
                                        (((1,), (1,)), ((), ())), preferred_element_type=F32))
            yoffs.append(jnp.dot(cm, prev_ref[g].astype(BF16), preferred_element_type=F32))

        for g in range(SSD_GROUPS):
            for q in range(SSD_HPG // 2):
                h0 = g * SSD_HPG + 2 * q
                ps = slice(g * SSD_GROUP_DIM + q * LANES, g * SSD_GROUP_DIM + (q + 1) * LANES)
                xh_b = xbc_ref[rows, ps]
                a_col = _pair_select(lane_lo, a_cs, h0)
                a_row = jnp.where(lane_lo_row, a_t[h0:h0 + 1, :], a_t[h0 + 1:h0 + 2, :])
                dt_row = jnp.where(lane_lo_row, dt_t[h0:h0 + 1, :], dt_t[h0 + 1:h0 + 2, :])
                decay = jnp.exp(jnp.where(causal2, a_col - a_row, -jnp.inf))
                m = (cb2s[g] * decay * dt_row).astype(BF16)
                zero = jnp.zeros_like(xh_b)
                rhs = jnp.concatenate([jnp.where(lane_lo, xh_b, zero), jnp.where(lane_lo, zero, xh_b)], axis=0)
                y_diag = jnp.dot(m, rhs, preferred_element_type=F32)
                y_off = yoffs[g][:, q * LANES:(q + 1) * LANES] * jnp.exp(a_col)
                y_ref[:, ps] = y_diag + y_off + dsk_ref[:, ps] * xh_b.astype(F32)

        for g in range(SSD_GROUPS):
            xs_parts, cdec_parts = [], []
            for q in range(SSD_HPG // 2):
                h0 = g * SSD_HPG + 2 * q
                ps = slice(g * SSD_GROUP_DIM + q * LANES, g * SSD_GROUP_DIM + (q + 1) * LANES)
                xs_parts.append((xbc_ref[rows, ps].astype(F32) * _pair_select(lane_lo, dtw, h0)).astype(BF16))
                cdec_parts.append(jnp.where(lane_lo_row, cdec[:, h0:h0 + 1], cdec[:, h0 + 1:h0 + 2]))
            xs_dec = jnp.concatenate(xs_parts, axis=1)
            st = lax.dot_general(bmat(g), xs_dec, (((0,), (0,)), ((), ())), preferred_element_type=F32)
            prev_ref[g] = prev_ref[g] * jnp.concatenate(cdec_parts, axis=1) + st

        for g in range(SSD_GROUPS):
            gs = slice(g * SSD_GROUP_DIM, (g + 1) * SSD_GROUP_DIM)
            yg = y_ref[:, gs] * _silu(z_ref[rows, gs].astype(F32))
            ms = jnp.mean(yg * yg, axis=-1, keepdims=True)
            o_ref[rows, gs] = (yg * lax.rsqrt(ms + RMS_EPS) * nw_ref[:, gs]).astype(BF16)
        return carry

    lax.fori_loop(0, tb // CHUNK, chunk, 0)


def _ssd(proj, dt_raw, dt_bias, a_neg, d_skip_ch, norm_w, batch, frame):
    t = proj.shape[0]
    tb = _pick_tile(frame, 832, CHUNK)
    nblk = frame // tb

    def vec(n):
        return pl.BlockSpec((1, n), lambda b, j: (0, 0))

    return pl.pallas_call(
        functools.partial(_ssd_kernel, tb=tb),
        grid=(batch, nblk),
        in_specs=[
            pl.BlockSpec((tb, SSD_XBC), lambda b, j: (b * nblk + j, PROJ_BLK_XBC)),
            pl.BlockSpec((tb, SSD_INNER), lambda b, j: (b * nblk + j, PROJ_BLK_Z)),
            pl.BlockSpec((tb, LANES), lambda b, j: (b * nblk + j, 0)),
            vec(LANES), vec(LANES), vec(SSD_INNER), vec(SSD_INNER),
        ],
        out_specs=pl.BlockSpec((tb, SSD_INNER), lambda b, j: (b * nblk + j, 0)),
        out_shape=jax.ShapeDtypeStruct((t, SSD_INNER), BF16),
        scratch_shapes=[
            pltpu.VMEM((SSD_GROUPS, SSD_STATE, SSD_GROUP_DIM), F32),
            pltpu.VMEM((CHUNK, SSD_INNER), F32),
        ],
        compiler_params=_params("parallel", "arbitrary"),
        name="ssd",
    )(proj, proj, dt_raw, dt_bias, a_neg, d_skip_ch, norm_w)


def _route(logits):
    mx = jnp.max(logits, axis=0, keepdims=True)
    ex = jnp.exp(logits - mx)
    sc = ex / jnp.sum(ex, axis=0, keepdims=True)
    rows = [sc[e:e + 1, :] for e in range(N_EXPERTS)]
    best = None
    for g in range(N_EXPERT_GROUPS):
        a, b, c, d = rows[4 * g:4 * g + 4]
        hi1, lo1 = jnp.maximum(a, b), jnp.minimum(a, b)
        hi2, lo2 = jnp.maximum(c, d), jnp.minimum(c, d)
        gsc = jnp.maximum(hi1, hi2) + jnp.maximum(jnp.minimum(hi1, hi2), jnp.maximum(lo1, lo2))
        if best is None:
            best, sel = gsc, jnp.zeros_like(gsc)
        else:
            better = gsc > best
            sel = jnp.where(better, float(g), sel)
            best = jnp.maximum(best, gsc)
    v = []
    for i in range(EXPERTS_PER_GROUP):
        vi = rows[i]
        for g in range(1, N_EXPERT_GROUPS):
            vi = jnp.where(sel == float(g), rows[4 * g + i], vi)
        v.append(vi)

    def first_max(u):
        m = jnp.maximum(jnp.maximum(u[0], u[1]), jnp.maximum(u[2], u[3]))
        idx = jnp.where(u[0] == m, 0.0, jnp.where(u[1] == m, 1.0, jnp.where(u[2] == m, 2.0, 3.0)))
        return m, idx

    m1, i1 = first_max(v)
    u = [jnp.where(i1 == float(i), -1.0, v[i]) for i in range(EXPERTS_PER_GROUP)]
    m2, i2 = first_max(u)
    den = m1 + m2
    w1, w2 = m1 / den, m2 / den
    lo = jnp.minimum(i1, i2)
    hi = jnp.maximum(i1, i2)
    w_lo = jnp.where(i1 < i2, w1, w2)
    w_hi = jnp.where(i1 < i2, w2, w1)
    pair = jnp.where(lo == 0.0, hi - 1.0, jnp.where(lo == 1.0, hi + 1.0, 5.0))
    cls = sel * float(len(PAIRS)) + pair
    zero = jnp.zeros_like(cls)
    return jnp.concatenate([cls, w_lo, w_hi, zero, zero, zero, zero, zero], axis=0)


def _mix_kernel(yn_ref, gate_ref, pool_ref, h_ref, wso_ref, wp_ref, wo_ref, bg_ref, ps_ref, g_ref, b_ref,
                wr_hi_ref, wr_lo_ref, h1_ref, route_ref, halo_ref, *, tb):
    j = pl.program_id(1)

    @pl.when(j == 0)
    def _():
        halo_ref[0:POOL_HALO, :] = jnp.zeros((POOL_HALO, POOL_WIDTH), F32)

    halo_ref[POOL_HALO:POOL_HALO + tb, :] = pool_ref[...].astype(F32)
    subs = [(0, tb)]
    nt = (((1,), (1,)), ((), ()))

    for r, n in subs:
        gates = _sigmoid(gate_ref[r:r + n, :].astype(F32) + bg_ref[...])

        tpos = (j * tb + r - PAD_FRONT + 1 + lax.broadcasted_iota(jnp.int32, (n, 1), 0)).astype(F32)
        pools = []
        for gi, win in enumerate(POOL_WINDOWS):
            cs = slice(gi * POOL_GROUP_DIM, (gi + 1) * POOL_GROUP_DIM)
            s = halo_ref[r:r + n + POOL_HALO, cs]
            step = 1
            while step < win:
                s = s[step:] + s[:-step]
                step *= 2
            wsum = s[s.shape[0] - n:]
            cnt = jnp.clip(tpos, 1.0, float(win))
            pooled = wsum / cnt - halo_ref[POOL_HALO + r:POOL_HALO + r + n, cs]
            pools.append(jnp.dot(pooled.astype(BF16), wp_ref[gi], preferred_element_type=F32))
        y_pool = jnp.concatenate(pools, axis=1) * ps_ref[...]

        y_ssd = jnp.dot(yn_ref[r:r + n, :], wso_ref[...], preferred_element_type=F32)
        mixed =(gates[:, :D_MODEL] * y_ssd + gates[:, D_MODEL:] * y_pool).astype(BF16)
        out = jnp.dot(mixed, wo_ref[...], preferred_element_type=F32)
        h1 = _layer_norm(DN_ALPHA * h_ref[r:r + n, :] + out, g_ref[...], b_ref[...])
        h1_ref[r:r + n, :] = h1

        h_hi = h1.astype(BF16)
        h_lo = (h1 - h_hi.astype(F32)).astype(BF16)
        logits = (lax.dot_general(wr_hi_ref[...], h_hi, nt, preferred_element_type=F32)
                  + lax.dot_general(wr_lo_ref[...], h_hi, nt, preferred_element_type=F32)
                  + lax.dot_general(wr_hi_ref[...], h_lo, nt, preferred_element_type=F32))
        route_ref[0, :, r:r + n] = _route(logits)

    halo_ref[0:POOL_HALO, :] = halo_ref[tb:tb + POOL_HALO, :]


def _mix(yn, proj, h, w_ssd_out, w_pool, w_out, b_gate, pool_scale, ln_g, ln_b, wr_hi, wr_lo, batch, frame):
    t = yn.shape[0]
    tb = _pick_tile(frame, 320, CHUNK)
    nblk = frame // tb
    d = D_MODEL

    def full(shape):
        return pl.BlockSpec(shape, lambda b, j: (0,) * len(shape))

    return pl.pallas_call(
        functools.partial(_mix_kernel, tb=tb),
        grid=(batch, nblk),
        in_specs=[
            pl.BlockSpec((tb, SSD_INNER), lambda b, j: (b * nblk + j, 0)),
            pl.BlockSpec((tb, N_BRANCH * d), lambda b, j: (b * nblk + j, PROJ_BLK_GATE)),
            pl.BlockSpec((tb, POOL_WIDTH), lambda b, j: (b * nblk + j, PROJ_BLK_POOL)),
            pl.BlockSpec((tb, d), lambda b, j: (b * nblk + j, 0)),
            full((SSD_INNER, d)), full((len(POOL_WINDOWS), POOL_GROUP_DIM, POOL_GROUP_DIM)), full((d, d)),
            full((1, N_BRANCH * d)), full((1, POOL_WIDTH)), full((1, d)), full((1, d)),
            full((N_EXPERTS, d)), full((N_EXPERTS, d)),
        ],
        out_specs=[
            pl.BlockSpec((tb, d), lambda b, j: (b * nblk + j, 0)),
            pl.BlockSpec((1, SUBLANES, tb), lambda b, j: (b * nblk + j, 0, 0)),
        ],
        out_shape=[jax.ShapeDtypeStruct((t, d), F32),
                   jax.ShapeDtypeStruct((batch * nblk, SUBLANES, tb), F32)],
        scratch_shapes=[pltpu.VMEM((POOL_HALO + tb, POOL_WIDTH), F32)],
        compiler_params=_params("parallel", "arbitrary"),
        name="mix",
    )(yn, proj, proj, h, w_ssd_out, w_pool, w_out, b_gate, pool_scale, ln_g, ln_b, wr_hi, wr_lo)


def _row_copies(hbm, idx_ref, base, buf, sem, n, *, to_hbm, wait):
    def body(g, carry):
        for u in range(SUBLANES):
            row = 0 if wait else idx_ref[base + g * SUBLANES + u]
            vm = buf.at[g, pl.ds(u, 1)]
            hb = hbm.at[pl.ds(row, 1)]
            cp = pltpu.make_async_copy(vm, hb, sem) if to_hbm else pltpu.make_async_copy(hb, vm, sem)
            if wait:
                cp.wait()
            else:
                cp.start()
        return carry
    lax.fori_loop(0, n // SUBLANES, body, 0)


def _gather_rows(src_hbm, idx_ref, base, buf, sem, n):
    _row_copies(src_hbm, idx_ref, base, buf, sem, n, to_hbm=False, wait=False)


def _wait_rows(src_hbm, buf, sem, n):
    _row_copies(src_hbm, None, 0, buf, sem, n, to_hbm=False, wait=True)


def _scatter_rows(buf, idx_ref, base, dst_hbm, sem, n):
    _row_copies(dst_hbm, idx_ref, base, buf, sem, n, to_hbm=True, wait=False)


def _wait_scattered(buf, dst_hbm, sem, n):
    _row_copies(dst_hbm, None, 0, buf, sem, n, to_hbm=True, wait=True)


def _dispatch_kernel(pos_ref, zt_ref, zon_ref, nu_ref, h_ref, w_ref, x_hbm, stage, zbuf, sem, zsem, *,
                     tm, n, n_tiles):
    i = pl.program_id(0)
    slot = i % 2

    @pl.when(i == 0)
    def _():
        zbuf[...] = jnp.zeros_like(zbuf)

        def zero_tile(tile):
            row0 = pl.multiple_of(tile * EXPERT_TM, EXPERT_TM)
            cp = pltpu.make_async_copy(zbuf, x_hbm.at[pl.ds(row0, EXPERT_TM)], zsem)
            cp.start()
            cp.wait()

        for c in range(N_CLASSES):
            pl.when(zon_ref[c] > 0)(functools.partial(zero_tile, zt_ref[c]))
            pl.when(nu_ref[0] + c < n_tiles)(functools.partial(zero_tile, nu_ref[0] + c))

    @pl.when(i >= 2)
    def _():
        _wait_scattered(stage.at[slot], x_hbm, sem.at[slot], tm)

    stage[slot, :, :, 0:D_MODEL] = h_ref[...]
    stage[slot, :, :, D_MODEL:D_MODEL + LANES] = w_ref[...]
    _scatter_rows(stage.at[slot], pos_ref, i * tm, x_hbm, sem.at[slot], tm)

    @pl.when(i == n - 1)
    def _():
        if n >= 2:
            _wait_scattered(stage.at[1 - slot], x_hbm, sem.at[1 - slot], tm)
        _wait_scattered(stage.at[slot], x_hbm, sem.at[slot], tm)


def _dispatch(h1, wcols, pos, zero_tile, zero_on, n_used, n_slots):
    t8, _, d = h1.shape
    t = t8 * SUBLANES
    tm = _pick_tile(t, 512, 16)
    g = tm // SUBLANES
    grid_spec = pltpu.PrefetchScalarGridSpec(
        num_scalar_prefetch=4,
        grid=(t // tm,),
        in_specs=[pl.BlockSpec((g, SUBLANES, d), lambda i, *_: (i, 0, 0)),
                  pl.BlockSpec((g, SUBLANES, LANES), lambda i, *_: (i, 0, 0))],
        out_specs=pl.BlockSpec(memory_space=pl.ANY),
        scratch_shapes=[pltpu.VMEM((2, g, SUBLANES, d + LANES), F32), pltpu.VMEM((EXPERT_TM, d + LANES), F32),
                        pltpu.SemaphoreType.DMA((2,)), pltpu.SemaphoreType.DMA(())],
    )
    return pl.pallas_call(
        functools.partial(_dispatch_kernel, tm=tm, n=t // tm, n_tiles=n_slots // EXPERT_TM),
        grid_spec=grid_spec,
        out_shape=jax.ShapeDtypeStruct((n_slots, d + LANES), F32),
        compiler_params=_params("arbitrary"),
        name="dispatch",
    )(pos, zero_tile, zero_on, n_used, h1, wcols)


def _expert_kernel(ea_ref, eb_ref, nu_ref, newa_ref, newb_ref, x_ref, wga_ref, wua_ref, wda_ref,
                   wgb_ref, wub_ref, wdb_ref, o_ref, wg_s, wu_s, wd_s):
    i = pl.program_id(0)

    @pl.when(i < nu_ref[0])
    def _():
        @pl.when(newa_ref[i] > 0)
        def _():
            wg_s[0] = wga_ref[0].astype(BF16)
            wu_s[0] = wua_ref[0].astype(BF16)
            wd_s[0] = wda_ref[0].astype(BF16)

        @pl.when(newb_ref[i] > 0)
        def _():
            wg_s[1] = wgb_ref[0].astype(BF16)
            wu_s[1] = wub_ref[0].astype(BF16)
            wd_s[1] = wdb_ref[0].astype(BF16)

        x = x_ref[:, 0:D_MODEL].astype(BF16)
        wts = x_ref[:, D_MODEL:D_MODEL + LANES]

        def mlp(e):
            hid = _silu(jnp.dot(x, wg_s[e], preferred_element_type=F32)) * jnp.dot(x, wu_s[e], preferred_element_type=F32)
            return jnp.dot(hid.astype(BF16), wd_s[e], preferred_element_type=F32)

        o_ref[...] = wts[:, 0:1] * mlp(0) + wts[:, 1:2] * mlp(1)

    @pl.when(i >= nu_ref[0])
    def _():
        o_ref[...] = jnp.zeros_like(o_ref)


def _experts(x_slots, tile_ea, tile_eb, n_used, w_gate, w_up, w_down):
    n_slots, dw = x_slots.shape
    d = D_MODEL
    n_tiles = n_slots // EXPERT_TM
    first = jnp.ones((1,), jnp.int32)
    new_a = jnp.concatenate([first, (tile_ea[1:] != tile_ea[:-1]).astype(jnp.int32)])
    new_b = jnp.concatenate([first, (tile_eb[1:] != tile_eb[:-1]).astype(jnp.int32)])

    def wspec(shape, which):
        if which == 0:
            return pl.BlockSpec(shape, lambda i, ea, eb, *_: (ea[i], 0, 0))
        return pl.BlockSpec(shape, lambda i, ea, eb, *_: (eb[i], 0, 0))

    up = (1, d, D_EXPERT)
    down = (1, D_EXPERT, d)
    grid_spec = pltpu.PrefetchScalarGridSpec(
        num_scalar_prefetch=5,
        grid=(n_tiles,),
        in_specs=[
            pl.BlockSpec((EXPERT_TM, dw), lambda i, *_: (i, 0)),
            wspec(up, 0), wspec(up, 0), wspec(down, 0),
            wspec(up, 1), wspec(up, 1), wspec(down, 1),
        ],
        out_specs=pl.BlockSpec((EXPERT_TM, d), lambda i, *_: (i, 0)),
        scratch_shapes=[pltpu.VMEM((2, d, D_EXPERT), BF16), pltpu.VMEM((2, d, D_EXPERT), BF16),
                        pltpu.VMEM((2, D_EXPERT, d), BF16)],
    )
    return pl.pallas_call(
        _expert_kernel,
        grid_spec=grid_spec,
        out_shape=jax.ShapeDtypeStruct((n_slots, d), F32),
        compiler_params=_params("arbitrary"),
        name="experts",
    )(tile_ea, tile_eb, n_used, new_a, new_b, x_slots, w_gate, w_up, w_down, w_gate, w_up, w_down)


def _combine_kernel(pos_ref, h_ref, m_ref, y_hbm, g_ref, b_ref, h2_ref, hb_ref, ybuf, sem, *, tm):
    i = pl.program_id(0)
    n = pl.num_programs(0)
    slot = i % 2

    @pl.when(i == 0)
    def _():
        _gather_rows(y_hbm, pos_ref, 0, ybuf.at[0], sem.at[0], tm)

    @pl.when(i + 1 < n)
    def _():
        _gather_rows(y_hbm, pos_ref, (i + 1) * tm, ybuf.at[1 - slot], sem.at[1 - slot], tm)

    _wait_rows(y_hbm, ybuf.at[slot], sem.at[slot], tm)
    y = ybuf[slot].reshape(tm, D_MODEL)
    h2 = _layer_norm(DN_ALPHA * h_ref[...] + y, g_ref[...], b_ref[...])
    h2_ref[...] = h2
    hb_ref[...] = (h2 * m_ref[...]).astype(BF16)


def _combine(h1, rowmask, y_slots, pos, ln_g, ln_b):
    t, d = h1.shape
    tm = _pick_tile(t, 256, 16)
    row = pl.BlockSpec((tm, d), lambda i, pos: (i, 0))
    vec = pl.BlockSpec((1, d), lambda i, pos: (0, 0))
    grid_spec = pltpu.PrefetchScalarGridSpec(
        num_scalar_prefetch=1,
        grid=(t // tm,),
        in_specs=[row, pl.BlockSpec((tm, 1), lambda i, pos: (i, 0)), pl.BlockSpec(memory_space=pl.ANY), vec, vec],
        out_specs=[row, row],
        scratch_shapes=[pltpu.VMEM((2, tm // SUBLANES, SUBLANES, d), F32), pltpu.SemaphoreType.DMA((2,))],
    )
    return pl.pallas_call(
        functools.partial(_combine_kernel, tm=tm),
        grid_spec=grid_spec,
        out_shape=[jax.ShapeDtypeStruct((t, d), F32), jax.ShapeDtypeStruct((t, d), BF16)],
        compiler_params=_params("arbitrary"),
        name="combine",
    )(pos, h1, rowmask, y_slots, ln_g, ln_b)


_PAIR_LO = np.array([p[0] for p in PAIRS], np.int32)
_PAIR_HI = np.array([p[1] for p in PAIRS], np.int32)


def _dispatch_plan(route, t):
    route = jnp.transpose(route, (1, 0, 2)).reshape(SUBLANES, t)
    cls = route[0].astype(jnp.int32)
    onehot = (cls[:, None] == jnp.arange(N_CLASSES, dtype=jnp.int32)[None, :]).astype(jnp.int32)
    csum = jnp.cumsum(onehot, axis=0)
    counts = csum[-1]
    rank = jnp.sum(onehot * csum, axis=1) - 1
    tiles_per = (counts + EXPERT_TM - 1) // EXPERT_TM
    tile_end = jnp.cumsum(tiles_per)
    cls_start = (tile_end - tiles_per) * EXPERT_TM
    pos = cls_start[cls] + rank
    n_tiles = -(-t // EXPERT_TM) + N_CLASSES
    n_used = tile_end[-1]
    tile_ids = jnp.minimum(jnp.arange(n_tiles, dtype=jnp.int32), n_used - 1)
    tile_cls = jnp.sum((tile_ids[:, None] >= tile_end[None, :]).astype(jnp.int32), axis=1)
    grp = tile_cls // len(PAIRS)
    pair = tile_cls % len(PAIRS)
    tile_ea = grp * EXPERTS_PER_GROUP + jnp.asarray(_PAIR_LO)[pair]
    tile_eb = grp * EXPERTS_PER_GROUP + jnp.asarray(_PAIR_HI)[pair]
    wcols = jnp.pad(route[1:3].T, ((0, 0), (0, LANES - 2)))
    zero_tile = jnp.maximum(tile_end - 1, 0).astype(jnp.int32)
    zero_on = (tiles_per > 0).astype(jnp.int32)
    return (pos.astype(jnp.int32), wcols, tile_ea, tile_eb, n_used.reshape(1).astype(jnp.int32),
            zero_tile, zero_on, n_tiles * EXPERT_TM)


def kernel(x, meta_tokens, ln_in_g, ln_in_b, w_router, w_in, conv_w, conv_b, dt_bias, a_log, d_skip,
           ssd_norm_w, w_ssd_out, w_pool, pool_scale, b_gate, w_out, ln1_g, ln1_b, w_exp_gate, w_exp_up,
           w_exp_down, ln2_g, ln2_b):
    batch, seq, d = x.shape
    frame = FRAME_HEAD + seq
    assert d == D_MODEL and frame % CHUNK == 0
    t = batch * frame
    depth = w_in.shape[0]

    head = jnp.concatenate([jnp.zeros((PAD_FRONT, d), x.dtype), meta_tokens.astype(x.dtype)], axis=0)
    tokens = jnp.concatenate([jnp.broadcast_to(head[None], (batch, FRAME_HEAD, d)), x], axis=1).reshape(t, d)
    rowmask = jnp.asarray(np.tile(np.arange(frame) >= PAD_FRONT, batch).astype(np.float32).reshape(t, 1))
    h, hb = _ln_rows(tokens, rowmask, ln_in_g, ln_in_b)

    wr_t = w_router.T
    wr_hi = wr_t.astype(BF16)
    wr_lo = (wr_t - wr_hi.astype(F32)).astype(BF16)

    def pad_lanes(v):
        return jnp.pad(v, (0, LANES - v.shape[0])).reshape(1, LANES)

    for l in range(depth):
        w = w_in[l]
        w_main = jnp.concatenate(
            [w[:, COL_XBC:COL_DT], w[:, COL_Z:COL_XBC], w[:, COL_GATE:], w[:, COL_POOL:COL_GATE]], axis=1).astype(BF16)
        w_dt = jnp.pad(w[:, COL_DT:COL_POOL], ((0, 0), (0, LANES - SSD_HEADS))).astype(BF16)
        proj, dt_raw = _in_proj(hb, w_main, w_dt, conv_w[l], conv_b[l].reshape(1, SSD_XBC))

        yn = _ssd(
            proj, dt_raw, pad_lanes(dt_bias[l]),
            pad_lanes(-jnp.exp(a_log[l])), jnp.repeat(d_skip[l], SSD_HEAD_DIM).reshape(1, SSD_INNER),
            ssd_norm_w[l].reshape(1, SSD_INNER), batch, frame)

        h1, route = _mix(
            yn, proj, h, w_ssd_out[l].astype(BF16), w_pool[l].astype(BF16), w_out[l].astype(BF16),
            b_gate[l].reshape(1, N_BRANCH * d), pool_scale[l].reshape(1, POOL_WIDTH),
            ln1_g[l].reshape(1, d), ln1_b[l].reshape(1, d), wr_hi, wr_lo, batch, frame)

        pos, wcols, tile_ea, tile_eb, n_used, zero_tile, zero_on, n_slots = _dispatch_plan(route, t)
        x_slots = _dispatch(h1.reshape(t // SUBLANES, SUBLANES, d), wcols.reshape(t // SUBLANES, SUBLANES, LANES),
                            pos, zero_tile, zero_on, n_used, n_slots)
        y_slots = _experts(x_slots, tile_ea, tile_eb, n_used, w_exp_gate[l], w_exp_up[l], w_exp_down[l])
        h, hb = _combine(h1, rowmask, y_slots, pos, ln2_g[l].reshape(1, d), ln2_b[l].reshape(1, d))

    return h.reshape(batch, frame, d)[:, FRAME_HEAD:]
```

```python
import functools

import jax
import jax.numpy as jnp
import numpy as np
from jax import lax
from jax.experimental import pallas as pl
from jax.experimental.pallas import tpu as pltpu

F32 = jnp.float32
BF16 = jnp.bfloat16

D_MODEL = 1024
CHUNK = 64
N_META = 16
PAD_FRONT = CHUNK - N_META
FRAME_HEAD = PAD_FRONT + N_META

SSD_INNER = 2048
SSD_HEAD_DIM = 64
SSD_HEADS = 32
SSD_GROUPS = 8
SSD_HPG = 4
SSD_STATE = 128
SSD_CONV = 4
SSD_GN = SSD_GROUPS * SSD_STATE
SSD_XBC = SSD_INNER + 2 * SSD_GN
SSD_GROUP_DIM = SSD_HPG * SSD_HEAD_DIM

POOL_WIDTH = 1024
POOL_WINDOWS = (2, 4, 8, 16)
POOL_GROUP_DIM = 256
POOL_HALO = 16

N_BRANCH = 2
COL_Z = 0
COL_XBC = COL_Z + SSD_INNER
COL_DT = COL_XBC + SSD_XBC
COL_POOL = COL_DT + SSD_HEADS
COL_GATE = COL_POOL + POOL_WIDTH
IN_COLS = COL_GATE + N_BRANCH * D_MODEL

N_EXPERTS = 16
N_EXPERT_GROUPS = 4
EXPERTS_PER_GROUP = 4
D_EXPERT = 512
PAIRS = ((0, 1), (0, 2), (0, 3), (1, 3), (1, 2), (3, 2))
N_CLASSES = N_EXPERT_GROUPS * len(PAIRS)

DEPTH = 2
DN_ALPHA = (2.0 * DEPTH) ** 0.25
LN_EPS = 1e-5
RMS_EPS = 1e-5

LANES = 128
SUBLANES = 8
VMEM_LIMIT = 56 * 1024 * 1024

PROJ_TN = 1024
PLAIN_COLS = SSD_INNER + N_BRANCH * D_MODEL + POOL_WIDTH
PLAIN_BLK_Z = 0
PLAIN_BLK_GATE = 1
PLAIN_BLK_POOL = (SSD_INNER + N_BRANCH * D_MODEL) // POOL_WIDTH

EXPERT_TM = 256


def _pick_tile(n, target, mult):
    best = None
    for t in range(mult, min(n, target) + 1, mult):
        if n % t == 0:
            best = t
    assert best is not None, (n, target, mult)
    return best


def _params(*sem):
    return pltpu.CompilerParams(dimension_semantics=sem, vmem_limit_bytes=VMEM_LIMIT)


def _sigmoid(x):
    return 0.5 + 0.5 * jnp.tanh(0.5 * x)


def _silu(x):
    hx = 0.5 * x
    return hx + hx * jnp.tanh(hx)


def _layer_norm(x, g, b):
    mu = jnp.mean(x, axis=-1, keepdims=True)
    xc = x - mu
    var = jnp.mean(xc * xc, axis=-1, keepdims=True)
    return xc * lax.rsqrt(var + LN_EPS) * g + b


def _frame_block_copy(hbm, buf, sem, step, slot, *, blk, nblk, to_hbm, wait):
    b = step // nblk
    j = step - b * nblk

    def run(vm, hb):
        cp = pltpu.make_async_copy(vm, hb, sem.at[slot]) if to_hbm else pltpu.make_async_copy(hb, vm, sem.at[slot])
        if wait:
            cp.wait()
        else:
            cp.start()

    if blk > FRAME_HEAD:
        @pl.when(j == 0)
        def _():
            n = blk - FRAME_HEAD
            run(buf.at[slot, pl.ds(FRAME_HEAD, n)], hbm.at[b, pl.ds(0, n)])

    @pl.when(j > 0)
    def _():
        start = pl.multiple_of(j * blk - FRAME_HEAD, CHUNK)
        run(buf.at[slot], hbm.at[b, pl.ds(start, blk)])


def _embed_kernel(x_hbm, head_ref, g_ref, b_ref, h_ref, hb_ref, xbuf, sem, *, blk, nblk, n):
    s = pl.program_id(0)
    slot = s % 2
    copy = functools.partial(_frame_block_copy, x_hbm, xbuf, sem, blk=blk, nblk=nblk, to_hbm=False)

    @pl.when(s == 0)
    def _():
        copy(0, 0, wait=False)

    @pl.when(s + 1 < n)
    def _():
        copy(s + 1, 1 - slot, wait=False)

    copy(s, slot, wait=True)
    first = s % nblk == 0

    @pl.when(first)
    def _():
        xbuf[slot, 0:FRAME_HEAD, :] = head_ref[...]

    y = _layer_norm(xbuf[slot], g_ref[...], b_ref[...])
    h_ref[...] = y
    row = lax.broadcasted_iota(jnp.int32, (blk, 1), 0)
    keep = jnp.logical_or(jnp.logical_not(first), row >= PAD_FRONT)
    hb_ref[...] = jnp.where(keep, y, 0.0).astype(BF16)


def _embed_ln(x, head, g, b):
    batch, seq, d = x.shape
    frame = FRAME_HEAD + seq
    blk = _pick_tile(frame, 832, CHUNK)
    nblk = frame // blk
    n = batch * nblk
    row = pl.BlockSpec((blk, d), lambda s: (s, 0))
    vec = pl.BlockSpec((1, d), lambda s: (0, 0))
    return pl.pallas_call(
        functools.partial(_embed_kernel, blk=blk, nblk=nblk, n=n),
        grid=(n,),
        in_specs=[pl.BlockSpec(memory_space=pl.ANY), pl.BlockSpec((FRAME_HEAD, d), lambda s: (0, 0)), vec, vec],
        out_specs=[row, row],
        out_shape=[jax.ShapeDtypeStruct((batch * frame, d), F32), jax.ShapeDtypeStruct((batch * frame, d), BF16)],
        scratch_shapes=[pltpu.VMEM((2, blk, d), F32), pltpu.SemaphoreType.DMA((2,))],
        compiler_params=_params("arbitrary"),
        name="embed_ln",
    )(x, head, g.reshape(1, d), b.reshape(1, d))


CONV_TILES = SSD_XBC // PROJ_TN


def _in_proj_kernel(x_ref, wx_ref, wp_ref, wdt_ref, cw_ref, cb_ref, xo_ref, po_ref, dt_ref, tail_ref, *, slab):
    i = pl.program_id(0)
    j = pl.program_id(1)
    tm = x_ref.shape[0]

    @pl.when(j == 0)
    def _():
        dt_ref[...] = jnp.dot(x_ref[...], wdt_ref[...], preferred_element_type=F32)

    @pl.when(j >= CONV_TILES)
    def _():
        po_ref[...] = jnp.dot(x_ref[...], wp_ref[...], preferred_element_type=F32).astype(BF16)

    @pl.when(j < CONV_TILES)
    def _():
        jt = jnp.minimum(j, CONV_TILES - 1)

        @pl.when(i == 0)
        def _():
            tail_ref[jt] = jnp.zeros((SUBLANES, PROJ_TN), F32)

        prev8 = tail_ref[jt]
        w = wx_ref[...]
        wp = wp_ref[...]
        sub = lax.broadcasted_iota(jnp.int32, (1, SUBLANES, PROJ_TN), 1)
        cw = [cw_ref[k:k + 1, :].reshape(1, 1, PROJ_TN) for k in range(SSD_CONV)]
        cb = cb_ref[...].reshape(1, 1, PROJ_TN)

        def shift_rows(v3, head8, s):
            r = pltpu.roll(v3, s, axis=1)
            above = jnp.concatenate([pltpu.roll(head8, s, axis=0)[None], r[:-1]], axis=0)
            return jnp.where(sub < s, above, r)

        for s in range(tm // slab):
            rows = slice(s * slab, (s + 1) * slab)
            acc = jnp.dot(x_ref[rows, :], w, preferred_element_type=F32)
            x0 = acc.reshape(slab // SUBLANES, SUBLANES, PROJ_TN)
            x2 = shift_rows(x0, prev8, 2)
            odd = cw[2] * x0 + cw[0] * x2
            odd_head = cw[2][0] * prev8 + cw[0][0] * pltpu.roll(prev8, 2, axis=0)
            conv = cb + cw[3] * x0 + cw[1] * x2 + shift_rows(odd, odd_head, 1)
            xo_ref[rows, :] = _silu(conv).reshape(slab, PROJ_TN).astype(BF16)
            prev8 = acc[slab - SUBLANES:]
            po_ref[rows, :] = jnp.dot(x_ref[rows, :], wp, preferred_element_type=F32).astype(BF16)
        tail_ref[jt] = prev8


def _in_proj(hb, w_xbc, w_plain, w_dt, conv_w, conv_b):
    t, d = hb.shape
    tm = _pick_tile(t, 1664, 16)
    slab = _pick_tile(tm, 256, 16)

    def conv_col(i, j):
        return (0, jnp.minimum(j, CONV_TILES - 1))

    return pl.pallas_call(
        functools.partial(_in_proj_kernel, slab=slab),
        grid=(t // tm, PLAIN_COLS // PROJ_TN),
        in_specs=[
            pl.BlockSpec((tm, d), lambda i, j: (i, 0)),
            pl.BlockSpec((d, PROJ_TN), conv_col),
            pl.BlockSpec((d, PROJ_TN), lambda i, j: (0, j)),
            pl.BlockSpec((d, LANES), lambda i, j: (0, 0)),
            pl.BlockSpec((SSD_CONV, PROJ_TN), conv_col),
            pl.BlockSpec((1, PROJ_TN), conv_col),
        ],
        out_specs=[
            pl.BlockSpec((tm, PROJ_TN), lambda i, j: (i, jnp.minimum(j, CONV_TILES - 1))),
            pl.BlockSpec((tm, PROJ_TN), lambda i, j: (i, j)),
            pl.BlockSpec((tm, LANES), lambda i, j: (i, 0)),
        ],
        out_shape=[jax.ShapeDtypeStruct((t, SSD_XBC), BF16), jax.ShapeDtypeStruct((t, PLAIN_COLS), BF16),
                   jax.ShapeDtypeStruct((t, LANES), F32)],
        scratch_shapes=[pltpu.VMEM((CONV_TILES, SUBLANES, PROJ_TN), F32)],
        compiler_params=_params("arbitrary", "arbitrary"),
        name="in_proj",
    )(hb, w_xbc, w_plain, w_dt, conv_w, conv_b)


def _pair_select(lane_lo, col, h0):
    return jnp.where(lane_lo, col[:, h0:h0 + 1], col[:, h0 + 1:h0 + 2])


def _ssd_kernel(xbc_ref, z_ref, dt_ref, dtb_ref, a_ref, dsk_ref, nw_ref, o_ref, prev_ref, y_ref, *, tb):
    j = pl.program_id(1)

    @pl.when(j == 0)
    def _():
        prev_ref[...] = jnp.zeros_like(prev_ref)

    row_i = lax.broadcasted_iota(jnp.int32, (CHUNK, LANES), 0)
    row_c = lax.broadcasted_iota(jnp.int32, (CHUNK, 1), 0)
    lane_i = lax.broadcasted_iota(jnp.int32, (CHUNK, LANES), 1)
    lane_lo = lane_i < SSD_HEAD_DIM
    causal2 = row_i >= jnp.where(lane_lo, lane_i, lane_i - SSD_HEAD_DIM)
    lane_lo_row = lax.broadcasted_iota(jnp.int32, (1, LANES), 1) < SSD_HEAD_DIM

    def chunk(c, carry):
        r0 = pl.multiple_of(c * CHUNK, CHUNK)
        rows = pl.ds(r0, CHUNK)
        valid = (j * tb + r0 + row_c) >= PAD_FRONT

        dtr = dt_ref[rows, :] + dtb_ref[...]
        dt = jnp.maximum(dtr, 0.0) + jnp.log1p(jnp.exp(-jnp.abs(dtr)))
        dt = jnp.where(valid, dt, 0.0)
        a_cs = dt * a_ref[...]
        for s in (1, 2, 4, 8, 16, 32):
            a_cs = a_cs + jnp.where(row_i >= s, pltpu.roll(a_cs, s, axis=0), 0.0)
        a_last = a_cs[CHUNK - 1:CHUNK, :]
        dtw = dt * jnp.exp(a_last - a_cs)
        cdec = jnp.exp(a_last)
        a_t = jnp.concatenate([a_cs, a_cs], axis=0).T
        dt_t = jnp.concatenate([dt, dt], axis=0).T

        def bmat(g):
            return xbc_ref[rows, SSD_INNER + g * SSD_STATE:SSD_INNER + (g + 1) * SSD_STATE]

        def cmat(g):
            o0 = SSD_INNER + SSD_GN + g * SSD_STATE
            return xbc_ref[rows, o0:o0 + SSD_STATE]

        cb2s, yoffs = [], []
        for g in range(SSD_GROUPS):
            bm, cm = bmat(g), cmat(g)
            cb2s.append(lax.dot_general(cm, jnp.concatenate([bm, bm], axis=0),
                                        (((1,), (1,)), ((), ())), preferred_element_type=F32))
            yoffs.append(jnp.dot(cm, prev_ref[g].astype(BF16), preferred_element_type=F32))

        for g in range(SSD_GROUPS):
            for q in range(SSD_HPG // 2):
                h0 = g * SSD_HPG + 2 * q
                ps = slice(g * SSD_GROUP_DIM + q * LANES, g * SSD_GROUP_DIM + (q + 1) * LANES)
                xh_b = xbc_ref[rows, ps]
                a_col = _pair_select(lane_lo, a_cs, h0)
                a_row = jnp.where(lane_lo_row, a_t[h0:h0 + 1, :], a_t[h0 + 1:h0 + 2, :])
                dt_row = jnp.where(lane_lo_row, dt_t[h0:h0 + 1, :], dt_t[h0 + 1:h0 + 2, :])
                decay = jnp.exp(jnp.where(causal2, a_col - a_row, -jnp.inf))
                m = (cb2s[g] * decay * dt_row).astype(BF16)
                zero = jnp.zeros_like(xh_b)
                rhs = jnp.concatenate([jnp.where(lane_lo, xh_b, zero), jnp.where(lane_lo, zero, xh_b)], axis=0)
                y_diag = jnp.dot(m, rhs, preferred_element_type=F32)
                y_off = yoffs[g][:, q * LANES:(q + 1) * LANES] * jnp.exp(a_col)
                y_ref[:, ps] = y_diag + y_off + dsk_ref[:, ps] * xh_b.astype(F32)

        for g in range(SSD_GROUPS):
            xs_parts, cdec_parts = [], []
            for q in range(SSD_HPG // 2):
                h0 = g * SSD_HPG + 2 * q
                ps = slice(g * SSD_GROUP_DIM + q * LANES, g * SSD_GROUP_DIM + (q + 1) * LANES)
                xs_parts.append((xbc_ref[rows, ps].astype(F32) * _pair_select(lane_lo, dtw, h0)).astype(BF16))
                cdec_parts.append(jnp.where(lane_lo_row, cdec[:, h0:h0 + 1], cdec[:, h0 + 1:h0 + 2]))
            xs_dec = jnp.concatenate(xs_parts, axis=1)
            st = lax.dot_general(bmat(g), xs_dec, (((0,), (0,)), ((), ())), preferred_element_type=F32)
            prev_ref[g] = prev_ref[g] * jnp.concatenate(cdec_parts, axis=1) + st

        for g in range(SSD_GROUPS):
            gs = slice(g * SSD_GROUP_DIM, (g + 1) * SSD_GROUP_DIM)
            yg = y_ref[:, gs] * _silu(z_ref[rows, gs].astype(F32))
            ms = jnp.mean(yg * yg, axis=-1, keepdims=True)
            o_ref[rows, gs] = (yg * lax.rsqrt(ms + RMS_EPS) * nw_ref[:, gs]).astype(BF16)
        return carry

    lax.fori_loop(0, tb // CHUNK, chunk, 0)


def _ssd(xact, plain, dt_raw, dt_bias, a_neg, d_skip_ch, norm_w, batch, frame):
    t = xact.shape[0]
    tb = _pick_tile(frame, 832, CHUNK)
    nblk = frame // tb

    def vec(n):
        return pl.BlockSpec((1, n), lambda b, j: (0, 0))

    return pl.pallas_call(
        functools.partial(_ssd_kernel, tb=tb),
        grid=(batch, nblk),
        in_specs=[
            pl.BlockSpec((tb, SSD_XBC), lambda b, j: (b * nblk + j, 0)),
            pl.BlockSpec((tb, SSD_INNER), lambda b, j: (b * nblk + j, PLAIN_BLK_Z)),
            pl.BlockSpec((tb, LANES), lambda b, j: (b * nblk + j, 0)),
            vec(LANES), vec(LANES), vec(SSD_INNER), vec(SSD_INNER),
        ],
        out_specs=pl.BlockSpec((tb, SSD_INNER), lambda b, j: (b * nblk + j, 0)),
        out_shape=jax.ShapeDtypeStruct((t, SSD_INNER), BF16),
        scratch_shapes=[
            pltpu.VMEM((SSD_GROUPS, SSD_STATE, SSD_GROUP_DIM), F32),
            pltpu.VMEM((CHUNK, SSD_INNER), F32),
        ],
        compiler_params=_params("parallel", "arbitrary"),
        name="ssd",
    )(xact, plain, dt_raw, dt_bias, a_neg, d_skip_ch, norm_w)


def _route(logits):
    mx = jnp.max(logits, axis=0, keepdims=True)
    ex = jnp.exp(logits - mx)
    sc = ex / jnp.sum(ex, axis=0, keepdims=True)
    rows = [sc[e:e + 1, :] for e in range(N_EXPERTS)]
    best = None
    for g in range(N_EXPERT_GROUPS):
        a, b, c, d = rows[4 * g:4 * g + 4]
        hi1, lo1 = jnp.maximum(a, b), jnp.minimum(a, b)
        hi2, lo2 = jnp.maximum(c, d), jnp.minimum(c, d)
        gsc = jnp.maximum(hi1, hi2) + jnp.maximum(jnp.minimum(hi1, hi2), jnp.maximum(lo1, lo2))
        if best is None:
            best, sel = gsc, jnp.zeros_like(gsc)
        else:
            better = gsc > best
            sel = jnp.where(better, float(g), sel)
            best = jnp.maximum(best, gsc)
    v = []
    for i in range(EXPERTS_PER_GROUP):
        vi = rows[i]
        for g in range(1, N_EXPERT_GROUPS):
            vi = jnp.where(sel == float(g), rows[4 * g + i], vi)
        v.append(vi)

    def first_max(u):
        m = jnp.maximum(jnp.maximum(u[0], u[1]), jnp.maximum(u[2], u[3]))
        idx = jnp.where(u[0] == m, 0.0, jnp.where(u[1] == m, 1.0, jnp.where(u[2] == m, 2.0, 3.0)))
        return m, idx

    m1, i1 = first_max(v)
    u = [jnp.where(i1 == float(i), -1.0, v[i]) for i in range(EXPERTS_PER_GROUP)]
    m2, i2 = first_max(u)
    den = m1 + m2
    w1, w2 = m1 / den, m2 / den
    lo = jnp.minimum(i1, i2)
    hi = jnp.maximum(i1, i2)
    w_lo = jnp.where(i1 < i2, w1, w2)
    w_hi = jnp.where(i1 < i2, w2, w1)
    pair = jnp.where(lo == 0.0, hi - 1.0, jnp.where(lo == 1.0, jnp.where(hi == 3.0, 3.0, 4.0), 5.0))
    swap = pair == 5.0
    w_a = jnp.where(swap, w_hi, w_lo)
    w_b = jnp.where(swap, w_lo, w_hi)
    cls = sel * float(len(PAIRS)) + pair
    zero = jnp.zeros_like(cls)
    return jnp.concatenate([cls, w_a, w_b, zero, zero, zero, zero, zero], axis=0)


def _mix_kernel(yn_ref, gate_ref, pool_ref, h_ref, wso_ref, wp_ref, wo_ref, bg_ref, ps_ref, g_ref, b_ref,
                wr_hi_ref, wr_lo_ref, h1_ref, route_ref, halo_ref, *, tb):
    j = pl.program_id(1)

    @pl.when(j == 0)
    def _():
        halo_ref[0:POOL_HALO, :] = jnp.zeros((POOL_HALO, POOL_WIDTH), F32)

    halo_ref[POOL_HALO:POOL_HALO + tb, :] = pool_ref[...].astype(F32)
    subs = [(0, tb)]
    nt = (((1,), (1,)), ((), ()))

    for r, n in subs:
        gates = _sigmoid(gate_ref[r:r + n, :].astype(F32) + bg_ref[...])

        tpos = (j * tb + r - PAD_FRONT + 1 + lax.broadcasted_iota(jnp.int32, (n, 1), 0)).astype(F32)
        pools = []
        for gi, win in enumerate(POOL_WINDOWS):
            cs = slice(gi * POOL_GROUP_DIM, (gi + 1) * POOL_GROUP_DIM)
            s = halo_ref[r:r + n + POOL_HALO, cs]
            step = 1
            while step < win:
                s = s[step:] + s[:-step]
                step *= 2
            wsum = s[s.shape[0] - n:]
            cnt = jnp.clip(tpos, 1.0, float(win))
            pooled = wsum / cnt - halo_ref[POOL_HALO + r:POOL_HALO + r + n, cs]
            pools.append(jnp.dot(pooled.astype(BF16), wp_ref[gi], preferred_element_type=F32))
        y_pool = jnp.concatenate(pools, axis=1) * ps_ref[...]

        y_ssd = jnp.dot(yn_ref[r:r + n, :], wso_ref[...], preferred_element_type=F32)
        mixed =(gates[:, :D_MODEL] * y_ssd + gates[:, D_MODEL:] * y_pool).astype(BF16)
        out = jnp.dot(mixed, wo_ref[...], preferred_element_type=F32)
        h1 = _layer_norm(DN_ALPHA * h_ref[r:r + n, :] + out, g_ref[...], b_ref[...])
        h1_ref[r:r + n, :] = h1

        h_hi = h1.astype(BF16)
        h_lo = (h1 - h_hi.astype(F32)).astype(BF16)
        logits = (lax.dot_general(wr_hi_ref[...], h_hi, nt, preferred_element_type=F32)
                  + lax.dot_general(wr_lo_ref[...], h_hi, nt, preferred_element_type=F32)
                  + lax.dot_general(wr_hi_ref[...], h_lo, nt, preferred_element_type=F32))
        route_ref[0, :, r:r + n] = _route(logits)

    halo_ref[0:POOL_HALO, :] = halo_ref[tb:tb + POOL_HALO, :]


def _mix(yn, plain, h, w_ssd_out, w_pool, w_out, b_gate, pool_scale, ln_g, ln_b, wr_hi, wr_lo, batch, frame):
    t = yn.shape[0]
    tb = _pick_tile(frame, 832, CHUNK)
    nblk = frame // tb
    d = D_MODEL

    def full(shape):
        return pl.BlockSpec(shape, lambda b, j: (0,) * len(shape))

    return pl.pallas_call(
        functools.partial(_mix_kernel, tb=tb),
        grid=(batch, nblk),
        in_specs=[
            pl.BlockSpec((tb, SSD_INNER), lambda b, j: (b * nblk + j, 0)),
            pl.BlockSpec((tb, N_BRANCH * d), lambda b, j: (b * nblk + j, PLAIN_BLK_GATE)),
            pl.BlockSpec((tb, POOL_WIDTH), lambda b, j: (b * nblk + j, PLAIN_BLK_POOL)),
            pl.BlockSpec((tb, d), lambda b, j: (b * nblk + j, 0)),
            full((SSD_INNER, d)), full((len(POOL_WINDOWS), POOL_GROUP_DIM, POOL_GROUP_DIM)), full((d, d)),
            full((1, N_BRANCH * d)), full((1, POOL_WIDTH)), full((1, d)), full((1, d)),
            full((N_EXPERTS, d)), full((N_EXPERTS, d)),
        ],
        out_specs=[
            pl.BlockSpec((tb, d), lambda b, j: (b * nblk + j, 0)),
            pl.BlockSpec((1, SUBLANES, tb), lambda b, j: (b * nblk + j, 0, 0)),
        ],
        out_shape=[jax.ShapeDtypeStruct((t, d), F32),
                   jax.ShapeDtypeStruct((batch * nblk, SUBLANES, tb), F32)],
        scratch_shapes=[pltpu.VMEM((POOL_HALO + tb, POOL_WIDTH), F32)],
        compiler_params=_params("parallel", "arbitrary"),
        name="mix",
    )(yn, plain, plain, h, w_ssd_out, w_pool, w_out, b_gate, pool_scale, ln_g, ln_b, wr_hi, wr_lo)


def _row_copies(hbm, idx_ref, base, buf, sem, n, *, to_hbm, wait):
    def body(g, carry):
        for u in range(SUBLANES):
            row = 0 if wait else idx_ref[base + g * SUBLANES + u]
            vm = buf.at[g, pl.ds(u, 1)]
            hb = hbm.at[pl.ds(row, 1)]
            cp = pltpu.make_async_copy(vm, hb, sem) if to_hbm else pltpu.make_async_copy(hb, vm, sem)
            if wait:
                cp.wait()
            else:
                cp.start()
        return carry
    lax.fori_loop(0, n // SUBLANES, body, 0)


def _gather_rows(src_hbm, idx_ref, base, buf, sem, n):
    _row_copies(src_hbm, idx_ref, base, buf, sem, n, to_hbm=False, wait=False)


def _wait_rows(src_hbm, buf, sem, n):
    _row_copies(src_hbm, None, 0, buf, sem, n, to_hbm=False, wait=True)


def _scatter_rows(buf, idx_ref, base, dst_hbm, sem, n):
    _row_copies(dst_hbm, idx_ref, base, buf, sem, n, to_hbm=True, wait=False)


def _wait_scattered(buf, dst_hbm, sem, n):
    _row_copies(dst_hbm, None, 0, buf, sem, n, to_hbm=True, wait=True)


def _dispatch_kernel(pos_ref, zt_ref, zon_ref, nu_ref, h_ref, w_ref, x_hbm, stage, zbuf, sem, zsem, *,
                     tm, n, n_tiles):
    i = pl.program_id(0)
    slot = i % 2

    @pl.when(i == 0)
    def _():
        zbuf[...] = jnp.zeros_like(zbuf)

        def zero_tile(tile):
            row0 = pl.multiple_of(tile * EXPERT_TM, EXPERT_TM)
            cp = pltpu.make_async_copy(zbuf, x_hbm.at[pl.ds(row0, EXPERT_TM)], zsem)
            cp.start()
            cp.wait()

        for c in range(N_CLASSES):
            pl.when(zon_ref[c] > 0)(functools.partial(zero_tile, zt_ref[c]))
            pl.when(nu_ref[0] + c < n_tiles)(functools.partial(zero_tile, nu_ref[0] + c))

    @pl.when(i >= 2)
    def _():
        _wait_scattered(stage.at[slot], x_hbm, sem.at[slot], tm)

    stage[slot, :, :, 0:D_MODEL] = h_ref[...]
    stage[slot, :, :, D_MODEL:D_MODEL + LANES] = w_ref[...]
    _scatter_rows(stage.at[slot], pos_ref, i * tm, x_hbm, sem.at[slot], tm)

    @pl.when(i == n - 1)
    def _():
        if n >= 2:
            _wait_scattered(stage.at[1 - slot], x_hbm, sem.at[1 - slot], tm)
        _wait_scattered(stage.at[slot], x_hbm, sem.at[slot], tm)


def _dispatch(h1, wcols, pos, zero_tile, zero_on, n_used, n_slots):
    t8, _, d = h1.shape
    t = t8 * SUBLANES
    tm = _pick_tile(t, 512, 16)
    g = tm // SUBLANES
    grid_spec = pltpu.PrefetchScalarGridSpec(
        num_scalar_prefetch=4,
        grid=(t // tm,),
        in_specs=[pl.BlockSpec((g, SUBLANES, d), lambda i, *_: (i, 0, 0)),
                  pl.BlockSpec((g, SUBLANES, LANES), lambda i, *_: (i, 0, 0))],
        out_specs=pl.BlockSpec(memory_space=pl.ANY),
        scratch_shapes=[pltpu.VMEM((2, g, SUBLANES, d + LANES), F32), pltpu.VMEM((EXPERT_TM, d + LANES), F32),
                        pltpu.SemaphoreType.DMA((2,)), pltpu.SemaphoreType.DMA(())],
    )
    return pl.pallas_call(
        functools.partial(_dispatch_kernel, tm=tm, n=t // tm, n_tiles=n_slots // EXPERT_TM),
        grid_spec=grid_spec,
        out_shape=jax.ShapeDtypeStruct((n_slots, d + LANES), F32),
        compiler_params=_params("arbitrary"),
        name="dispatch",
    )(pos, zero_tile, zero_on, n_used, h1, wcols)


def _expert_kernel(ea_ref, eb_ref, nu_ref, newa_ref, newb_ref, x_ref, wga_ref, wua_ref, wda_ref,
                   wgb_ref, wub_ref, wdb_ref, o_ref, wg_s, wu_s, wd_s):
    i = pl.program_id(0)

    @pl.when(i < nu_ref[0])
    def _():
        @pl.when(newa_ref[i] > 0)
        def _():
            wg_s[0] = wga_ref[0].astype(BF16)
            wu_s[0] = wua_ref[0].astype(BF16)
            wd_s[0] = wda_ref[0].astype(BF16)

        @pl.when(newb_ref[i] > 0)
        def _():
            wg_s[1] = wgb_ref[0].astype(BF16)
            wu_s[1] = wub_ref[0].astype(BF16)
            wd_s[1] = wdb_ref[0].astype(BF16)

        x = x_ref[:, 0:D_MODEL].astype(BF16)
        wts = x_ref[:, D_MODEL:D_MODEL + LANES]

        def mlp(e):
            hid = _silu(jnp.dot(x, wg_s[e], preferred_element_type=F32)) * jnp.dot(x, wu_s[e], preferred_element_type=F32)
            return jnp.dot(hid.astype(BF16), wd_s[e], preferred_element_type=F32)

        o_ref[...] = wts[:, 0:1] * mlp(0) + wts[:, 1:2] * mlp(1)

    @pl.when(i >= nu_ref[0])
    def _():
        o_ref[...] = jnp.zeros_like(o_ref)


def _experts(x_slots, tile_ea, tile_eb, n_used, w_gate, w_up, w_down):
    n_slots, dw = x_slots.shape
    d = D_MODEL
    n_tiles = n_slots // EXPERT_TM
    first = jnp.ones((1,), jnp.int32)
    new_a = jnp.concatenate([first, (tile_ea[1:] != tile_ea[:-1]).astype(jnp.int32)])
    new_b = jnp.concatenate([first, (tile_eb[1:] != tile_eb[:-1]).astype(jnp.int32)])

    def wspec(shape, which):
        if which == 0:
            return pl.BlockSpec(shape, lambda i, ea, eb, *_: (ea[i], 0, 0))
        return pl.BlockSpec(shape, lambda i, ea, eb, *_: (eb[i], 0, 0))

    up = (1, d, D_EXPERT)
    down = (1, D_EXPERT, d)
    grid_spec = pltpu.PrefetchScalarGridSpec(
        num_scalar_prefetch=5,
        grid=(n_tiles,),
        in_specs=[
            pl.BlockSpec((EXPERT_TM, dw), lambda i, *_: (i, 0)),
            wspec(up, 0), wspec(up, 0), wspec(down, 0),
            wspec(up, 1), wspec(up, 1), wspec(down, 1),
        ],
        out_specs=pl.BlockSpec((EXPERT_TM, d), lambda i, *_: (i, 0)),
        scratch_shapes=[pltpu.VMEM((2, d, D_EXPERT), BF16), pltpu.VMEM((2, d, D_EXPERT), BF16),
                        pltpu.VMEM((2, D_EXPERT, d), BF16)],
    )
    return pl.pallas_call(
        _expert_kernel,
        grid_spec=grid_spec,
        out_shape=jax.ShapeDtypeStruct((n_slots, d), F32),
        compiler_params=_params("arbitrary"),
        name="experts",
    )(tile_ea, tile_eb, n_used, new_a, new_b, x_slots, w_gate, w_up, w_down, w_gate, w_up, w_down)


def _combine_kernel(pos_ref, h_ref, m_ref, y_hbm, g_ref, b_ref, h2_ref, hb_ref, ybuf, sem, *, tm):
    i = pl.program_id(0)
    n = pl.num_programs(0)
    slot = i % 2

    @pl.when(i == 0)
    def _():
        _gather_rows(y_hbm, pos_ref, 0, ybuf.at[0], sem.at[0], tm)

    @pl.when(i + 1 < n)
    def _():
        _gather_rows(y_hbm, pos_ref, (i + 1) * tm, ybuf.at[1 - slot], sem.at[1 - slot], tm)

    _wait_rows(y_hbm, ybuf.at[slot], sem.at[slot], tm)
    y = ybuf[slot].reshape(tm, D_MODEL)
    h2 = _layer_norm(DN_ALPHA * h_ref[...] + y, g_ref[...], b_ref[...])
    h2_ref[...] = h2
    hb_ref[...] = (h2 * m_ref[...]).astype(BF16)


def _combine_out_kernel(pos_ref, h_ref, y_hbm, g_ref, b_ref, out_hbm, ybuf, obuf, sem, osem, *, tm, nblk, n):
    s = pl.program_id(0)
    slot = s % 2
    put = functools.partial(_frame_block_copy, out_hbm, obuf, osem, blk=tm, nblk=nblk, to_hbm=True)

    @pl.when(s == 0)
    def _():
        _gather_rows(y_hbm, pos_ref, 0, ybuf.at[0], sem.at[0], tm)

    @pl.when(s + 1 < n)
    def _():
        _gather_rows(y_hbm, pos_ref, (s + 1) * tm, ybuf.at[1 - slot], sem.at[1 - slot], tm)

    @pl.when(s >= 2)
    def _():
        put(s - 2, slot, wait=True)

    _wait_rows(y_hbm, ybuf.at[slot], sem.at[slot], tm)
    y = ybuf[slot].reshape(tm, D_MODEL)
    obuf[slot] = _layer_norm(DN_ALPHA * h_ref[...] + y, g_ref[...], b_ref[...])
    put(s, slot, wait=False)

    @pl.when(s == n - 1)
    def _():
        if n >= 2:
            put(s - 1, 1 - slot, wait=True)
        put(s, slot, wait=True)


def _combine_out(h1, y_slots, pos, ln_g, ln_b, batch, seq):
    t, d = h1.shape
    frame = t // batch
    tm = _pick_tile(frame, 512, CHUNK)
    nblk = frame // tm
    n = t // tm
    row = pl.BlockSpec((tm, d), lambda i, pos: (i, 0))
    vec = pl.BlockSpec((1, d), lambda i, pos: (0, 0))
    grid_spec = pltpu.PrefetchScalarGridSpec(
        num_scalar_prefetch=1,
        grid=(n,),
        in_specs=[row, pl.BlockSpec(memory_space=pl.ANY), vec, vec],
        out_specs=pl.BlockSpec(memory_space=pl.ANY),
        scratch_shapes=[pltpu.VMEM((2, tm // SUBLANES, SUBLANES, d), F32), pltpu.VMEM((2, tm, d), F32),
                        pltpu.SemaphoreType.DMA((2,)), pltpu.SemaphoreType.DMA((2,))],
    )
    return pl.pallas_call(
        functools.partial(_combine_out_kernel, tm=tm, nblk=nblk, n=n),
        grid_spec=grid_spec,
        out_shape=jax.ShapeDtypeStruct((batch, seq, d), F32),
        compiler_params=_params("arbitrary"),
        name="combine_out",
    )(pos, h1, y_slots, ln_g, ln_b)


def _combine(h1, rowmask, y_slots, pos, ln_g, ln_b):
    t, d = h1.shape
    tm = _pick_tile(t, 512, 16)
    row = pl.BlockSpec((tm, d), lambda i, pos: (i, 0))
    vec = pl.BlockSpec((1, d), lambda i, pos: (0, 0))
    grid_spec = pltpu.PrefetchScalarGridSpec(
        num_scalar_prefetch=1,
        grid=(t // tm,),
        in_specs=[row, pl.BlockSpec((tm, 1), lambda i, pos: (i, 0)), pl.BlockSpec(memory_space=pl.ANY), vec, vec],
        out_specs=[row, row],
        scratch_shapes=[pltpu.VMEM((2, tm // SUBLANES, SUBLANES, d), F32), pltpu.SemaphoreType.DMA((2,))],
    )
    return pl.pallas_call(
        functools.partial(_combine_kernel, tm=tm),
        grid_spec=grid_spec,
        out_shape=[jax.ShapeDtypeStruct((t, d), F32), jax.ShapeDtypeStruct((t, d), BF16)],
        compiler_params=_params("arbitrary"),
        name="combine",
    )(pos, h1, rowmask, y_slots, ln_g, ln_b)


_PAIR_A = np.array([p[0] for p in PAIRS], np.int32)
_PAIR_B = np.array([p[1] for p in PAIRS], np.int32)


def _dispatch_plan(route, t, layer):
    route = jnp.transpose(route, (1, 0, 2)).reshape(SUBLANES, t)
    cls = route[0].astype(jnp.int32)
    onehot = (cls[:, None] == jnp.arange(N_CLASSES, dtype=jnp.int32)[None, :]).astype(jnp.int32)
    csum = jnp.cumsum(onehot, axis=0)
    counts = csum[-1]
    rank = jnp.sum(onehot * csum, axis=1) - 1
    tiles_per = (counts + EXPERT_TM - 1) // EXPERT_TM
    tile_end = jnp.cumsum(tiles_per)
    cls_start = (tile_end - tiles_per) * EXPERT_TM
    pos = cls_start[cls] + rank
    n_tiles = -(-t // EXPERT_TM) + N_CLASSES
    n_used = tile_end[-1]
    tile_ids = jnp.minimum(jnp.arange(n_tiles, dtype=jnp.int32), n_used - 1)
    tile_cls = jnp.sum((tile_ids[:, None] >= tile_end[None, :]).astype(jnp.int32), axis=1)
    grp = tile_cls // len(PAIRS)
    pair = tile_cls % len(PAIRS)
    tile_ea = layer * N_EXPERTS + grp * EXPERTS_PER_GROUP + jnp.asarray(_PAIR_A)[pair]
    tile_eb = layer * N_EXPERTS + grp * EXPERTS_PER_GROUP + jnp.asarray(_PAIR_B)[pair]
    wcols = jnp.pad(route[1:3].T, ((0, 0), (0, LANES - 2)))
    zero_tile = jnp.maximum(tile_end - 1, 0).astype(jnp.int32)
    zero_on = (tiles_per > 0).astype(jnp.int32)
    return (pos.astype(jnp.int32), wcols, tile_ea, tile_eb, n_used.reshape(1).astype(jnp.int32),
            zero_tile, zero_on, n_tiles * EXPERT_TM)


def kernel(x, meta_tokens, ln_in_g, ln_in_b, w_router, w_in, conv_w, conv_b, dt_bias, a_log, d_skip,
           ssd_norm_w, w_ssd_out, w_pool, pool_scale, b_gate, w_out, ln1_g, ln1_b, w_exp_gate, w_exp_up,
           w_exp_down, ln2_g, ln2_b):
    batch, seq, d = x.shape
    frame = FRAME_HEAD + seq
    assert d == D_MODEL and frame % CHUNK == 0
    t = batch * frame
    depth = w_in.shape[0]

    head = jnp.concatenate([jnp.zeros((PAD_FRONT, d), x.dtype), meta_tokens.astype(x.dtype)], axis=0)
    rowmask = jnp.asarray(np.tile(np.arange(frame) >= PAD_FRONT, batch).astype(np.float32).reshape(t, 1))
    h, hb = _embed_ln(x, head, ln_in_g, ln_in_b)

    wr_t = w_router.T
    wr_hi = wr_t.astype(BF16)
    wr_lo = (wr_t - wr_hi.astype(F32)).astype(BF16)

    def pad_lanes(v):
        return jnp.pad(v, (0, LANES - v.shape[0])).reshape(1, LANES)

    wg_all = w_exp_gate.reshape(depth * N_EXPERTS, d, D_EXPERT)
    wu_all = w_exp_up.reshape(depth * N_EXPERTS, d, D_EXPERT)
    wd_all = w_exp_down.reshape(depth * N_EXPERTS, D_EXPERT, d)

    for l in range(depth):
        w = w_in[l]
        w_xbc = w[:, COL_XBC:COL_DT].astype(BF16)
        w_plain = jnp.concatenate([w[:, COL_Z:COL_XBC], w[:, COL_GATE:], w[:, COL_POOL:COL_GATE]], axis=1).astype(BF16)
        w_dt = jnp.pad(w[:, COL_DT:COL_POOL], ((0, 0), (0, LANES - SSD_HEADS))).astype(BF16)
        xact, plain, dt_raw = _in_proj(hb, w_xbc, w_plain, w_dt, conv_w[l], conv_b[l].reshape(1, SSD_XBC))

        yn = _ssd(
            xact, plain, dt_raw, pad_lanes(dt_bias[l]),
            pad_lanes(-jnp.exp(a_log[l])), jnp.repeat(d_skip[l], SSD_HEAD_DIM).reshape(1, SSD_INNER),
            ssd_norm_w[l].reshape(1, SSD_INNER), batch, frame)

        h1, route = _mix(
            yn, plain, h, w_ssd_out[l].astype(BF16), w_pool[l].astype(BF16), w_out[l].astype(BF16),
            b_gate[l].reshape(1, N_BRANCH * d), pool_scale[l].reshape(1, POOL_WIDTH),
            ln1_g[l].reshape(1, d), ln1_b[l].reshape(1, d), wr_hi, wr_lo, batch, frame)

        pos, wcols, tile_ea, tile_eb, n_used, zero_tile, zero_on, n_slots = _dispatch_plan(route, t, l)
        x_slots = _dispatch(h1.reshape(t // SUBLANES, SUBLANES, d), wcols.reshape(t // SUBLANES, SUBLANES, LANES),
                            pos, zero_tile, zero_on, n_used, n_slots)
        y_slots = _experts(x_slots, tile_ea, tile_eb, n_used, wg_all, wu_all, wd_all)
        if l == depth - 1:
            return _combine_out(h1, y_slots, pos, ln2_g[l].reshape(1, d), ln2_b[l].reshape(1, d), batch, seq)
        h, hb = _combine(h1, rowmask, y_slots, pos, ln2_g[l].reshape(1, d), ln2_b[l].reshape(1, d))
```

```python
import functools

import jax
import jax.numpy as jnp
import numpy as np
from jax import lax
from jax.experimental import pallas as pl
from jax.experimental.pallas import tpu as pltpu

F32 = jnp.float32
BF16 = jnp.bfloat16

D_MODEL = 1024
CHUNK = 64
N_META = 16
PAD_FRONT = CHUNK - N_META
FRAME_HEAD = PAD_FRONT + N_META

SSD_INNER = 2048
SSD_HEAD_DIM = 64
SSD_HEADS = 32
SSD_GROUPS = 8
SSD_HPG = 4
SSD_STATE = 128
SSD_CONV = 4
SSD_GN = SSD_GROUPS * SSD_STATE
SSD_XBC = SSD_INNER + 2 * SSD_GN
SSD_GROUP_DIM = SSD_HPG * SSD_HEAD_DIM

POOL_WIDTH = 1024
POOL_WINDOWS = (2, 4, 8, 16)
POOL_GROUP_DIM = 256
POOL_HALO = 16

N_BRANCH = 2
COL_Z = 0
COL_XBC = COL_Z + SSD_INNER
COL_DT = COL_XBC + SSD_XBC
COL_POOL = COL_DT + SSD_HEADS
COL_GATE = COL_POOL + POOL_WIDTH
IN_COLS = COL_GATE + N_BRANCH * D_MODEL

N_EXPERTS = 16
N_EXPERT_GROUPS = 4
EXPERTS_PER_GROUP = 4
D_EXPERT = 512
PAIRS = ((0, 1), (0, 2), (0, 3), (1, 3), (1, 2), (3, 2))
N_CLASSES = N_EXPERT_GROUPS * len(PAIRS)

DEPTH = 2
DN_ALPHA = (2.0 * DEPTH) ** 0.25
LN_EPS = 1e-5
RMS_EPS = 1e-5

LANES = 128
SUBLANES = 8
VMEM_LIMIT = 56 * 1024 * 1024

PROJ_TN = 1024
PLAIN_COLS = SSD_INNER + N_BRANCH * D_MODEL + POOL_WIDTH
PLAIN_BLK_Z = 0
PLAIN_BLK_GATE = 1
PLAIN_BLK_POOL = (SSD_INNER + N_BRANCH * D_MODEL) // POOL_WIDTH

EXPERT_TM = 256


def _pick_tile(n, target, mult):
    best = None
    for t in range(mult, min(n, target) + 1, mult):
        if n % t == 0:
            best = t
    assert best is not None, (n, target, mult)
    return best


def _params(*sem):
    return pltpu.CompilerParams(dimension_semantics=sem, vmem_limit_bytes=VMEM_LIMIT)


def _sigmoid(x):
    return 0.5 + 0.5 * jnp.tanh(0.5 * x)


def _silu(x):
    hx = 0.5 * x
    return hx + hx * jnp.tanh(hx)


def _layer_norm(x, g, b):
    mu = jnp.mean(x, axis=-1, keepdims=True)
    xc = x - mu
    var = jnp.mean(xc * xc, axis=-1, keepdims=True)
    return xc * lax.rsqrt(var + LN_EPS) * g + b


def _frame_block_copy(hbm, buf, sem, step, slot, *, blk, nblk, to_hbm, wait):
    b = step // nblk
    j = step - b * nblk

    def run(vm, hb):
        cp = pltpu.make_async_copy(vm, hb, sem.at[slot]) if to_hbm else pltpu.make_async_copy(hb, vm, sem.at[slot])
        if wait:
            cp.wait()
        else:
            cp.start()

    if blk > FRAME_HEAD:
        @pl.when(j == 0)
        def _():
            n = blk - FRAME_HEAD
            run(buf.at[slot, pl.ds(FRAME_HEAD, n)], hbm.at[b, pl.ds(0, n)])

    @pl.when(j > 0)
    def _():
        start = pl.multiple_of(j * blk - FRAME_HEAD, CHUNK)
        run(buf.at[slot], hbm.at[b, pl.ds(start, blk)])


def _embed_kernel(x_hbm, head_ref, g_ref, b_ref, h_ref, hb_ref, xbuf, sem, *, blk, nblk, n):
    s = pl.program_id(0)
    slot = s % 2
    copy = functools.partial(_frame_block_copy, x_hbm, xbuf, sem, blk=blk, nblk=nblk, to_hbm=False)

    @pl.when(s == 0)
    def _():
        copy(0, 0, wait=False)

    @pl.when(s + 1 < n)
    def _():
        copy(s + 1, 1 - slot, wait=False)

    copy(s, slot, wait=True)
    first = s % nblk == 0

    @pl.when(first)
    def _():
        xbuf[slot, 0:FRAME_HEAD, :] = head_ref[...]

    y = _layer_norm(xbuf[slot], g_ref[...], b_ref[...])
    h_ref[...] = y
    row = lax.broadcasted_iota(jnp.int32, (blk, 1), 0)
    keep = jnp.logical_or(jnp.logical_not(first), row >= PAD_FRONT)
    hb_ref[...] = jnp.where(keep, y, 0.0).astype(BF16)


def _embed_ln(x, head, g, b):
    batch, seq, d = x.shape
    frame = FRAME_HEAD + seq
    blk = _pick_tile(frame, 832, CHUNK)
    nblk = frame // blk
    n = batch * nblk
    row = pl.BlockSpec((blk, d), lambda s: (s, 0))
    vec = pl.BlockSpec((1, d), lambda s: (0, 0))
    return pl.pallas_call(
        functools.partial(_embed_kernel, blk=blk, nblk=nblk, n=n),
        grid=(n,),
        in_specs=[pl.BlockSpec(memory_space=pl.ANY), pl.BlockSpec((FRAME_HEAD, d), lambda s: (0, 0)), vec, vec],
        out_specs=[row, row],
        out_shape=[jax.ShapeDtypeStruct((batch * frame, d), F32), jax.ShapeDtypeStruct((batch * frame, d), BF16)],
        scratch_shapes=[pltpu.VMEM((2, blk, d), F32), pltpu.SemaphoreType.DMA((2,))],
        compiler_params=_params("arbitrary"),
        name="embed_ln",
    )(x, head, g.reshape(1, d), b.reshape(1, d))


CONV_TILES = SSD_XBC // PROJ_TN
Z_TILES = SSD_INNER // PROJ_TN
assert COL_XBC % PROJ_TN == 0 and COL_Z % PROJ_TN == 0 and Z_TILES <= CONV_TILES


def _in_proj_kernel(x_ref, wx_ref, wz_ref, wp_ref, wdt_ref, cw_ref, cb_ref, xo_ref, po_ref, dt_ref, tail_ref, *,
                    slab):
    i = pl.program_id(0)
    j = pl.program_id(1)
    tm = x_ref.shape[0]

    @pl.when(j == 0)
    def _():
        dt_ref[...] = jnp.dot(x_ref[...], wdt_ref[...], preferred_element_type=F32)

    @pl.when(j >= CONV_TILES)
    def _():
        po_ref[...] = jnp.dot(x_ref[...], wp_ref[...], preferred_element_type=F32).astype(BF16)

    @pl.when(j < CONV_TILES)
    def _():
        jt = jnp.minimum(j, CONV_TILES - 1)

        @pl.when(i == 0)
        def _():
            tail_ref[jt] = jnp.zeros((SUBLANES, PROJ_TN), F32)

        prev8 = tail_ref[jt]
        w = wx_ref[...].astype(BF16)
        wp = jnp.where(j < Z_TILES, wz_ref[...].astype(BF16), wp_ref[...])
        sub = lax.broadcasted_iota(jnp.int32, (1, SUBLANES, PROJ_TN), 1)
        cw = [cw_ref[k:k + 1, :].reshape(1, 1, PROJ_TN) for k in range(SSD_CONV)]
        cb = cb_ref[...].reshape(1, 1, PROJ_TN)

        def shift_rows(v3, head8, s):
            r = pltpu.roll(v3, s, axis=1)
            above = jnp.concatenate([pltpu.roll(head8, s, axis=0)[None], r[:-1]], axis=0)
            return jnp.where(sub < s, above, r)

        for s in range(tm // slab):
            rows = slice(s * slab, (s + 1) * slab)
            acc = jnp.dot(x_ref[rows, :], w, preferred_element_type=F32)
            x0 = acc.reshape(slab // SUBLANES, SUBLANES, PROJ_TN)
            x2 = shift_rows(x0, prev8, 2)
            odd = cw[2] * x0 + cw[0] * x2
            odd_head = cw[2][0] * prev8 + cw[0][0] * pltpu.roll(prev8, 2, axis=0)
            conv = cb + cw[3] * x0 + cw[1] * x2 + shift_rows(odd, odd_head, 1)
            xo_ref[rows, :] = _silu(conv).reshape(slab, PROJ_TN).astype(BF16)
            prev8 = acc[slab - SUBLANES:]
            po_ref[rows, :] = jnp.dot(x_ref[rows, :], wp, preferred_element_type=F32).astype(BF16)
        tail_ref[jt] = prev8


def _in_proj(hb, w_in, layer, w_gp, w_dt, conv_w, conv_b):
    t, d = hb.shape
    tm = _pick_tile(t, 1664, 16)
    slab = _pick_tile(tm, 256, 16)

    def conv_col(i, j):
        return (0, jnp.minimum(j, CONV_TILES - 1))

    return pl.pallas_call(
        functools.partial(_in_proj_kernel, slab=slab),
        grid=(t // tm, PLAIN_COLS // PROJ_TN),
        in_specs=[
            pl.BlockSpec((tm, d), lambda i, j: (i, 0)),
            pl.BlockSpec((None, d, PROJ_TN), lambda i, j: (layer, 0, COL_XBC // PROJ_TN + jnp.minimum(j, CONV_TILES - 1))),
            pl.BlockSpec((None, d, PROJ_TN), lambda i, j: (layer, 0, COL_Z // PROJ_TN + jnp.minimum(j, Z_TILES - 1))),
            pl.BlockSpec((d, PROJ_TN), lambda i, j: (0, jnp.maximum(j - Z_TILES, 0))),
            pl.BlockSpec((d, LANES), lambda i, j: (0, 0)),
            pl.BlockSpec((SSD_CONV, PROJ_TN), conv_col),
            pl.BlockSpec((1, PROJ_TN), conv_col),
        ],
        out_specs=[
            pl.BlockSpec((tm, PROJ_TN), lambda i, j: (i, jnp.minimum(j, CONV_TILES - 1))),
            pl.BlockSpec((tm, PROJ_TN), lambda i, j: (i, j)),
            pl.BlockSpec((tm, LANES), lambda i, j: (i, 0)),
        ],
        out_shape=[jax.ShapeDtypeStruct((t, SSD_XBC), BF16), jax.ShapeDtypeStruct((t, PLAIN_COLS), BF16),
                   jax.ShapeDtypeStruct((t, LANES), F32)],
        scratch_shapes=[pltpu.VMEM((CONV_TILES, SUBLANES, PROJ_TN), F32)],
        compiler_params=_params("arbitrary", "arbitrary"),
        name="in_proj",
    )(hb, w_in, w_in, w_gp, w_dt, conv_w, conv_b)


def _pair_select(lane_lo, col, h0):
    return jnp.where(lane_lo, col[:, h0:h0 + 1], col[:, h0 + 1:h0 + 2])


def _ssd_kernel(xbc_ref, z_ref, dt_ref, dtb_ref, a_ref, dsk_ref, nw_ref, o_ref, prev_ref, y_ref, *, tb):
    j = pl.program_id(1)

    @pl.when(j == 0)
    def _():
        prev_ref[...] = jnp.zeros_like(prev_ref)

    row_i = lax.broadcasted_iota(jnp.int32, (CHUNK, LANES), 0)
    row_c = lax.broadcasted_iota(jnp.int32, (CHUNK, 1), 0)
    lane_i = lax.broadcasted_iota(jnp.int32, (CHUNK, LANES), 1)
    lane_lo = lane_i < SSD_HEAD_DIM
    causal2 = row_i >= jnp.where(lane_lo, lane_i, lane_i - SSD_HEAD_DIM)
    lane_lo_row = lax.broadcasted_iota(jnp.int32, (1, LANES), 1) < SSD_HEAD_DIM

    def chunk(c, carry):
        r0 = pl.multiple_of(c * CHUNK, CHUNK)
        rows = pl.ds(r0, CHUNK)
        valid = (j * tb + r0 + row_c) >= PAD_FRONT

        dtr = dt_ref[rows, :] + dtb_ref[...]
        dt = jnp.maximum(dtr, 0.0) + jnp.log1p(jnp.exp(-jnp.abs(dtr)))
        dt = jnp.where(valid, dt, 0.0)
        a_cs = dt * a_ref[...]
        for s in (1, 2, 4, 8, 16, 32):
            a_cs = a_cs + jnp.where(row_i >= s, pltpu.roll(a_cs, s, axis=0), 0.0)
        a_last = a_cs[CHUNK - 1:CHUNK, :]
        dtw = dt * jnp.exp(a_last - a_cs)
        cdec = jnp.exp(a_last)
        a_t = jnp.concatenate([a_cs, a_cs], axis=0).T
        dt_t = jnp.concatenate([dt, dt], axis=0).T

        def bmat(g):
            return xbc_ref[rows, SSD_INNER + g * SSD_STATE:SSD_INNER + (g + 1) * SSD_STATE]

        def cmat(g):
            o0 = SSD_INNER + SSD_GN + g * SSD_STATE
            return xbc_ref[rows, o0:o0 + SSD_STATE]

        cb2s, yoffs = [], []
        for g in range(SSD_GROUPS):
            bm, cm = bmat(g), cmat(g)
            cb2s.append(lax.dot_general(cm, jnp.concatenate([bm, bm], axis=0),
                                        (((1,), (1,)), ((), ())), preferred_element_type=F32))
            yoffs.append(jnp.dot(cm, prev_ref[g].astype(BF16), preferred_element_type=F32))

        for g in range(SSD_GROUPS):
            for q in range(SSD_HPG // 2):
                h0 = g * SSD_HPG + 2 * q
                ps = slice(g * SSD_GROUP_DIM + q * LANES, g * SSD_GROUP_DIM + (q + 1) * LANES)
                xh_b = xbc_ref[rows, ps]
                a_col = _pair_select(lane_lo, a_cs, h0)
                a_row = jnp.where(lane_lo_row, a_t[h0:h0 + 1, :], a_t[h0 + 1:h0 + 2, :])
                dt_row = jnp.where(lane_lo_row, dt_t[h0:h0 + 1, :], dt_t[h0 + 1:h0 + 2, :])
                decay = jnp.exp(jnp.where(causal2, a_col - a_row, -jnp.inf))
                m = (cb2s[g] * decay * dt_row).astype(BF16)
                zero = jnp.zeros_like(xh_b)
                rhs = jnp.concatenate([jnp.where(lane_lo, xh_b, zero), jnp.where(lane_lo, zero, xh_b)], axis=0)
                y_diag = jnp.dot(m, rhs, preferred_element_type=F32)
                y_off = yoffs[g][:, q * LANES:(q + 1) * LANES] * jnp.exp(a_col)
                y_ref[:, ps] = y_diag + y_off + dsk_ref[:, ps] * xh_b.astype(F32)

        for g in range(SSD_GROUPS):
            xs_parts, cdec_parts = [], []
            for q in range(SSD_HPG // 2):
                h0 = g * SSD_HPG + 2 * q
                ps = slice(g * SSD_GROUP_DIM + q * LANES, g * SSD_GROUP_DIM + (q + 1) * LANES)
                xs_parts.append((xbc_ref[rows, ps].astype(F32) * _pair_select(lane_lo, dtw, h0)).astype(BF16))
                cdec_parts.append(jnp.where(lane_lo_row, cdec[:, h0:h0 + 1], cdec[:, h0 + 1:h0 + 2]))
            xs_dec = jnp.concatenate(xs_parts, axis=1)
            st = lax.dot_general(bmat(g), xs_dec, (((0,), (0,)), ((), ())), preferred_element_type=F32)
            prev_ref[g] = prev_ref[g] * jnp.concatenate(cdec_parts, axis=1) + st

        for g in range(SSD_GROUPS):
            gs = slice(g * SSD_GROUP_DIM, (g + 1) * SSD_GROUP_DIM)
            yg = y_ref[:, gs] * _silu(z_ref[rows, gs].astype(F32))
            ms = jnp.mean(yg * yg, axis=-1, keepdims=True)
            o_ref[rows, gs] = (yg * lax.rsqrt(ms + RMS_EPS) * nw_ref[:, gs]).astype(BF16)
        return carry

    lax.fori_loop(0, tb // CHUNK, chunk, 0)


def _ssd(xact, plain, dt_raw, dt_bias, a_neg, d_skip_ch, norm_w, batch, frame):
    t = xact.shape[0]
    tb = _pick_tile(frame, 832, CHUNK)
    nblk = frame // tb

    def vec(n):
        return pl.BlockSpec((1, n), lambda b, j: (0, 0))

    return pl.pallas_call(
        functools.partial(_ssd_kernel, tb=tb),
        grid=(batch, nblk),
        in_specs=[
            pl.BlockSpec((tb, SSD_XBC), lambda b, j: (b * nblk + j, 0)),
            pl.BlockSpec((tb, SSD_INNER), lambda b, j: (b * nblk + j, PLAIN_BLK_Z)),
            pl.BlockSpec((tb, LANES), lambda b, j: (b * nblk + j, 0)),
            vec(LANES), vec(LANES), vec(SSD_INNER), vec(SSD_INNER),
        ],
        out_specs=pl.BlockSpec((tb, SSD_INNER), lambda b, j: (b * nblk + j, 0)),
        out_shape=jax.ShapeDtypeStruct((t, SSD_INNER), BF16),
        scratch_shapes=[
            pltpu.VMEM((SSD_GROUPS, SSD_STATE, SSD_GROUP_DIM), F32),
            pltpu.VMEM((CHUNK, SSD_INNER), F32),
        ],
        compiler_params=_params("parallel", "arbitrary"),
        name="ssd",
    )(xact, plain, dt_raw, dt_bias, a_neg, d_skip_ch, norm_w)


def _route(logits):
    mx = jnp.max(logits, axis=0, keepdims=True)
    ex = jnp.exp(logits - mx)
    sc = ex / jnp.sum(ex, axis=0, keepdims=True)
    rows = [sc[e:e + 1, :] for e in range(N_EXPERTS)]
    best = None
    for g in range(N_EXPERT_GROUPS):
        a, b, c, d = rows[4 * g:4 * g + 4]
        hi1, lo1 = jnp.maximum(a, b), jnp.minimum(a, b)
        hi2, lo2 = jnp.maximum(c, d), jnp.minimum(c, d)
        gsc = jnp.maximum(hi1, hi2) + jnp.maximum(jnp.minimum(hi1, hi2), jnp.maximum(lo1, lo2))
        if best is None:
            best, sel = gsc, jnp.zeros_like(gsc)
        else:
            better = gsc > best
            sel = jnp.where(better, float(g), sel)
            best = jnp.maximum(best, gsc)
    v = []
    for i in range(EXPERTS_PER_GROUP):
        vi = rows[i]
        for g in range(1, N_EXPERT_GROUPS):
            vi = jnp.where(sel == float(g), rows[4 * g + i], vi)
        v.append(vi)

    def first_max(u):
        m = jnp.maximum(jnp.maximum(u[0], u[1]), jnp.maximum(u[2], u[3]))
        idx = jnp.where(u[0] == m, 0.0, jnp.where(u[1] == m, 1.0, jnp.where(u[2] == m, 2.0, 3.0)))
        return m, idx

    m1, i1 = first_max(v)
    u = [jnp.where(i1 == float(i), -1.0, v[i]) for i in range(EXPERTS_PER_GROUP)]
    m2, i2 = first_max(u)
    den = m1 + m2
    w1, w2 = m1 / den, m2 / den
    lo = jnp.minimum(i1, i2)
    hi = jnp.maximum(i1, i2)
    w_lo = jnp.where(i1 < i2, w1, w2)
    w_hi = jnp.where(i1 < i2, w2, w1)
    pair = jnp.where(lo == 0.0, hi - 1.0, jnp.where(lo == 1.0, jnp.where(hi == 3.0, 3.0, 4.0), 5.0))
    swap = pair == 5.0
    w_a = jnp.where(swap, w_hi, w_lo)
    w_b = jnp.where(swap, w_lo, w_hi)
    cls = sel * float(len(PAIRS)) + pair
    zero = jnp.zeros_like(cls)
    return jnp.concatenate([cls, w_a, w_b, zero, zero, zero, zero, zero], axis=0)


def _mix_kernel(yn_ref, gate_ref, pool_ref, h_ref, wso_ref, wp_ref, wo_ref, bg_ref, ps_ref, g_ref, b_ref,
                wr_hi_ref, wr_lo_ref, h1_ref, route_ref, halo_ref, *, tb):
    j = pl.program_id(1)

    @pl.when(j == 0)
    def _():
        halo_ref[0:POOL_HALO, :] = jnp.zeros((POOL_HALO, POOL_WIDTH), F32)

    halo_ref[POOL_HALO:POOL_HALO + tb, :] = pool_ref[...].astype(F32)
    subs = [(0, tb)]
    nt = (((1,), (1,)), ((), ()))

    for r, n in subs:
        gates = _sigmoid(gate_ref[r:r + n, :].astype(F32) + bg_ref[...])

        tpos = (j * tb + r - PAD_FRONT + 1 + lax.broadcasted_iota(jnp.int32, (n, 1), 0)).astype(F32)
        pools = []
        for gi, win in enumerate(POOL_WINDOWS):
            cs = slice(gi * POOL_GROUP_DIM, (gi + 1) * POOL_GROUP_DIM)
            s = halo_ref[r:r + n + POOL_HALO, cs]
            step = 1
            while step < win:
                s = s[step:] + s[:-step]
                step *= 2
            wsum = s[s.shape[0] - n:]
            cnt = jnp.clip(tpos, 1.0, float(win))
            pooled = wsum / cnt - halo_ref[POOL_HALO + r:POOL_HALO + r + n, cs]
            pools.append(jnp.dot(pooled.astype(BF16), wp_ref[gi], preferred_element_type=F32))
        y_pool = jnp.concatenate(pools, axis=1) * ps_ref[...]

        y_ssd = jnp.dot(yn_ref[r:r + n, :], wso_ref[...], preferred_element_type=F32)
        mixed =(gates[:, :D_MODEL] * y_ssd + gates[:, D_MODEL:] * y_pool).astype(BF16)
        out = jnp.dot(mixed, wo_ref[...], preferred_element_type=F32)
        h1 = _layer_norm(DN_ALPHA * h_ref[r:r + n, :] + out, g_ref[...], b_ref[...])
        h1_ref[r:r + n, :] = h1

        h_hi = h1.astype(BF16)
        h_lo = (h1 - h_hi.astype(F32)).astype(BF16)
        logits = (lax.dot_general(wr_hi_ref[...], h_hi, nt, preferred_element_type=F32)
                  + lax.dot_general(wr_lo_ref[...], h_hi, nt, preferred_element_type=F32)
                  + lax.dot_general(wr_hi_ref[...], h_lo, nt, preferred_element_type=F32))
        route_ref[0, :, r:r + n] = _route(logits)

    halo_ref[0:POOL_HALO, :] = halo_ref[tb:tb + POOL_HALO, :]


def _mix(yn, plain, h, w_ssd_out, w_pool, w_out, b_gate, pool_scale, ln_g, ln_b, wr_hi, wr_lo, batch, frame):
    t = yn.shape[0]
    tb = _pick_tile(frame, 832, CHUNK)
    nblk = frame // tb
    d = D_MODEL

    def full(shape):
        return pl.BlockSpec(shape, lambda b, j: (0,) * len(shape))

    return pl.pallas_call(
        functools.partial(_mix_kernel, tb=tb),
        grid=(batch, nblk),
        in_specs=[
            pl.BlockSpec((tb, SSD_INNER), lambda b, j: (b * nblk + j, 0)),
            pl.BlockSpec((tb, N_BRANCH * d), lambda b, j: (b * nblk + j, PLAIN_BLK_GATE)),
            pl.BlockSpec((tb, POOL_WIDTH), lambda b, j: (b * nblk + j, PLAIN_BLK_POOL)),
            pl.BlockSpec((tb, d), lambda b, j: (b * nblk + j, 0)),
            full((SSD_INNER, d)), full((len(POOL_WINDOWS), POOL_GROUP_DIM, POOL_GROUP_DIM)), full((d, d)),
            full((1, N_BRANCH * d)), full((1, POOL_WIDTH)), full((1, d)), full((1, d)),
            full((N_EXPERTS, d)), full((N_EXPERTS, d)),
        ],
        out_specs=[
            pl.BlockSpec((tb, d), lambda b, j: (b * nblk + j, 0)),
            pl.BlockSpec((1, SUBLANES, tb), lambda b, j: (b * nblk + j, 0, 0)),
        ],
        out_shape=[jax.ShapeDtypeStruct((t, d), F32),
                   jax.ShapeDtypeStruct((batch * nblk, SUBLANES, tb), F32)],
        scratch_shapes=[pltpu.VMEM((POOL_HALO + tb, POOL_WIDTH), F32)],
        compiler_params=_params("parallel", "arbitrary"),
        name="mix",
    )(yn, plain, plain, h, w_ssd_out, w_pool, w_out, b_gate, pool_scale, ln_g, ln_b, wr_hi, wr_lo)


def _row_copies(hbm, idx_ref, base, buf, sem, n, *, to_hbm, wait):
    def body(g, carry):
        for u in range(SUBLANES):
            row = 0 if wait else idx_ref[base + g * SUBLANES + u]
            vm = buf.at[g, pl.ds(u, 1)]
            hb = hbm.at[pl.ds(row, 1)]
            cp = pltpu.make_async_copy(vm, hb, sem) if to_hbm else pltpu.make_async_copy(hb, vm, sem)
            if wait:
                cp.wait()
            else:
                cp.start(priority=u % 2)
        return carry
    lax.fori_loop(0, n // SUBLANES, body, 0)


def _gather_rows(src_hbm, idx_ref, base, buf, sem, n):
    _row_copies(src_hbm, idx_ref, base, buf, sem, n, to_hbm=False, wait=False)


def _wait_rows(src_hbm, buf, sem, n):
    _row_copies(src_hbm, None, 0, buf, sem, n, to_hbm=False, wait=True)


def _scatter_rows(buf, idx_ref, base, dst_hbm, sem, n):
    _row_copies(dst_hbm, idx_ref, base, buf, sem, n, to_hbm=True, wait=False)


def _wait_scattered(buf, dst_hbm, sem, n):
    _row_copies(dst_hbm, None, 0, buf, sem, n, to_hbm=True, wait=True)


def _dispatch_kernel(pos_ref, zt_ref, zon_ref, nu_ref, h_hbm, x_hbm, stage, zbuf, isem, sem, zsem, *,
                     tm, n, n_tiles):
    i = pl.program_id(0)
    g = tm // SUBLANES
    slot = i % 3

    def load(step, buf):
        return pltpu.make_async_copy(h_hbm.at[pl.ds(step * g, g)], stage.at[buf], isem.at[buf])

    @pl.when(i == 0)
    def _():
        load(0, 0).start()
        zbuf[...] = jnp.zeros_like(zbuf)

        def zero_tile(tile):
            row0 = pl.multiple_of(tile * EXPERT_TM, EXPERT_TM)
            cp = pltpu.make_async_copy(zbuf, x_hbm.at[pl.ds(row0, EXPERT_TM)], zsem)
            cp.start()
            cp.wait()

        for c in range(N_CLASSES):
            pl.when(zon_ref[c] > 0)(functools.partial(zero_tile, zt_ref[c]))
            pl.when(nu_ref[0] + c < n_tiles)(functools.partial(zero_tile, nu_ref[0] + c))

    nxt = (i + 1) % 3

    @pl.when(i >= 2)
    def _():
        _wait_scattered(stage.at[nxt], x_hbm, sem.at[nxt], tm)

    @pl.when(i + 1 < n)
    def _():
        load(i + 1, nxt).start()

    load(i, slot).wait()
    _scatter_rows(stage.at[slot], pos_ref, i * tm, x_hbm, sem.at[slot], tm)

    @pl.when(i == n - 1)
    def _():
        if n >= 2:
            prev = (i + 2) % 3
            _wait_scattered(stage.at[prev], x_hbm, sem.at[prev], tm)
        _wait_scattered(stage.at[slot], x_hbm, sem.at[slot], tm)


def _dispatch(h1, pos, zero_tile, zero_on, n_used, n_slots):
    t8, _, d = h1.shape
    t = t8 * SUBLANES
    tm = _pick_tile(t, 512, 16)
    g = tm // SUBLANES
    grid_spec = pltpu.PrefetchScalarGridSpec(
        num_scalar_prefetch=4,
        grid=(t // tm,),
        in_specs=[pl.BlockSpec(memory_space=pl.ANY)],
        out_specs=pl.BlockSpec(memory_space=pl.ANY),
        scratch_shapes=[pltpu.VMEM((3, g, SUBLANES, d), F32), pltpu.VMEM((EXPERT_TM, d), F32),
                        pltpu.SemaphoreType.DMA((3,)), pltpu.SemaphoreType.DMA((3,)), pltpu.SemaphoreType.DMA(())],
    )
    return pl.pallas_call(
        functools.partial(_dispatch_kernel, tm=tm, n=t // tm, n_tiles=n_slots // EXPERT_TM),
        grid_spec=grid_spec,
        out_shape=jax.ShapeDtypeStruct((n_slots, d), F32),
        compiler_params=_params("arbitrary"),
        name="dispatch",
    )(pos, zero_tile, zero_on, n_used, h1)


def _pack_pair(a, b):
    a32 = lax.bitcast_convert_type(a.astype(BF16).astype(F32), jnp.uint32)
    b32 = lax.bitcast_convert_type(b.astype(BF16).astype(F32), jnp.uint32)
    return a32 | (b32 >> 16)


def _unpack_pair(w):
    a = lax.bitcast_convert_type(w & jnp.uint32(0xFFFF0000), F32)
    b = lax.bitcast_convert_type(w << 16, F32)
    return a, b


def _expert_kernel(ea_ref, eb_ref, nu_ref, newa_ref, newb_ref, x_ref, wga_ref, wua_ref, wda_ref,
                   wgb_ref, wub_ref, wdb_ref, o_ref, wg_s, wu_s, wd_s):
    i = pl.program_id(0)

    @pl.when(i < nu_ref[0])
    def _():
        @pl.when(newa_ref[i] > 0)
        def _():
            wg_s[0] = wga_ref[0].astype(BF16)
            wu_s[0] = wua_ref[0].astype(BF16)
            wd_s[0] = wda_ref[0].astype(BF16)

        @pl.when(newb_ref[i] > 0)
        def _():
            wg_s[1] = wgb_ref[0].astype(BF16)
            wu_s[1] = wub_ref[0].astype(BF16)
            wd_s[1] = wdb_ref[0].astype(BF16)

        x = x_ref[...].astype(BF16)

        def mlp(e):
            hid = _silu(jnp.dot(x, wg_s[e], preferred_element_type=F32)) * jnp.dot(x, wu_s[e], preferred_element_type=F32)
            return jnp.dot(hid.astype(BF16), wd_s[e], preferred_element_type=F32)

        o_ref[...] = _pack_pair(mlp(0), mlp(1))

    @pl.when(i >= nu_ref[0])
    def _():
        o_ref[...] = jnp.zeros_like(o_ref)


def _experts(x_slots, tile_ea, tile_eb, n_used, w_gate, w_up, w_down):
    n_slots, dw = x_slots.shape
    d = D_MODEL
    n_tiles = n_slots // EXPERT_TM
    first = jnp.ones((1,), jnp.int32)
    new_a = jnp.concatenate([first, (tile_ea[1:] != tile_ea[:-1]).astype(jnp.int32)])
    new_b = jnp.concatenate([first, (tile_eb[1:] != tile_eb[:-1]).astype(jnp.int32)])

    def wspec(shape, which):
        if which == 0:
            return pl.BlockSpec(shape, lambda i, ea, eb, *_: (ea[i], 0, 0))
        return pl.BlockSpec(shape, lambda i, ea, eb, *_: (eb[i], 0, 0))

    up = (1, d, D_EXPERT)
    down = (1, D_EXPERT, d)
    grid_spec = pltpu.PrefetchScalarGridSpec(
        num_scalar_prefetch=5,
        grid=(n_tiles,),
        in_specs=[
            pl.BlockSpec((EXPERT_TM, dw), lambda i, *_: (i, 0)),
            wspec(up, 0), wspec(up, 0), wspec(down, 0),
            wspec(up, 1), wspec(up, 1), wspec(down, 1),
        ],
        out_specs=pl.BlockSpec((EXPERT_TM, d), lambda i, *_: (i, 0)),
        scratch_shapes=[pltpu.VMEM((2, d, D_EXPERT), BF16), pltpu.VMEM((2, d, D_EXPERT), BF16),
                        pltpu.VMEM((2, D_EXPERT, d), BF16)],
    )
    return pl.pallas_call(
        _expert_kernel,
        grid_spec=grid_spec,
        out_shape=jax.ShapeDtypeStruct((n_slots, d), jnp.uint32),
        compiler_params=_params("arbitrary"),
        name="experts",
    )(tile_ea, tile_eb, n_used, new_a, new_b, x_slots, w_gate, w_up, w_down, w_gate, w_up, w_down)


def _moe_sum(ybuf_slot, w_ref, tm):
    ya, yb = _unpack_pair(ybuf_slot.reshape(tm, D_MODEL))
    w = w_ref[...]
    return w[:, 0:1] * ya + w[:, 1:2] * yb


def _combine_kernel(pos_ref, h_ref, w_ref, m_ref, y_hbm, g_ref, b_ref, h2_ref, hb_ref, ybuf, sem, *, tm):
    i = pl.program_id(0)
    n = pl.num_programs(0)
    slot = i % 2

    @pl.when(i == 0)
    def _():
        _gather_rows(y_hbm, pos_ref, 0, ybuf.at[0], sem.at[0], tm)

    @pl.when(i + 1 < n)
    def _():
        _gather_rows(y_hbm, pos_ref, (i + 1) * tm, ybuf.at[1 - slot], sem.at[1 - slot], tm)

    _wait_rows(y_hbm, ybuf.at[slot], sem.at[slot], tm)
    h2 = _layer_norm(DN_ALPHA * h_ref[...] + _moe_sum(ybuf[slot], w_ref, tm), g_ref[...], b_ref[...])
    h2_ref[...] = h2
    hb_ref[...] = (h2 * m_ref[...]).astype(BF16)


def _combine_out_kernel(pos_ref, h_ref, w_ref, y_hbm, g_ref, b_ref, out_hbm, ybuf, obuf, sem, osem, *, tm, nblk, n):
    s = pl.program_id(0)
    slot = s % 2
    put = functools.partial(_frame_block_copy, out_hbm, obuf, osem, blk=tm, nblk=nblk, to_hbm=True)

    @pl.when(s == 0)
    def _():
        _gather_rows(y_hbm, pos_ref, 0, ybuf.at[0], sem.at[0], tm)

    @pl.when(s + 1 < n)
    def _():
        _gather_rows(y_hbm, pos_ref, (s + 1) * tm, ybuf.at[1 - slot], sem.at[1 - slot], tm)

    @pl.when(s >= 2)
    def _():
        put(s - 2, slot, wait=True)

    _wait_rows(y_hbm, ybuf.at[slot], sem.at[slot], tm)
    obuf[slot] = _layer_norm(DN_ALPHA * h_ref[...] + _moe_sum(ybuf[slot], w_ref, tm), g_ref[...], b_ref[...])
    put(s, slot, wait=False)

    @pl.when(s == n - 1)
    def _():
        if n >= 2:
            put(s - 1, 1 - slot, wait=True)
        put(s, slot, wait=True)


def _combine_out(h1, wcols, y_slots, pos, ln_g, ln_b, batch, seq):
    t, d = h1.shape
    frame = t // batch
    tm = _pick_tile(frame, 512, CHUNK)
    nblk = frame // tm
    n = t // tm
    row = pl.BlockSpec((tm, d), lambda i, pos: (i, 0))
    vec = pl.BlockSpec((1, d), lambda i, pos: (0, 0))
    grid_spec = pltpu.PrefetchScalarGridSpec(
        num_scalar_prefetch=1,
        grid=(n,),
        in_specs=[row, pl.BlockSpec((tm, LANES), lambda i, pos: (i, 0)), pl.BlockSpec(memory_space=pl.ANY), vec, vec],
        out_specs=pl.BlockSpec(memory_space=pl.ANY),
        scratch_shapes=[pltpu.VMEM((2, tm // SUBLANES, SUBLANES, d), jnp.uint32), pltpu.VMEM((2, tm, d), F32),
                        pltpu.SemaphoreType.DMA((2,)), pltpu.SemaphoreType.DMA((2,))],
    )
    return pl.pallas_call(
        functools.partial(_combine_out_kernel, tm=tm, nblk=nblk, n=n),
        grid_spec=grid_spec,
        out_shape=jax.ShapeDtypeStruct((batch, seq, d), F32),
        compiler_params=_params("arbitrary"),
        name="combine_out",
    )(pos, h1, wcols, y_slots, ln_g, ln_b)


def _combine(h1, wcols, rowmask, y_slots, pos, ln_g, ln_b):
    t, d = h1.shape
    tm = _pick_tile(t, 512, 16)
    row = pl.BlockSpec((tm, d), lambda i, pos: (i, 0))
    vec = pl.BlockSpec((1, d), lambda i, pos: (0, 0))
    grid_spec = pltpu.PrefetchScalarGridSpec(
        num_scalar_prefetch=1,
        grid=(t // tm,),
        in_specs=[row, pl.BlockSpec((tm, LANES), lambda i, pos: (i, 0)), pl.BlockSpec((tm, 1), lambda i, pos: (i, 0)),
                  pl.BlockSpec(memory_space=pl.ANY), vec, vec],
        out_specs=[row, row],
        scratch_shapes=[pltpu.VMEM((2, tm // SUBLANES, SUBLANES, d), jnp.uint32), pltpu.SemaphoreType.DMA((2,))],
    )
    return pl.pallas_call(
        functools.partial(_combine_kernel, tm=tm),
        grid_spec=grid_spec,
        out_shape=[jax.ShapeDtypeStruct((t, d), F32), jax.ShapeDtypeStruct((t, d), BF16)],
        compiler_params=_params("arbitrary"),
        name="combine",
    )(pos, h1, wcols, rowmask, y_slots, ln_g, ln_b)


_PAIR_A = np.array([p[0] for p in PAIRS], np.int32)
_PAIR_B = np.array([p[1] for p in PAIRS], np.int32)


def _dispatch_plan(route, t, layer):
    route = jnp.transpose(route, (1, 0, 2)).reshape(SUBLANES, t)
    cls = route[0].astype(jnp.int32)
    onehot = (cls[:, None] == jnp.arange(N_CLASSES, dtype=jnp.int32)[None, :]).astype(jnp.int32)
    csum = jnp.cumsum(onehot, axis=0)
    counts = csum[-1]
    rank = jnp.sum(onehot * csum, axis=1) - 1
    tiles_per = (counts + EXPERT_TM - 1) // EXPERT_TM
    tile_end = jnp.cumsum(tiles_per)
    cls_start = (tile_end - tiles_per) * EXPERT_TM
    pos = cls_start[cls] + rank
    n_tiles = -(-t // EXPERT_TM) + N_CLASSES
    n_used = tile_end[-1]
    tile_ids = jnp.minimum(jnp.arange(n_tiles, dtype=jnp.int32), n_used - 1)
    tile_cls = jnp.sum((tile_ids[:, None] >= tile_end[None, :]).astype(jnp.int32), axis=1)
    grp = tile_cls // len(PAIRS)
    pair = tile_cls % len(PAIRS)
    tile_ea = layer * N_EXPERTS + grp * EXPERTS_PER_GROUP + jnp.asarray(_PAIR_A)[pair]
    tile_eb = layer * N_EXPERTS + grp * EXPERTS_PER_GROUP + jnp.asarray(_PAIR_B)[pair]
    wcols = jnp.pad(route[1:3].T, ((0, 0), (0, LANES - 2)))
    zero_tile = jnp.maximum(tile_end - 1, 0).astype(jnp.int32)
    zero_on = (tiles_per > 0).astype(jnp.int32)
    return (pos.astype(jnp.int32), wcols, tile_ea, tile_eb, n_used.reshape(1).astype(jnp.int32),
            zero_tile, zero_on, n_tiles * EXPERT_TM)


def kernel(x, meta_tokens, ln_in_g, ln_in_b, w_router, w_in, conv_w, conv_b, dt_bias, a_log, d_skip,
           ssd_norm_w, w_ssd_out, w_pool, pool_scale, b_gate, w_out, ln1_g, ln1_b, w_exp_gate, w_exp_up,
           w_exp_down, ln2_g, ln2_b):
    batch, seq, d = x.shape
    frame = FRAME_HEAD + seq
    assert d == D_MODEL and frame % CHUNK == 0
    t = batch * frame
    depth = w_in.shape[0]

    head = jnp.concatenate([jnp.zeros((PAD_FRONT, d), x.dtype), meta_tokens.astype(x.dtype)], axis=0)
    rowmask = jnp.asarray(np.tile(np.arange(frame) >= PAD_FRONT, batch).astype(np.float32).reshape(t, 1))
    h, hb = _embed_ln(x, head, ln_in_g, ln_in_b)

    wr_t = w_router.T
    wr_hi = wr_t.astype(BF16)
    wr_lo = (wr_t - wr_hi.astype(F32)).astype(BF16)

    def pad_lanes(v):
        return jnp.pad(v, (0, LANES - v.shape[0])).reshape(1, LANES)

    wg_all = w_exp_gate.reshape(depth * N_EXPERTS, d, D_EXPERT)
    wu_all = w_exp_up.reshape(depth * N_EXPERTS, d, D_EXPERT)
    wd_all = w_exp_down.reshape(depth * N_EXPERTS, D_EXPERT, d)

    for l in range(depth):
        w = w_in[l]
        w_gp = jnp.concatenate([w[:, COL_GATE:], w[:, COL_POOL:COL_GATE]], axis=1).astype(BF16)
        w_dt = jnp.pad(w[:, COL_DT:COL_POOL], ((0, 0), (0, LANES - SSD_HEADS))).astype(BF16)
        xact, plain, dt_raw = _in_proj(hb, w_in, l, w_gp, w_dt, conv_w[l], conv_b[l].reshape(1, SSD_XBC))

        yn = _ssd(
            xact, plain, dt_raw, pad_lanes(dt_bias[l]),
            pad_lanes(-jnp.exp(a_log[l])), jnp.repeat(d_skip[l], SSD_HEAD_DIM).reshape(1, SSD_INNER),
            ssd_norm_w[l].reshape(1, SSD_INNER), batch, frame)

        h1, route = _mix(
            yn, plain, h, w_ssd_out[l].astype(BF16), w_pool[l].astype(BF16), w_out[l].astype(BF16),
            b_gate[l].reshape(1, N_BRANCH * d), pool_scale[l].reshape(1, POOL_WIDTH),
            ln1_g[l].reshape(1, d), ln1_b[l].reshape(1, d), wr_hi, wr_lo, batch, frame)

        pos, wcols, tile_ea, tile_eb, n_used, zero_tile, zero_on, n_slots = _dispatch_plan(route, t, l)
        x_slots = _dispatch(h1.reshape(t // SUBLANES, SUBLANES, d), pos, zero_tile, zero_on, n_used, n_slots)
        y_slots = _experts(x_slots, tile_ea, tile_eb, n_used, wg_all, wu_all, wd_all)
        if l == depth - 1:
            return _combine_out(h1, wcols, y_slots, pos, ln2_g[l].reshape(1, d), ln2_b[l].reshape(1, d), batch, seq)
        h, hb = _combine(h1, wcols, rowmask, y_slots, pos, ln2_g[l].reshape(1, d), ln2_b[l].reshape(1, d))
```

```python
import functools

import jax
import jax.numpy as jnp
import numpy as np
from jax import lax
from jax.experimental import pallas as pl
from jax.experimental.pallas import tpu as pltpu

F32 = jnp.float32
BF16 = jnp.bfloat16

D_MODEL = 1024
CHUNK = 64
N_META = 16
PAD_FRONT = CHUNK - N_META
FRAME_HEAD = PAD_FRONT + N_META

SSD_INNER = 2048
SSD_HEAD_DIM = 64
SSD_HEADS = 32
SSD_GROUPS = 8
SSD_HPG = 4
SSD_STATE = 128
SSD_CONV = 4
SSD_GN = SSD_GROUPS * SSD_STATE
SSD_XBC = SSD_INNER + 2 * SSD_GN
SSD_GROUP_DIM = SSD_HPG * SSD_HEAD_DIM

POOL_WIDTH = 1024
POOL_WINDOWS = (2, 4, 8, 16)
POOL_GROUP_DIM = 256
POOL_HALO = 16

N_BRANCH = 2
COL_Z = 0
COL_XBC = COL_Z + SSD_INNER
COL_DT = COL_XBC + SSD_XBC
COL_POOL = COL_DT + SSD_HEADS
COL_GATE = COL_POOL + POOL_WIDTH
IN_COLS = COL_GATE + N_BRANCH * D_MODEL

N_EXPERTS = 16
N_EXPERT_GROUPS = 4
EXPERTS_PER_GROUP = 4
D_EXPERT = 512
PAIRS = ((0, 1), (0, 2), (0, 3), (1, 3), (1, 2), (3, 2))
N_CLASSES = N_EXPERT_GROUPS * len(PAIRS)

DEPTH = 2
DN_ALPHA = (2.0 * DEPTH) ** 0.25
LN_EPS = 1e-5
RMS_EPS = 1e-5

LANES = 128
SUBLANES = 8
VMEM_LIMIT = 56 * 1024 * 1024

PROJ_TN = 1024
PLAIN_COLS = SSD_INNER + N_BRANCH * D_MODEL + POOL_WIDTH
PLAIN_BLK_Z = 0
PLAIN_BLK_GATE = 1
PLAIN_BLK_POOL = (SSD_INNER + N_BRANCH * D_MODEL) // POOL_WIDTH

EXPERT_TM = 256


def _pick_tile(n, target, mult):
    best = None
    for t in range(mult, min(n, target) + 1, mult):
        if n % t == 0:
            best = t
    assert best is not None, (n, target, mult)
    return best


def _params(*sem):
    return pltpu.CompilerParams(dimension_semantics=sem, vmem_limit_bytes=VMEM_LIMIT)


def _sigmoid(x):
    return 0.5 + 0.5 * jnp.tanh(0.5 * x)


def _silu(x):
    hx = 0.5 * x
    return hx + hx * jnp.tanh(hx)


def _layer_norm(x, g, b):
    mu = jnp.mean(x, axis=-1, keepdims=True)
    xc = x - mu
    var = jnp.mean(xc * xc, axis=-1, keepdims=True)
    return xc * lax.rsqrt(var + LN_EPS) * g + b


def _frame_block_copy(hbm, buf, sem, step, slot, *, blk, nblk, to_hbm, wait):
    b = step // nblk
    j = step - b * nblk

    def run(vm, hb):
        cp = pltpu.make_async_copy(vm, hb, sem.at[slot]) if to_hbm else pltpu.make_async_copy(hb, vm, sem.at[slot])
        if wait:
            cp.wait()
        else:
            cp.start()

    if blk > FRAME_HEAD:
        @pl.when(j == 0)
        def _():
            n = blk - FRAME_HEAD
            run(buf.at[slot, pl.ds(FRAME_HEAD, n)], hbm.at[b, pl.ds(0, n)])

    @pl.when(j > 0)
    def _():
        start = pl.multiple_of(j * blk - FRAME_HEAD, CHUNK)
        run(buf.at[slot], hbm.at[b, pl.ds(start, blk)])


def _embed_kernel(x_hbm, head_ref, g_ref, b_ref, h_ref, hb_ref, xbuf, sem, *, blk, nblk, n):
    s = pl.program_id(0)
    slot = s % 2
    copy = functools.partial(_frame_block_copy, x_hbm, xbuf, sem, blk=blk, nblk=nblk, to_hbm=False)

    @pl.when(s == 0)
    def _():
        copy(0, 0, wait=False)

    @pl.when(s + 1 < n)
    def _():
        copy(s + 1, 1 - slot, wait=False)

    copy(s, slot, wait=True)
    first = s % nblk == 0

    @pl.when(first)
    def _():
        xbuf[slot, 0:FRAME_HEAD, :] = head_ref[...]

    y = _layer_norm(xbuf[slot], g_ref[...], b_ref[...])
    h_ref[...] = y
    row = lax.broadcasted_iota(jnp.int32, (blk, 1), 0)
    keep = jnp.logical_or(jnp.logical_not(first), row >= PAD_FRONT)
    hb_ref[...] = jnp.where(keep, y, 0.0).astype(BF16)


def _embed_ln(x, head, g, b):
    batch, seq, d = x.shape
    frame = FRAME_HEAD + seq
    blk = _pick_tile(frame, 832, CHUNK)
    nblk = frame // blk
    n = batch * nblk
    row = pl.BlockSpec((blk, d), lambda s: (s, 0))
    vec = pl.BlockSpec((1, d), lambda s: (0, 0))
    return pl.pallas_call(
        functools.partial(_embed_kernel, blk=blk, nblk=nblk, n=n),
        grid=(n,),
        in_specs=[pl.BlockSpec(memory_space=pl.ANY), pl.BlockSpec((FRAME_HEAD, d), lambda s: (0, 0)), vec, vec],
        out_specs=[row, row],
        out_shape=[jax.ShapeDtypeStruct((batch * frame, d), F32), jax.ShapeDtypeStruct((batch * frame, d), BF16)],
        scratch_shapes=[pltpu.VMEM((2, blk, d), F32), pltpu.SemaphoreType.DMA((2,))],
        compiler_params=_params("arbitrary"),
        name="embed_ln",
    )(x, head, g.reshape(1, d), b.reshape(1, d))


CONV_TILES = SSD_XBC // PROJ_TN


def _in_proj_kernel(x_ref, wx_ref, wp_ref, wdt_ref, cw_ref, cb_ref, xo_ref, po_ref, dt_ref, tail_ref, *, slab):
    i = pl.program_id(0)
    j = pl.program_id(1)
    tm = x_ref.shape[0]

    @pl.when(j == 0)
    def _():
        dt_ref[...] = jnp.dot(x_ref[...], wdt_ref[...], preferred_element_type=F32)

    @pl.when(j >= CONV_TILES)
    def _():
        po_ref[...] = jnp.dot(x_ref[...], wp_ref[...], preferred_element_type=F32).astype(BF16)

    @pl.when(j < CONV_TILES)
    def _():
        jt = jnp.minimum(j, CONV_TILES - 1)

        @pl.when(i == 0)
        def _():
            tail_ref[jt] = jnp.zeros((SUBLANES, PROJ_TN), F32)

        prev8 = tail_ref[jt]
        w = wx_ref[...]
        wp = wp_ref[...]
        sub = lax.broadcasted_iota(jnp.int32, (1, SUBLANES, PROJ_TN), 1)
        cw = [cw_ref[k:k + 1, :].reshape(1, 1, PROJ_TN) for k in range(SSD_CONV)]
        cb = cb_ref[...].reshape(1, 1, PROJ_TN)

        def shift_rows(v3, head8, s):
            r = pltpu.roll(v3, s, axis=1)
            above = jnp.concatenate([pltpu.roll(head8, s, axis=0)[None], r[:-1]], axis=0)
            return jnp.where(sub < s, above, r)

        for s in range(tm // slab):
            rows = slice(s * slab, (s + 1) * slab)
            acc = jnp.dot(x_ref[rows, :], w, preferred_element_type=F32)
            x0 = acc.reshape(slab // SUBLANES, SUBLANES, PROJ_TN)
            x2 = shift_rows(x0, prev8, 2)
            odd = cw[2] * x0 + cw[0] * x2
            odd_head = cw[2][0] * prev8 + cw[0][0] * pltpu.roll(prev8, 2, axis=0)
            conv = cb + cw[3] * x0 + cw[1] * x2 + shift_rows(odd, odd_head, 1)
            xo_ref[rows, :] = _silu(conv).reshape(slab, PROJ_TN).astype(BF16)
            prev8 = acc[slab - SUBLANES:]
            po_ref[rows, :] = jnp.dot(x_ref[rows, :], wp, preferred_element_type=F32).astype(BF16)
        tail_ref[jt] = prev8


def _in_proj(hb, w_xbc, w_plain, w_dt, conv_w, conv_b):
    t, d = hb.shape
    tm = _pick_tile(t, 1664, 16)
    slab = _pick_tile(tm, 256, 16)

    def conv_col(i, j):
        return (0, jnp.minimum(j, CONV_TILES - 1))

    return pl.pallas_call(
        functools.partial(_in_proj_kernel, slab=slab),
        grid=(t // tm, PLAIN_COLS // PROJ_TN),
        in_specs=[
            pl.BlockSpec((tm, d), lambda i, j: (i, 0)),
            pl.BlockSpec((d, PROJ_TN), conv_col),
            pl.BlockSpec((d, PROJ_TN), lambda i, j: (0, j)),
            pl.BlockSpec((d, LANES), lambda i, j: (0, 0)),
            pl.BlockSpec((SSD_CONV, PROJ_TN), conv_col),
            pl.BlockSpec((1, PROJ_TN), conv_col),
        ],
        out_specs=[
            pl.BlockSpec((tm, PROJ_TN), lambda i, j: (i, jnp.minimum(j, CONV_TILES - 1))),
            pl.BlockSpec((tm, PROJ_TN), lambda i, j: (i, j)),
            pl.BlockSpec((tm, LANES), lambda i, j: (i, 0)),
        ],
        out_shape=[jax.ShapeDtypeStruct((t, SSD_XBC), BF16), jax.ShapeDtypeStruct((t, PLAIN_COLS), BF16),
                   jax.ShapeDtypeStruct((t, LANES), F32)],
        scratch_shapes=[pltpu.VMEM((CONV_TILES, SUBLANES, PROJ_TN), F32)],
        compiler_params=_params("arbitrary", "arbitrary"),
        name="in_proj",
    )(hb, w_xbc, w_plain, w_dt, conv_w, conv_b)


def _pair_select(lane_lo, col, h0):
    return jnp.where(lane_lo, col[:, h0:h0 + 1], col[:, h0 + 1:h0 + 2])


def _ssd_kernel(xbc_ref, z_ref, dt_ref, dtb_ref, a_ref, dsk_ref, nw_ref, o_ref, prev_ref, y_ref, *, tb):
    j = pl.program_id(1)

    @pl.when(j == 0)
    def _():
        prev_ref[...] = jnp.zeros_like(prev_ref)

    row_i = lax.broadcasted_iota(jnp.int32, (CHUNK, LANES), 0)
    row_c = lax.broadcasted_iota(jnp.int32, (CHUNK, 1), 0)
    lane_i = lax.broadcasted_iota(jnp.int32, (CHUNK, LANES), 1)
    lane_lo = lane_i < SSD_HEAD_DIM
    causal2 = row_i >= jnp.where(lane_lo, lane_i, lane_i - SSD_HEAD_DIM)
    lane_lo_row = lax.broadcasted_iota(jnp.int32, (1, LANES), 1) < SSD_HEAD_DIM

    def chunk(c, carry):
        r0 = pl.multiple_of(c * CHUNK, CHUNK)
        rows = pl.ds(r0, CHUNK)
        valid = (j * tb + r0 + row_c) >= PAD_FRONT

        dtr = dt_ref[rows, :] + dtb_ref[...]
        dt = jnp.maximum(dtr, 0.0) + jnp.log1p(jnp.exp(-jnp.abs(dtr)))
        dt = jnp.where(valid, dt, 0.0)
        a_cs = dt * a_ref[...]
        for s in (1, 2, 4, 8, 16, 32):
            a_cs = a_cs + jnp.where(row_i >= s, pltpu.roll(a_cs, s, axis=0), 0.0)
        a_last = a_cs[CHUNK - 1:CHUNK, :]
        dtw = dt * jnp.exp(a_last - a_cs)
        cdec = jnp.exp(a_last)
        a_t = jnp.concatenate([a_cs, a_cs], axis=0).T
        dt_t = jnp.concatenate([dt, dt], axis=0).T

        def bmat(g):
            return xbc_ref[rows, SSD_INNER + g * SSD_STATE:SSD_INNER + (g + 1) * SSD_STATE]

        def cmat(g):
            o0 = SSD_INNER + SSD_GN + g * SSD_STATE
            return xbc_ref[rows, o0:o0 + SSD_STATE]

        cb2s, yoffs = [], []
        for g in range(SSD_GROUPS):
            bm, cm = bmat(g), cmat(g)
            cb2s.append(lax.dot_general(cm, jnp.concatenate([bm, bm], axis=0),
                                        (((1,), (1,)), ((), ())), preferred_element_type=F32))
            yoffs.append(jnp.dot(cm, prev_ref[g].astype(BF16), preferred_element_type=F32))

        for g in range(SSD_GROUPS):
            for q in range(SSD_HPG // 2):
                h0 = g * SSD_HPG + 2 * q
                ps = slice(g * SSD_GROUP_DIM + q * LANES, g * SSD_GROUP_DIM + (q + 1) * LANES)
                xh_b = xbc_ref[rows, ps]
                a_col = _pair_select(lane_lo, a_cs, h0)
                a_row = jnp.where(lane_lo_row, a_t[h0:h0 + 1, :], a_t[h0 + 1:h0 + 2, :])
                dt_row = jnp.where(lane_lo_row, dt_t[h0:h0 + 1, :], dt_t[h0 + 1:h0 + 2, :])
                decay = jnp.exp(jnp.where(causal2, a_col - a_row, -jnp.inf))
                m = (cb2s[g] * decay * dt_row).astype(BF16)
                zero = jnp.zeros_like(xh_b)
                rhs = jnp.concatenate([jnp.where(lane_lo, xh_b, zero), jnp.where(lane_lo, zero, xh_b)], axis=0)
                y_diag = jnp.dot(m, rhs, preferred_element_type=F32)
                y_off = yoffs[g][:, q * LANES:(q + 1) * LANES] * jnp.exp(a_col)
                y_ref[:, ps] = y_diag + y_off + dsk_ref[:, ps] * xh_b.astype(F32)

        for g in range(SSD_GROUPS):
            xs_parts, cdec_parts = [], []
            for q in range(SSD_HPG // 2):
                h0 = g * SSD_HPG + 2 * q
                ps = slice(g * SSD_GROUP_DIM + q * LANES, g * SSD_GROUP_DIM + (q + 1) * LANES)
                xs_parts.append((xbc_ref[rows, ps].astype(F32) * _pair_select(lane_lo, dtw, h0)).astype(BF16))
                cdec_parts.append(jnp.where(lane_lo_row, cdec[:, h0:h0 + 1], cdec[:, h0 + 1:h0 + 2]))
            xs_dec = jnp.concatenate(xs_parts, axis=1)
            st = lax.dot_general(bmat(g), xs_dec, (((0,), (0,)), ((), ())), preferred_element_type=F32)
            prev_ref[g] = prev_ref[g] * jnp.concatenate(cdec_parts, axis=1) + st

        for g in range(SSD_GROUPS):
            gs = slice(g * SSD_GROUP_DIM, (g + 1) * SSD_GROUP_DIM)
            yg = y_ref[:, gs] * _silu(z_ref[rows, gs].astype(F32))
            ms = jnp.mean(yg * yg, axis=-1, keepdims=True)
            o_ref[rows, gs] = (yg * lax.rsqrt(ms + RMS_EPS) * nw_ref[:, gs]).astype(BF16)
        return carry

    lax.fori_loop(0, tb // CHUNK, chunk, 0)


def _ssd(xact, plain, dt_raw, dt_bias, a_neg, d_skip_ch, norm_w, batch, frame):
    t = xact.shape[0]
    tb = _pick_tile(frame, 832, CHUNK)
    nblk = frame // tb

    def vec(n):
        return pl.BlockSpec((1, n), lambda b, j: (0, 0))

    return pl.pallas_call(
        functools.partial(_ssd_kernel, tb=tb),
        grid=(batch, nblk),
        in_specs=[
            pl.BlockSpec((tb, SSD_XBC), lambda b, j: (b * nblk + j, 0)),
            pl.BlockSpec((tb, SSD_INNER), lambda b, j: (b * nblk + j, PLAIN_BLK_Z)),
            pl.BlockSpec((tb, LANES), lambda b, j: (b * nblk + j, 0)),
            vec(LANES), vec(LANES), vec(SSD_INNER), vec(SSD_INNER),
        ],
        out_specs=pl.BlockSpec((tb, SSD_INNER), lambda b, j: (b * nblk + j, 0)),
        out_shape=jax.ShapeDtypeStruct((t, SSD_INNER), BF16),
        scratch_shapes=[
            pltpu.VMEM((SSD_GROUPS, SSD_STATE, SSD_GROUP_DIM), F32),
            pltpu.VMEM((CHUNK, SSD_INNER), F32),
        ],
        compiler_params=_params("parallel", "arbitrary"),
        name="ssd",
    )(xact, plain, dt_raw, dt_bias, a_neg, d_skip_ch, norm_w)


def _route(logits):
    mx = jnp.max(logits, axis=0, keepdims=True)
    ex = jnp.exp(logits - mx)
    sc = ex / jnp.sum(ex, axis=0, keepdims=True)
    rows = [sc[e:e + 1, :] for e in range(N_EXPERTS)]
    best = None
    for g in range(N_EXPERT_GROUPS):
        a, b, c, d = rows[4 * g:4 * g + 4]
        hi1, lo1 = jnp.maximum(a, b), jnp.minimum(a, b)
        hi2, lo2 = jnp.maximum(c, d), jnp.minimum(c, d)
        gsc = jnp.maximum(hi1, hi2) + jnp.maximum(jnp.minimum(hi1, hi2), jnp.maximum(lo1, lo2))
        if best is None:
            best, sel = gsc, jnp.zeros_like(gsc)
        else:
            better = gsc > best
            sel = jnp.where(better, float(g), sel)
            best = jnp.maximum(best, gsc)
    v = []
    for i in range(EXPERTS_PER_GROUP):
        vi = rows[i]
        for g in range(1, N_EXPERT_GROUPS):
            vi = jnp.where(sel == float(g), rows[4 * g + i], vi)
        v.append(vi)

    def first_max(u):
        m = jnp.maximum(jnp.maximum(u[0], u[1]), jnp.maximum(u[2], u[3]))
        idx = jnp.where(u[0] == m, 0.0, jnp.where(u[1] == m, 1.0, jnp.where(u[2] == m, 2.0, 3.0)))
        return m, idx

    m1, i1 = first_max(v)
    u = [jnp.where(i1 == float(i), -1.0, v[i]) for i in range(EXPERTS_PER_GROUP)]
    m2, i2 = first_max(u)
    den = m1 + m2
    w1, w2 = m1 / den, m2 / den
    lo = jnp.minimum(i1, i2)
    hi = jnp.maximum(i1, i2)
    w_lo = jnp.where(i1 < i2, w1, w2)
    w_hi = jnp.where(i1 < i2, w2, w1)
    pair = jnp.where(lo == 0.0, hi - 1.0, jnp.where(lo == 1.0, jnp.where(hi == 3.0, 3.0, 4.0), 5.0))
    swap = pair == 5.0
    w_a = jnp.where(swap, w_hi, w_lo)
    w_b = jnp.where(swap, w_lo, w_hi)
    cls = sel * float(len(PAIRS)) + pair
    zero = jnp.zeros_like(cls)
    return jnp.concatenate([cls, w_a, w_b, zero, zero, zero, zero, zero], axis=0)


def _mix_kernel(yn_ref, gate_ref, pool_ref, h_ref, wso_ref, wp_ref, wo_ref, bg_ref, ps_ref, g_ref, b_ref,
                wr_hi_ref, wr_lo_ref, h1_ref, route_ref, halo_ref, *, tb):
    j = pl.program_id(1)

    @pl.when(j == 0)
    def _():
        halo_ref[0:POOL_HALO, :] = jnp.zeros((POOL_HALO, POOL_WIDTH), F32)

    halo_ref[POOL_HALO:POOL_HALO + tb, :] = pool_ref[...].astype(F32)
    half = (tb // 2) // LANES * LANES
    subs = [(0, half), (half, tb - half)] if half >= 2 * LANES else [(0, tb)]
    nt = (((1,), (1,)), ((), ()))

    for r, n in subs:
        gates = _sigmoid(gate_ref[r:r + n, :].astype(F32) + bg_ref[...])

        tpos = (j * tb + r - PAD_FRONT + 1 + lax.broadcasted_iota(jnp.int32, (n, 1), 0)).astype(F32)
        pools = []
        for gi, win in enumerate(POOL_WINDOWS):
            cs = slice(gi * POOL_GROUP_DIM, (gi + 1) * POOL_GROUP_DIM)
            s = halo_ref[r:r + n + POOL_HALO, cs]
            step = 1
            while step < win:
                s = s[step:] + s[:-step]
                step *= 2
            wsum = s[s.shape[0] - n:]
            cnt = jnp.clip(tpos, 1.0, float(win))
            pooled = wsum / cnt - halo_ref[POOL_HALO + r:POOL_HALO + r + n, cs]
            pools.append(jnp.dot(pooled.astype(BF16), wp_ref[gi], preferred_element_type=F32))
        y_pool = jnp.concatenate(pools, axis=1) * ps_ref[...]

        y_ssd = jnp.dot(yn_ref[r:r + n, :], wso_ref[...], preferred_element_type=F32)
        mixed =(gates[:, :D_MODEL] * y_ssd + gates[:, D_MODEL:] * y_pool).astype(BF16)
        out = jnp.dot(mixed, wo_ref[...], preferred_element_type=F32)
        h1 = _layer_norm(DN_ALPHA * h_ref[r:r + n, :] + out, g_ref[...], b_ref[...])
        h1_ref[r:r + n, :] = h1

        h_hi = h1.astype(BF16)
        h_lo = (h1 - h_hi.astype(F32)).astype(BF16)
        logits = (lax.dot_general(wr_hi_ref[...], h_hi, nt, preferred_element_type=F32)
                  + lax.dot_general(wr_lo_ref[...], h_hi, nt, preferred_element_type=F32)
                  + lax.dot_general(wr_hi_ref[...], h_lo, nt, preferred_element_type=F32))
        route_ref[0, :, r:r + n] = _route(logits)

    halo_ref[0:POOL_HALO, :] = halo_ref[tb:tb + POOL_HALO, :]


def _mix(yn, plain, h, w_ssd_out, w_pool, w_out, b_gate, pool_scale, ln_g, ln_b, wr_hi, wr_lo, batch, frame):
    t = yn.shape[0]
    tb = _pick_tile(frame, 832, CHUNK)
    nblk = frame // tb
    d = D_MODEL

    def full(shape):
        return pl.BlockSpec(shape, lambda b, j: (0,) * len(shape))

    return pl.pallas_call(
        functools.partial(_mix_kernel, tb=tb),
        grid=(batch, nblk),
        in_specs=[
            pl.BlockSpec((tb, SSD_INNER), lambda b, j: (b * nblk + j, 0)),
            pl.BlockSpec((tb, N_BRANCH * d), lambda b, j: (b * nblk + j, PLAIN_BLK_GATE)),
            pl.BlockSpec((tb, POOL_WIDTH), lambda b, j: (b * nblk + j, PLAIN_BLK_POOL)),
            pl.BlockSpec((tb, d), lambda b, j: (b * nblk + j, 0)),
            full((SSD_INNER, d)), full((len(POOL_WINDOWS), POOL_GROUP_DIM, POOL_GROUP_DIM)), full((d, d)),
            full((1, N_BRANCH * d)), full((1, POOL_WIDTH)), full((1, d)), full((1, d)),
            full((N_EXPERTS, d)), full((N_EXPERTS, d)),
        ],
        out_specs=[
            pl.BlockSpec((tb, d), lambda b, j: (b * nblk + j, 0)),
            pl.BlockSpec((1, SUBLANES, tb), lambda b, j: (b * nblk + j, 0, 0)),
        ],
        out_shape=[jax.ShapeDtypeStruct((t, d), F32),
                   jax.ShapeDtypeStruct((batch * nblk, SUBLANES, tb), F32)],
        scratch_shapes=[pltpu.VMEM((POOL_HALO + tb, POOL_WIDTH), F32)],
        compiler_params=_params("parallel", "arbitrary"),
        name="mix",
    )(yn, plain, plain, h, w_ssd_out, w_pool, w_out, b_gate, pool_scale, ln_g, ln_b, wr_hi, wr_lo)


def _as_tiles(v):
    return v.reshape(v.shape[0], SUBLANES, LANES)


def _row_copies(hbm, idx_ref, base, buf, sem, n, *, to_hbm, wait):
    def body(g, carry):
        for u in range(SUBLANES):
            r = g * SUBLANES + u
            row = 0 if wait else idx_ref[base + r]
            vm = buf.at[r]
            hb = hbm.at[row]
            cp = pltpu.make_async_copy(vm, hb, sem) if to_hbm else pltpu.make_async_copy(hb, vm, sem)
            if wait:
                cp.wait()
            else:
                cp.start()
        return carry
    lax.fori_loop(0, n // SUBLANES, body, 0)


def _gather_rows(src_hbm, idx_ref, base, buf, sem, n):
    _row_copies(src_hbm, idx_ref, base, buf, sem, n, to_hbm=False, wait=False)


def _wait_rows(src_hbm, buf, sem, n):
    _row_copies(src_hbm, None, 0, buf, sem, n, to_hbm=False, wait=True)


def _scatter_rows(buf, idx_ref, base, dst_hbm, sem, n):
    _row_copies(dst_hbm, idx_ref, base, buf, sem, n, to_hbm=True, wait=False)


def _wait_scattered(buf, dst_hbm, sem, n):
    _row_copies(dst_hbm, None, 0, buf, sem, n, to_hbm=True, wait=True)


def _dispatch_kernel(pos_ref, zt_ref, zon_ref, nu_ref, h_ref, x_hbm, stage, zbuf, sem, zsem, *,
                     tm, n, n_tiles):
    i = pl.program_id(0)
    slot = i % 2

    @pl.when(i == 0)
    def _():
        zbuf[...] = jnp.zeros_like(zbuf)

        def zero_tile(tile):
            row0 = pl.multiple_of(tile * EXPERT_TM, EXPERT_TM)
            cp = pltpu.make_async_copy(zbuf, x_hbm.at[pl.ds(row0, EXPERT_TM)], zsem)
            cp.start()
            cp.wait()

        for c in range(N_CLASSES):
            pl.when(zon_ref[c] > 0)(functools.partial(zero_tile, zt_ref[c]))
            pl.when(nu_ref[0] + c < n_tiles)(functools.partial(zero_tile, nu_ref[0] + c))

    @pl.when(i >= 2)
    def _():
        _wait_scattered(stage.at[slot], x_hbm, sem.at[slot], tm)

    stage[slot] = _as_tiles(h_ref[...])
    _scatter_rows(stage.at[slot], pos_ref, i * tm, x_hbm, sem.at[slot], tm)

    @pl.when(i == n - 1)
    def _():
        if n >= 2:
            _wait_scattered(stage.at[1 - slot], x_hbm, sem.at[1 - slot], tm)
        _wait_scattered(stage.at[slot], x_hbm, sem.at[slot], tm)


def _dispatch(h1, pos, zero_tile, zero_on, n_used, n_slots):
    t, d = h1.shape
    assert d == SUBLANES * LANES
    tm = _pick_tile(t, 512, 16)
    grid_spec = pltpu.PrefetchScalarGridSpec(
        num_scalar_prefetch=4,
        grid=(t // tm,),
        in_specs=[pl.BlockSpec((tm, d), lambda i, *_: (i, 0))],
        out_specs=pl.BlockSpec(memory_space=pl.ANY),
        scratch_shapes=[pltpu.VMEM((2, tm, SUBLANES, LANES), F32), pltpu.VMEM((EXPERT_TM, SUBLANES, LANES), F32),
                        pltpu.SemaphoreType.DMA((2,)), pltpu.SemaphoreType.DMA(())],
    )
    return pl.pallas_call(
        functools.partial(_dispatch_kernel, tm=tm, n=t // tm, n_tiles=n_slots // EXPERT_TM),
        grid_spec=grid_spec,
        out_shape=jax.ShapeDtypeStruct((n_slots, SUBLANES, LANES), F32),
        compiler_params=_params("arbitrary"),
        name="dispatch",
    )(pos, zero_tile, zero_on, n_used, h1)


def _pack_pair(a, b):
    a32 = lax.bitcast_convert_type(a.astype(BF16).astype(F32), jnp.uint32)
    b32 = lax.bitcast_convert_type(b.astype(BF16).astype(F32), jnp.uint32)
    return a32 | (b32 >> 16)


def _unpack_pair(w):
    a = lax.bitcast_convert_type(w & jnp.uint32(0xFFFF0000), F32)
    b = lax.bitcast_convert_type(w << 16, F32)
    return a, b


def _expert_kernel(ea_ref, eb_ref, nu_ref, newa_ref, newb_ref, x_ref, wga_ref, wua_ref, wda_ref,
                   wgb_ref, wub_ref, wdb_ref, o_ref, wg_s, wu_s, wd_s):
    i = pl.program_id(0)

    @pl.when(i < nu_ref[0])
    def _():
        @pl.when(newa_ref[i] > 0)
        def _():
            wg_s[0] = wga_ref[0].astype(BF16)
            wu_s[0] = wua_ref[0].astype(BF16)
            wd_s[0] = wda_ref[0].astype(BF16)

        @pl.when(newb_ref[i] > 0)
        def _():
            wg_s[1] = wgb_ref[0].astype(BF16)
            wu_s[1] = wub_ref[0].astype(BF16)
            wd_s[1] = wdb_ref[0].astype(BF16)

        x = x_ref[...].reshape(EXPERT_TM, D_MODEL).astype(BF16)

        def mlp(e):
            hid = _silu(jnp.dot(x, wg_s[e], preferred_element_type=F32)) * jnp.dot(x, wu_s[e], preferred_element_type=F32)
            return jnp.dot(hid.astype(BF16), wd_s[e], preferred_element_type=F32)

        o_ref[...] = _as_tiles(_pack_pair(mlp(0), mlp(1)))

    @pl.when(i >= nu_ref[0])
    def _():
        o_ref[...] = jnp.zeros_like(o_ref)


def _experts(x_slots, tile_ea, tile_eb, n_used, w_gate, w_up, w_down):
    n_slots = x_slots.shape[0]
    d = D_MODEL
    n_tiles = n_slots // EXPERT_TM
    tile = pl.BlockSpec((EXPERT_TM, SUBLANES, LANES), lambda i, *_: (i, 0, 0))
    first = jnp.ones((1,), jnp.int32)
    new_a = jnp.concatenate([first, (tile_ea[1:] != tile_ea[:-1]).astype(jnp.int32)])
    new_b = jnp.concatenate([first, (tile_eb[1:] != tile_eb[:-1]).astype(jnp.int32)])

    def wspec(shape, which):
        if which == 0:
            return pl.BlockSpec(shape, lambda i, ea, eb, *_: (ea[i], 0, 0))
        return pl.BlockSpec(shape, lambda i, ea, eb, *_: (eb[i], 0, 0))

    up = (1, d, D_EXPERT)
    down = (1, D_EXPERT, d)
    grid_spec = pltpu.PrefetchScalarGridSpec(
        num_scalar_prefetch=5,
        grid=(n_tiles,),
        in_specs=[
            tile,
            wspec(up, 0), wspec(up, 0), wspec(down, 0),
            wspec(up, 1), wspec(up, 1), wspec(down, 1),
        ],
        out_specs=tile,
        scratch_shapes=[pltpu.VMEM((2, d, D_EXPERT), BF16), pltpu.VMEM((2, d, D_EXPERT), BF16),
                        pltpu.VMEM((2, D_EXPERT, d), BF16)],
    )
    return pl.pallas_call(
        _expert_kernel,
        grid_spec=grid_spec,
        out_shape=jax.ShapeDtypeStruct((n_slots, SUBLANES, LANES), jnp.uint32),
        compiler_params=_params("arbitrary"),
        name="experts",
    )(tile_ea, tile_eb, n_used, new_a, new_b, x_slots, w_gate, w_up, w_down, w_gate, w_up, w_down)


def _moe_sum(y_tiles, w_ref):
    ya, yb = _unpack_pair(y_tiles.reshape(y_tiles.shape[0], D_MODEL))
    w = w_ref[...]
    return w[:, 0:1] * ya + w[:, 1:2] * yb


def _combine_kernel(pos_ref, h_ref, w_ref, m_ref, y_hbm, g_ref, b_ref, h2_ref, hb_ref, ybuf, sem, *, tm):
    i = pl.program_id(0)
    n = pl.num_programs(0)
    slot = i % 2

    @pl.when(i == 0)
    def _():
        _gather_rows(y_hbm, pos_ref, 0, ybuf.at[0], sem.at[0], tm)

    @pl.when(i + 1 < n)
    def _():
        _gather_rows(y_hbm, pos_ref, (i + 1) * tm, ybuf.at[1 - slot], sem.at[1 - slot], tm)

    _wait_rows(y_hbm, ybuf.at[slot], sem.at[slot], tm)
    h2 = _layer_norm(DN_ALPHA * h_ref[...] + _moe_sum(ybuf[slot], w_ref), g_ref[...], b_ref[...])
    h2_ref[...] = h2
    hb_ref[...] = (h2 * m_ref[...]).astype(BF16)


def _combine_out_kernel(pos_ref, h_ref, w_ref, y_hbm, g_ref, b_ref, out_hbm, ybuf, obuf, sem, osem, *, tm, nblk, n):
    s = pl.program_id(0)
    slot = s % 2
    put = functools.partial(_frame_block_copy, out_hbm, obuf, osem, blk=tm, nblk=nblk, to_hbm=True)

    @pl.when(s == 0)
    def _():
        _gather_rows(y_hbm, pos_ref, 0, ybuf.at[0], sem.at[0], tm)

    @pl.when(s + 1 < n)
    def _():
        _gather_rows(y_hbm, pos_ref, (s + 1) * tm, ybuf.at[1 - slot], sem.at[1 - slot], tm)

    @pl.when(s >= 2)
    def _():
        put(s - 2, slot, wait=True)

    _wait_rows(y_hbm, ybuf.at[slot], sem.at[slot], tm)
    obuf[slot] = _layer_norm(DN_ALPHA * h_ref[...] + _moe_sum(ybuf[slot], w_ref), g_ref[...], b_ref[...])
    put(s, slot, wait=False)

    @pl.when(s == n - 1)
    def _():
        if n >= 2:
            put(s - 1, 1 - slot, wait=True)
        put(s, slot, wait=True)


def _combine_out(h1, wcols, y_slots, pos, ln_g, ln_b, batch, seq):
    t, d = h1.shape
    frame = t // batch
    tm = _pick_tile(frame, 512, CHUNK)
    nblk = frame // tm
    n = t // tm
    row = pl.BlockSpec((tm, d), lambda i, pos: (i, 0))
    vec = pl.BlockSpec((1, d), lambda i, pos: (0, 0))
    grid_spec = pltpu.PrefetchScalarGridSpec(
        num_scalar_prefetch=1,
        grid=(n,),
        in_specs=[row, pl.BlockSpec((tm, LANES), lambda i, pos: (i, 0)), pl.BlockSpec(memory_space=pl.ANY), vec, vec],
        out_specs=pl.BlockSpec(memory_space=pl.ANY),
        scratch_shapes=[pltpu.VMEM((2, tm, SUBLANES, LANES), jnp.uint32), pltpu.VMEM((2, tm, d), F32),
                        pltpu.SemaphoreType.DMA((2,)), pltpu.SemaphoreType.DMA((2,))],
    )
    return pl.pallas_call(
        functools.partial(_combine_out_kernel, tm=tm, nblk=nblk, n=n),
        grid_spec=grid_spec,
        out_shape=jax.ShapeDtypeStruct((batch, seq, d), F32),
        compiler_params=_params("arbitrary"),
        name="combine_out",
    )(pos, h1, wcols, y_slots, ln_g, ln_b)


def _combine(h1, wcols, rowmask, y_slots, pos, ln_g, ln_b):
    t, d = h1.shape
    tm = _pick_tile(t, 512, 16)
    row = pl.BlockSpec((tm, d), lambda i, pos: (i, 0))
    vec = pl.BlockSpec((1, d), lambda i, pos: (0, 0))
    grid_spec = pltpu.PrefetchScalarGridSpec(
        num_scalar_prefetch=1,
        grid=(t // tm,),
        in_specs=[row, pl.BlockSpec((tm, LANES), lambda i, pos: (i, 0)), pl.BlockSpec((tm, 1), lambda i, pos: (i, 0)),
                  pl.BlockSpec(memory_space=pl.ANY), vec, vec],
        out_specs=[row, row],
        scratch_shapes=[pltpu.VMEM((2, tm, SUBLANES, LANES), jnp.uint32), pltpu.SemaphoreType.DMA((2,))],
    )
    return pl.pallas_call(
        functools.partial(_combine_kernel, tm=tm),
        grid_spec=grid_spec,
        out_shape=[jax.ShapeDtypeStruct((t, d), F32), jax.ShapeDtypeStruct((t, d), BF16)],
        compiler_params=_params("arbitrary"),
        name="combine",
    )(pos, h1, wcols, rowmask, y_slots, ln_g, ln_b)


_PAIR_A = np.array([p[0] for p in PAIRS], np.int32)
_PAIR_B = np.array([p[1] for p in PAIRS], np.int32)


def _dispatch_plan(route, t, layer):
    route = jnp.transpose(route, (1, 0, 2)).reshape(SUBLANES, t)
    cls = route[0].astype(jnp.int32)
    onehot = (cls[:, None] == jnp.arange(N_CLASSES, dtype=jnp.int32)[None, :]).astype(jnp.int32)
    csum = jnp.cumsum(onehot, axis=0)
    counts = csum[-1]
    rank = jnp.sum(onehot * csum, axis=1) - 1
    tiles_per = (counts + EXPERT_TM - 1) // EXPERT_TM
    tile_end = jnp.cumsum(tiles_per)
    cls_start = (tile_end - tiles_per) * EXPERT_TM
    pos = cls_start[cls] + rank
    n_tiles = -(-t // EXPERT_TM) + N_CLASSES
    n_used = tile_end[-1]
    tile_ids = jnp.minimum(jnp.arange(n_tiles, dtype=jnp.int32), n_used - 1)
    tile_cls = jnp.sum((tile_ids[:, None] >= tile_end[None, :]).astype(jnp.int32), axis=1)
    grp = tile_cls // len(PAIRS)
    pair = tile_cls % len(PAIRS)
    tile_ea = layer * N_EXPERTS + grp * EXPERTS_PER_GROUP + jnp.asarray(_PAIR_A)[pair]
    tile_eb = layer * N_EXPERTS + grp * EXPERTS_PER_GROUP + jnp.asarray(_PAIR_B)[pair]
    wcols = jnp.pad(route[1:3].T, ((0, 0), (0, LANES - 2)))
    zero_tile = jnp.maximum(tile_end - 1, 0).astype(jnp.int32)
    zero_on = (tiles_per > 0).astype(jnp.int32)
    return (pos.astype(jnp.int32), wcols, tile_ea, tile_eb, n_used.reshape(1).astype(jnp.int32),
            zero_tile, zero_on, n_tiles * EXPERT_TM)


def kernel(x, meta_tokens, ln_in_g, ln_in_b, w_router, w_in, conv_w, conv_b, dt_bias, a_log, d_skip,
           ssd_norm_w, w_ssd_out, w_pool, pool_scale, b_gate, w_out, ln1_g, ln1_b, w_exp_gate, w_exp_up,
           w_exp_down, ln2_g, ln2_b):
    batch, seq, d = x.shape
    frame = FRAME_HEAD + seq
    assert d == D_MODEL and frame % CHUNK == 0
    t = batch * frame
    depth = w_in.shape[0]

    head = jnp.concatenate([jnp.zeros((PAD_FRONT, d), x.dtype), meta_tokens.astype(x.dtype)], axis=0)
    rowmask = jnp.asarray(np.tile(np.arange(frame) >= PAD_FRONT, batch).astype(np.float32).reshape(t, 1))
    h, hb = _embed_ln(x, head, ln_in_g, ln_in_b)

    wr_t = w_router.T
    wr_hi = wr_t.astype(BF16)
    wr_lo = (wr_t - wr_hi.astype(F32)).astype(BF16)

    def pad_lanes(v):
        return jnp.pad(v, (0, LANES - v.shape[0])).reshape(1, LANES)

    wg_all = w_exp_gate.reshape(depth * N_EXPERTS, d, D_EXPERT)
    wu_all = w_exp_up.reshape(depth * N_EXPERTS, d, D_EXPERT)
    wd_all = w_exp_down.reshape(depth * N_EXPERTS, D_EXPERT, d)

    for l in range(depth):
        w = w_in[l]
        w_xbc = w[:, COL_XBC:COL_DT].astype(BF16)
        w_plain = jnp.concatenate([w[:, COL_Z:COL_XBC], w[:, COL_GATE:], w[:, COL_POOL:COL_GATE]], axis=1).astype(BF16)
        w_dt = jnp.pad(w[:, COL_DT:COL_POOL], ((0, 0), (0, LANES - SSD_HEADS))).astype(BF16)
        xact, plain, dt_raw = _in_proj(hb, w_xbc, w_plain, w_dt, conv_w[l], conv_b[l].reshape(1, SSD_XBC))

        yn = _ssd(
            xact, plain, dt_raw, pad_lanes(dt_bias[l]),
            pad_lanes(-jnp.exp(a_log[l])), jnp.repeat(d_skip[l], SSD_HEAD_DIM).reshape(1, SSD_INNER),
            ssd_norm_w[l].reshape(1, SSD_INNER), batch, frame)

        h1, route = _mix(
            yn, plain, h, w_ssd_out[l].astype(BF16), w_pool[l].astype(BF16), w_out[l].astype(BF16),
            b_gate[l].reshape(1, N_BRANCH * d), pool_scale[l].reshape(1, POOL_WIDTH),
            ln1_g[l].reshape(1, d), ln1_b[l].reshape(1, d), wr_hi, wr_lo, batch, frame)

        pos, wcols, tile_ea, tile_eb, n_used, zero_tile, zero_on, n_slots = _dispatch_plan(route, t, l)
        x_slots = _dispatch(h1, pos, zero_tile, zero_on, n_used, n_slots)
        y_slots = _experts(x_slots, tile_ea, tile_eb, n_used, wg_all, wu_all, wd_all)
        if l == depth - 1:
            return _combine_out(h1, wcols, y_slots, pos, ln2_g[l].reshape(1, d), ln2_b[l].reshape(1, d), batch, seq)
        h, hb = _combine(h1, wcols, rowmask, y_slots, pos, ln2_g[l].reshape(1, d), ln2_b[l].reshape(1, d))
```

```python
import functools

import jax
import jax.numpy as jnp
import numpy as np
from jax import lax
from jax.experimental import pallas as pl
from jax.experimental.pallas import tpu as pltpu

F32 = jnp.float32
BF16 = jnp.bfloat16

D_MODEL = 1024
CHUNK = 64
N_META = 16
PAD_FRONT = CHUNK - N_META
FRAME_HEAD = PAD_FRONT + N_META

SSD_INNER = 2048
SSD_HEAD_DIM = 64
SSD_HEADS = 32
SSD_GROUPS = 8
SSD_HPG = 4
SSD_STATE = 128
SSD_CONV = 4
SSD_GN = SSD_GROUPS * SSD_STATE
SSD_XBC = SSD_INNER + 2 * SSD_GN
SSD_GROUP_DIM = SSD_HPG * SSD_HEAD_DIM

POOL_WIDTH = 1024
POOL_WINDOWS = (2, 4, 8, 16)
POOL_GROUP_DIM = 256
POOL_HALO = 16

N_BRANCH = 2
COL_Z = 0
COL_XBC = COL_Z + SSD_INNER
COL_DT = COL_XBC + SSD_XBC
COL_POOL = COL_DT + SSD_HEADS
COL_GATE = COL_POOL + POOL_WIDTH
IN_COLS = COL_GATE + N_BRANCH * D_MODEL

N_EXPERTS = 16
N_EXPERT_GROUPS = 4
EXPERTS_PER_GROUP = 4
D_EXPERT = 512
PAIRS = ((0, 1), (0, 2), (0, 3), (1, 3), (1, 2), (3, 2))
N_CLASSES = N_EXPERT_GROUPS * len(PAIRS)

DEPTH = 2
DN_ALPHA = (2.0 * DEPTH) ** 0.25
LN_EPS = 1e-5
RMS_EPS = 1e-5

LANES = 128
SUBLANES = 8
VMEM_LIMIT = 56 * 1024 * 1024

PROJ_TN = 1024
PLAIN_COLS = SSD_INNER + N_BRANCH * D_MODEL + POOL_WIDTH
PLAIN_BLK_Z = 0
PLAIN_BLK_GATE = 1
PLAIN_BLK_POOL = (SSD_INNER + N_BRANCH * D_MODEL) // POOL_WIDTH

EXPERT_TM = 256


def _pick_tile(n, target, mult):
    best = None
    for t in range(mult, min(n, target) + 1, mult):
        if n % t == 0:
            best = t
    assert best is not None, (n, target, mult)
    return best


def _params(*sem):
    return pltpu.CompilerParams(dimension_semantics=sem, vmem_limit_bytes=VMEM_LIMIT)


def _sigmoid(x):
    return 0.5 + 0.5 * jnp.tanh(0.5 * x)


def _silu(x):
    hx = 0.5 * x
    return hx + hx * jnp.tanh(hx)


def _layer_norm(x, g, b):
    mu = jnp.mean(x, axis=-1, keepdims=True)
    xc = x - mu
    var = jnp.mean(xc * xc, axis=-1, keepdims=True)
    return xc * lax.rsqrt(var + LN_EPS) * g + b


def _frame_block_copy(hbm, buf, sem, step, slot, *, blk, nblk, to_hbm, wait):
    b = step // nblk
    j = step - b * nblk

    def run(vm, hb):
        cp = pltpu.make_async_copy(vm, hb, sem.at[slot]) if to_hbm else pltpu.make_async_copy(hb, vm, sem.at[slot])
        if wait:
            cp.wait()
        else:
            cp.start()

    if blk > FRAME_HEAD:
        @pl.when(j == 0)
        def _():
            n = blk - FRAME_HEAD
            run(buf.at[slot, pl.ds(FRAME_HEAD, n)], hbm.at[b, pl.ds(0, n)])

    @pl.when(j > 0)
    def _():
        start = pl.multiple_of(j * blk - FRAME_HEAD, CHUNK)
        run(buf.at[slot], hbm.at[b, pl.ds(start, blk)])


def _embed_kernel(x_hbm, head_ref, g_ref, b_ref, h_ref, hb_ref, xbuf, sem, *, blk, nblk, n):
    s = pl.program_id(0)
    slot = s % 2
    copy = functools.partial(_frame_block_copy, x_hbm, xbuf, sem, blk=blk, nblk=nblk, to_hbm=False)

    @pl.when(s == 0)
    def _():
        copy(0, 0, wait=False)

    @pl.when(s + 1 < n)
    def _():
        copy(s + 1, 1 - slot, wait=False)

    copy(s, slot, wait=True)
    first = s % nblk == 0

    @pl.when(first)
    def _():
        xbuf[slot, 0:FRAME_HEAD, :] = head_ref[...]

    y = _layer_norm(xbuf[slot], g_ref[...], b_ref[...])
    h_ref[...] = y
    row = lax.broadcasted_iota(jnp.int32, (blk, 1), 0)
    keep = jnp.logical_or(jnp.logical_not(first), row >= PAD_FRONT)
    hb_ref[...] = jnp.where(keep, y, 0.0).astype(BF16)


def _embed_ln(x, head, g, b):
    batch, seq, d = x.shape
    frame = FRAME_HEAD + seq
    blk = _pick_tile(frame, 832, CHUNK)
    nblk = frame // blk
    n = batch * nblk
    row = pl.BlockSpec((blk, d), lambda s: (s, 0))
    vec = pl.BlockSpec((1, d), lambda s: (0, 0))
    return pl.pallas_call(
        functools.partial(_embed_kernel, blk=blk, nblk=nblk, n=n),
        grid=(n,),
        in_specs=[pl.BlockSpec(memory_space=pl.ANY), pl.BlockSpec((FRAME_HEAD, d), lambda s: (0, 0)), vec, vec],
        out_specs=[row, row],
        out_shape=[jax.ShapeDtypeStruct((batch * frame, d), F32), jax.ShapeDtypeStruct((batch * frame, d), BF16)],
        scratch_shapes=[pltpu.VMEM((2, blk, d), F32), pltpu.SemaphoreType.DMA((2,))],
        compiler_params=_params("arbitrary"),
        name="embed_ln",
    )(x, head, g.reshape(1, d), b.reshape(1, d))


CONV_TILES = SSD_XBC // PROJ_TN


def _in_proj_kernel(x_ref, wx_ref, wp_ref, wdt_ref, cw_ref, cb_ref, xo_ref, po_ref, dt_ref, tail_ref, *, slab):
    i = pl.program_id(0)
    j = pl.program_id(1)
    tm = x_ref.shape[0]

    @pl.when(j == 0)
    def _():
        dt_ref[...] = jnp.dot(x_ref[...], wdt_ref[...], preferred_element_type=F32)

    @pl.when(j >= CONV_TILES)
    def _():
        po_ref[...] = jnp.dot(x_ref[...], wp_ref[...], preferred_element_type=F32).astype(BF16)

    @pl.when(j < CONV_TILES)
    def _():
        jt = jnp.minimum(j, CONV_TILES - 1)

        @pl.when(i == 0)
        def _():
            tail_ref[jt] = jnp.zeros((SUBLANES, PROJ_TN), F32)

        prev8 = tail_ref[jt]
        w = wx_ref[...]
        wp = wp_ref[...]
        sub = lax.broadcasted_iota(jnp.int32, (1, SUBLANES, PROJ_TN), 1)
        cw = [cw_ref[k:k + 1, :].reshape(1, 1, PROJ_TN) for k in range(SSD_CONV)]
        cb = cb_ref[...].reshape(1, 1, PROJ_TN)

        def shift_rows(v3, head8, s):
            r = pltpu.roll(v3, s, axis=1)
            above = jnp.concatenate([pltpu.roll(head8, s, axis=0)[None], r[:-1]], axis=0)
            return jnp.where(sub < s, above, r)

        for s in range(tm // slab):
            rows = slice(s * slab, (s + 1) * slab)
            acc = jnp.dot(x_ref[rows, :], w, preferred_element_type=F32)
            x0 = acc.reshape(slab // SUBLANES, SUBLANES, PROJ_TN)
            x2 = shift_rows(x0, prev8, 2)
            odd = cw[2] * x0 + cw[0] * x2
            odd_head = cw[2][0] * prev8 + cw[0][0] * pltpu.roll(prev8, 2, axis=0)
            conv = cb + cw[3] * x0 + cw[1] * x2 + shift_rows(odd, odd_head, 1)
            xo_ref[rows, :] = _silu(conv).reshape(slab, PROJ_TN).astype(BF16)
            prev8 = acc[slab - SUBLANES:]
            po_ref[rows, :] = jnp.dot(x_ref[rows, :], wp, preferred_element_type=F32).astype(BF16)
        tail_ref[jt] = prev8


def _in_proj(hb, w_xbc, w_plain, w_dt, conv_w, conv_b):
    t, d = hb.shape
    tm = _pick_tile(t, 1664, 16)
    slab = _pick_tile(tm, 256, 16)

    def conv_col(i, j):
        return (0, jnp.minimum(j, CONV_TILES - 1))

    return pl.pallas_call(
        functools.partial(_in_proj_kernel, slab=slab),
        grid=(t // tm, PLAIN_COLS // PROJ_TN),
        in_specs=[
            pl.BlockSpec((tm, d), lambda i, j: (i, 0)),
            pl.BlockSpec((d, PROJ_TN), conv_col),
            pl.BlockSpec((d, PROJ_TN), lambda i, j: (0, j)),
            pl.BlockSpec((d, LANES), lambda i, j: (0, 0)),
            pl.BlockSpec((SSD_CONV, PROJ_TN), conv_col),
            pl.BlockSpec((1, PROJ_TN), conv_col),
        ],
        out_specs=[
            pl.BlockSpec((tm, PROJ_TN), lambda i, j: (i, jnp.minimum(j, CONV_TILES - 1))),
            pl.BlockSpec((tm, PROJ_TN), lambda i, j: (i, j)),
            pl.BlockSpec((tm, LANES), lambda i, j: (i, 0)),
        ],
        out_shape=[jax.ShapeDtypeStruct((t, SSD_XBC), BF16), jax.ShapeDtypeStruct((t, PLAIN_COLS), BF16),
                   jax.ShapeDtypeStruct((t, LANES), F32)],
        scratch_shapes=[pltpu.VMEM((CONV_TILES, SUBLANES, PROJ_TN), F32)],
        compiler_params=_params("arbitrary", "arbitrary"),
        name="in_proj",
    )(hb, w_xbc, w_plain, w_dt, conv_w, conv_b)


def _pair_select(lane_lo, col, h0):
    return jnp.where(lane_lo, col[:, h0:h0 + 1], col[:, h0 + 1:h0 + 2])


def _ssd_kernel(xbc_ref, z_ref, dt_ref, dtb_ref, a_ref, dsk_ref, nw_ref, o_ref, prev_ref, y_ref, *, tb):
    j = pl.program_id(1)

    @pl.when(j == 0)
    def _():
        prev_ref[...] = jnp.zeros_like(prev_ref)

    row_i = lax.broadcasted_iota(jnp.int32, (CHUNK, LANES), 0)
    row_c = lax.broadcasted_iota(jnp.int32, (CHUNK, 1), 0)
    lane_i = lax.broadcasted_iota(jnp.int32, (CHUNK, LANES), 1)
    lane_lo = lane_i < SSD_HEAD_DIM
    causal2 = row_i >= jnp.where(lane_lo, lane_i, lane_i - SSD_HEAD_DIM)
    lane_lo_row = lax.broadcasted_iota(jnp.int32, (1, LANES), 1) < SSD_HEAD_DIM

    def chunk(c, carry):
        r0 = pl.multiple_of(c * CHUNK, CHUNK)
        rows = pl.ds(r0, CHUNK)
        valid = (j * tb + r0 + row_c) >= PAD_FRONT

        dtr = dt_ref[rows, :] + dtb_ref[...]
        dt = jnp.maximum(dtr, 0.0) + jnp.log1p(jnp.exp(-jnp.abs(dtr)))
        dt = jnp.where(valid, dt, 0.0)
        a_cs = dt * a_ref[...]
        for s in (1, 2, 4, 8, 16, 32):
            a_cs = a_cs + jnp.where(row_i >= s, pltpu.roll(a_cs, s, axis=0), 0.0)
        a_last = a_cs[CHUNK - 1:CHUNK, :]
        dtw = dt * jnp.exp(a_last - a_cs)
        cdec = jnp.exp(a_last)
        a_t = jnp.concatenate([a_cs, a_cs], axis=0).T
        dt_t = jnp.concatenate([dt, dt], axis=0).T

        def bmat(g):
            return xbc_ref[rows, SSD_INNER + g * SSD_STATE:SSD_INNER + (g + 1) * SSD_STATE]

        def cmat(g):
            o0 = SSD_INNER + SSD_GN + g * SSD_STATE
            return xbc_ref[rows, o0:o0 + SSD_STATE]

        cb2s, yoffs = [], []
        for g in range(SSD_GROUPS):
            bm, cm = bmat(g), cmat(g)
            cb2s.append(lax.dot_general(cm, jnp.concatenate([bm, bm], axis=0),
                                        (((1,), (1,)), ((), ())), preferred_element_type=F32))
            yoffs.append(jnp.dot(cm, prev_ref[g].astype(BF16), preferred_element_type=F32))

        for g in range(SSD_GROUPS):
            for q in range(SSD_HPG // 2):
                h0 = g * SSD_HPG + 2 * q
                ps = slice(g * SSD_GROUP_DIM + q * LANES, g * SSD_GROUP_DIM + (q + 1) * LANES)
                xh_b = xbc_ref[rows, ps]
                a_col = _pair_select(lane_lo, a_cs, h0)
                a_row = jnp.where(lane_lo_row, a_t[h0:h0 + 1, :], a_t[h0 + 1:h0 + 2, :])
                dt_row = jnp.where(lane_lo_row, dt_t[h0:h0 + 1, :], dt_t[h0 + 1:h0 + 2, :])
                decay = jnp.exp(jnp.where(causal2, a_col - a_row, -jnp.inf))
                m = (cb2s[g] * decay * dt_row).astype(BF16)
                zero = jnp.zeros_like(xh_b)
                rhs = jnp.concatenate([jnp.where(lane_lo, xh_b, zero), jnp.where(lane_lo, zero, xh_b)], axis=0)
                y_diag = jnp.dot(m, rhs, preferred_element_type=F32)
                y_off = yoffs[g][:, q * LANES:(q + 1) * LANES] * jnp.exp(a_col)
                y_ref[:, ps] = y_diag + y_off + dsk_ref[:, ps] * xh_b.astype(F32)

        for g in range(SSD_GROUPS):
            xs_parts, cdec_parts = [], []
            for q in range(SSD_HPG // 2):
                h0 = g * SSD_HPG + 2 * q
                ps = slice(g * SSD_GROUP_DIM + q * LANES, g * SSD_GROUP_DIM + (q + 1) * LANES)
                xs_parts.append((xbc_ref[rows, ps].astype(F32) * _pair_select(lane_lo, dtw, h0)).astype(BF16))
                cdec_parts.append(jnp.where(lane_lo_row, cdec[:, h0:h0 + 1], cdec[:, h0 + 1:h0 + 2]))
            xs_dec = jnp.concatenate(xs_parts, axis=1)
            st = lax.dot_general(bmat(g), xs_dec, (((0,), (0,)), ((), ())), preferred_element_type=F32)
            prev_ref[g] = prev_ref[g] * jnp.concatenate(cdec_parts, axis=1) + st

        for g in range(SSD_GROUPS):
            gs = slice(g * SSD_GROUP_DIM, (g + 1) * SSD_GROUP_DIM)
            yg = y_ref[:, gs] * _silu(z_ref[rows, gs].astype(F32))
            ms = jnp.mean(yg * yg, axis=-1, keepdims=True)
            o_ref[rows, gs] = (yg * lax.rsqrt(ms + RMS_EPS) * nw_ref[:, gs]).astype(BF16)
        return carry

    lax.fori_loop(0, tb // CHUNK, chunk, 0)


def _ssd(xact, plain, dt_raw, dt_bias, a_neg, d_skip_ch, norm_w, batch, frame):
    t = xact.shape[0]
    tb = _pick_tile(frame, 832, CHUNK)
    nblk = frame // tb

    def vec(n):
        return pl.BlockSpec((1, n), lambda b, j: (0, 0))

    return pl.pallas_call(
        functools.partial(_ssd_kernel, tb=tb),
        grid=(batch, nblk),
        in_specs=[
            pl.BlockSpec((tb, SSD_XBC), lambda b, j: (b * nblk + j, 0)),
            pl.BlockSpec((tb, SSD_INNER), lambda b, j: (b * nblk + j, PLAIN_BLK_Z)),
            pl.BlockSpec((tb, LANES), lambda b, j: (b * nblk + j, 0)),
            vec(LANES), vec(LANES), vec(SSD_INNER), vec(SSD_INNER),
        ],
        out_specs=pl.BlockSpec((tb, SSD_INNER), lambda b, j: (b * nblk + j, 0)),
        out_shape=jax.ShapeDtypeStruct((t, SSD_INNER), BF16),
        scratch_shapes=[
            pltpu.VMEM((SSD_GROUPS, SSD_STATE, SSD_GROUP_DIM), F32),
            pltpu.VMEM((CHUNK, SSD_INNER), F32),
        ],
        compiler_params=_params("parallel", "arbitrary"),
        name="ssd",
    )(xact, plain, dt_raw, dt_bias, a_neg, d_skip_ch, norm_w)


def _route(logits):
    mx = jnp.max(logits, axis=0, keepdims=True)
    ex = jnp.exp(logits - mx)
    sc = ex / jnp.sum(ex, axis=0, keepdims=True)
    rows = [sc[e:e + 1, :] for e in range(N_EXPERTS)]
    best = None
    for g in range(N_EXPERT_GROUPS):
        a, b, c, d = rows[4 * g:4 * g + 4]
        hi1, lo1 = jnp.maximum(a, b), jnp.minimum(a, b)
        hi2, lo2 = jnp.maximum(c, d), jnp.minimum(c, d)
        gsc = jnp.maximum(hi1, hi2) + jnp.maximum(jnp.minimum(hi1, hi2), jnp.maximum(lo1, lo2))
        if best is None:
            best, sel = gsc, jnp.zeros_like(gsc)
        else:
            better = gsc > best
            sel = jnp.where(better, float(g), sel)
            best = jnp.maximum(best, gsc)
    v = []
    for i in range(EXPERTS_PER_GROUP):
        vi = rows[i]
        for g in range(1, N_EXPERT_GROUPS):
            vi = jnp.where(sel == float(g), rows[4 * g + i], vi)
        v.append(vi)

    def first_max(u):
        m = jnp.maximum(jnp.maximum(u[0], u[1]), jnp.maximum(u[2], u[3]))
        idx = jnp.where(u[0] == m, 0.0, jnp.where(u[1] == m, 1.0, jnp.where(u[2] == m, 2.0, 3.0)))
        return m, idx

    m1, i1 = first_max(v)
    u = [jnp.where(i1 == float(i), -1.0, v[i]) for i in range(EXPERTS_PER_GROUP)]
    m2, i2 = first_max(u)
    den = m1 + m2
    w1, w2 = m1 / den, m2 / den
    lo = jnp.minimum(i1, i2)
    hi = jnp.maximum(i1, i2)
    w_lo = jnp.where(i1 < i2, w1, w2)
    w_hi = jnp.where(i1 < i2, w2, w1)
    pair = jnp.where(lo == 0.0, hi - 1.0, jnp.where(lo == 1.0, jnp.where(hi == 3.0, 3.0, 4.0), 5.0))
    swap = pair == 5.0
    w_a = jnp.where(swap, w_hi, w_lo)
    w_b = jnp.where(swap, w_lo, w_hi)
    cls = sel * float(len(PAIRS)) + pair
    zero = jnp.zeros_like(cls)
    return jnp.concatenate([cls, w_a, w_b, zero, zero, zero, zero, zero], axis=0)


def _mix_kernel(yn_ref, gate_ref, pool_ref, h_ref, wso_ref, wp_ref, wo_ref, bg_ref, ps_ref, g_ref, b_ref,
                wr_hi_ref, wr_lo_ref, h1_ref, route_ref, halo_ref, *, tb):
    j = pl.program_id(1)

    @pl.when(j == 0)
    def _():
        halo_ref[0:POOL_HALO, :] = jnp.zeros((POOL_HALO, POOL_WIDTH), F32)

    halo_ref[POOL_HALO:POOL_HALO + tb, :] = pool_ref[...].astype(F32)
    half = (tb // 2) // LANES * LANES
    subs = [(0, half), (half, tb - half)] if half >= 2 * LANES else [(0, tb)]
    nt = (((1,), (1,)), ((), ()))

    for r, n in subs:
        gates = _sigmoid(gate_ref[r:r + n, :].astype(F32) + bg_ref[...])

        tpos = (j * tb + r - PAD_FRONT + 1 + lax.broadcasted_iota(jnp.int32, (n, 1), 0)).astype(F32)
        pools = []
        for gi, win in enumerate(POOL_WINDOWS):
            cs = slice(gi * POOL_GROUP_DIM, (gi + 1) * POOL_GROUP_DIM)
            s = halo_ref[r:r + n + POOL_HALO, cs]
            step = 1
            while step < win:
                s = s[step:] + s[:-step]
                step *= 2
            wsum = s[s.shape[0] - n:]
            cnt = jnp.clip(tpos, 1.0, float(win))
            pooled = wsum / cnt - halo_ref[POOL_HALO + r:POOL_HALO + r + n, cs]
            pools.append(jnp.dot(pooled.astype(BF16), wp_ref[gi], preferred_element_type=F32))
        y_pool = jnp.concatenate(pools, axis=1) * ps_ref[...]

        y_ssd = jnp.dot(yn_ref[r:r + n, :], wso_ref[...], preferred_element_type=F32)
        mixed =(gates[:, :D_MODEL] * y_ssd + gates[:, D_MODEL:] * y_pool).astype(BF16)
        out = jnp.dot(mixed, wo_ref[...], preferred_element_type=F32)
        h1 = _layer_norm(DN_ALPHA * h_ref[r:r + n, :] + out, g_ref[...], b_ref[...])
        h1_ref[r:r + n, :] = h1

        h_hi = h1.astype(BF16)
        h_lo = (h1 - h_hi.astype(F32)).astype(BF16)
        logits = (lax.dot_general(wr_hi_ref[...], h_hi, nt, preferred_element_type=F32)
                  + lax.dot_general(wr_lo_ref[...], h_hi, nt, preferred_element_type=F32)
                  + lax.dot_general(wr_hi_ref[...], h_lo, nt, preferred_element_type=F32))
        route_ref[0, :, r:r + n] = _route(logits)

    halo_ref[0:POOL_HALO, :] = halo_ref[tb:tb + POOL_HALO, :]


def _mix(yn, plain, h, w_ssd_out, w_pool, w_out, b_gate, pool_scale, ln_g, ln_b, wr_hi, wr_lo, batch, frame):
    t = yn.shape[0]
    tb = _pick_tile(frame, 832, CHUNK)
    nblk = frame // tb
    d = D_MODEL

    def full(shape):
        return pl.BlockSpec(shape, lambda b, j: (0,) * len(shape))

    return pl.pallas_call(
        functools.partial(_mix_kernel, tb=tb),
        grid=(batch, nblk),
        in_specs=[
            pl.BlockSpec((tb, SSD_INNER), lambda b, j: (b * nblk + j, 0)),
            pl.BlockSpec((tb, N_BRANCH * d), lambda b, j: (b * nblk + j, PLAIN_BLK_GATE)),
            pl.BlockSpec((tb, POOL_WIDTH), lambda b, j: (b * nblk + j, PLAIN_BLK_POOL)),
            pl.BlockSpec((tb, d), lambda b, j: (b * nblk + j, 0)),
            full((SSD_INNER, d)), full((len(POOL_WINDOWS), POOL_GROUP_DIM, POOL_GROUP_DIM)), full((d, d)),
            full((1, N_BRANCH * d)), full((1, POOL_WIDTH)), full((1, d)), full((1, d)),
            full((N_EXPERTS, d)), full((N_EXPERTS, d)),
        ],
        out_specs=[
            pl.BlockSpec((tb, d), lambda b, j: (b * nblk + j, 0)),
            pl.BlockSpec((1, SUBLANES, tb), lambda b, j: (b * nblk + j, 0, 0)),
        ],
        out_shape=[jax.ShapeDtypeStruct((t, d), F32),
                   jax.ShapeDtypeStruct((batch * nblk, SUBLANES, tb), F32)],
        scratch_shapes=[pltpu.VMEM((POOL_HALO + tb, POOL_WIDTH), F32)],
        compiler_params=_params("parallel", "arbitrary"),
        name="mix",
    )(yn, plain, plain, h, w_ssd_out, w_pool, w_out, b_gate, pool_scale, ln_g, ln_b, wr_hi, wr_lo)


def _as_tiles(v):
    return v.reshape(v.shape[0], SUBLANES, LANES)


def _row_copies(hbm, idx_ref, base, buf, sem, n, *, to_hbm, wait):
    def body(g, carry):
        for u in range(SUBLANES):
            r = g * SUBLANES + u
            row = 0 if wait else idx_ref[base + r]
            vm = buf.at[r]
            hb = hbm.at[row]
            cp = pltpu.make_async_copy(vm, hb, sem) if to_hbm else pltpu.make_async_copy(hb, vm, sem)
            if wait:
                cp.wait()
            else:
                cp.start(priority=u % 2)
        return carry
    lax.fori_loop(0, n // SUBLANES, body, 0)


def _gather_rows(src_hbm, idx_ref, base, buf, sem, n):
    _row_copies(src_hbm, idx_ref, base, buf, sem, n, to_hbm=False, wait=False)


def _wait_rows(src_hbm, buf, sem, n):
    _row_copies(src_hbm, None, 0, buf, sem, n, to_hbm=False, wait=True)


def _scatter_rows(buf, idx_ref, base, dst_hbm, sem, n):
    _row_copies(dst_hbm, idx_ref, base, buf, sem, n, to_hbm=True, wait=False)


def _wait_scattered(buf, dst_hbm, sem, n):
    _row_copies(dst_hbm, None, 0, buf, sem, n, to_hbm=True, wait=True)


def _dispatch_kernel(pos_ref, zt_ref, zon_ref, nu_ref, h_ref, x_hbm, stage, zbuf, sem, zsem, *,
                     tm, n, n_tiles):
    i = pl.program_id(0)
    slot = i % 2

    @pl.when(i == 0)
    def _():
        zbuf[...] = jnp.zeros_like(zbuf)

        def zero_tile(tile):
            row0 = pl.multiple_of(tile * EXPERT_TM, EXPERT_TM)
            cp = pltpu.make_async_copy(zbuf, x_hbm.at[pl.ds(row0, EXPERT_TM)], zsem)
            cp.start()
            cp.wait()

        for c in range(N_CLASSES):
            pl.when(zon_ref[c] > 0)(functools.partial(zero_tile, zt_ref[c]))
            pl.when(nu_ref[0] + c < n_tiles)(functools.partial(zero_tile, nu_ref[0] + c))

    @pl.when(i >= 2)
    def _():
        _wait_scattered(stage.at[slot], x_hbm, sem.at[slot], tm)

    stage[slot] = _as_tiles(h_ref[...])
    _scatter_rows(stage.at[slot], pos_ref, i * tm, x_hbm, sem.at[slot], tm)

    @pl.when(i == n - 1)
    def _():
        if n >= 2:
            _wait_scattered(stage.at[1 - slot], x_hbm, sem.at[1 - slot], tm)
        _wait_scattered(stage.at[slot], x_hbm, sem.at[slot], tm)


def _dispatch(h1, pos, zero_tile, zero_on, n_used, n_slots):
    t, d = h1.shape
    assert d == SUBLANES * LANES
    tm = _pick_tile(t, 512, 16)
    grid_spec = pltpu.PrefetchScalarGridSpec(
        num_scalar_prefetch=4,
        grid=(t // tm,),
        in_specs=[pl.BlockSpec((tm, d), lambda i, *_: (i, 0))],
        out_specs=pl.BlockSpec(memory_space=pl.ANY),
        scratch_shapes=[pltpu.VMEM((2, tm, SUBLANES, LANES), F32), pltpu.VMEM((EXPERT_TM, SUBLANES, LANES), F32),
                        pltpu.SemaphoreType.DMA((2,)), pltpu.SemaphoreType.DMA(())],
    )
    return pl.pallas_call(
        functools.partial(_dispatch_kernel, tm=tm, n=t // tm, n_tiles=n_slots // EXPERT_TM),
        grid_spec=grid_spec,
        out_shape=jax.ShapeDtypeStruct((n_slots, SUBLANES, LANES), F32),
        compiler_params=_params("arbitrary"),
        name="dispatch",
    )(pos, zero_tile, zero_on, n_used, h1)


def _pack_pair(a, b):
    a32 = lax.bitcast_convert_type(a.astype(BF16).astype(F32), jnp.uint32)
    b32 = lax.bitcast_convert_type(b.astype(BF16).astype(F32), jnp.uint32)
    return a32 | (b32 >> 16)


def _unpack_pair(w):
    a = lax.bitcast_convert_type(w & jnp.uint32(0xFFFF0000), F32)
    b = lax.bitcast_convert_type(w << 16, F32)
    return a, b


def _expert_kernel(ea_ref, eb_ref, nu_ref, newa_ref, newb_ref, x_ref, wga_ref, wua_ref, wda_ref,
                   wgb_ref, wub_ref, wdb_ref, o_ref, wg_s, wu_s, wd_s):
    i = pl.program_id(0)

    @pl.when(i < nu_ref[0])
    def _():
        @pl.when(newa_ref[i] > 0)
        def _():
            wg_s[0] = wga_ref[0].astype(BF16)
            wu_s[0] = wua_ref[0].astype(BF16)
            wd_s[0] = wda_ref[0].astype(BF16)

        @pl.when(newb_ref[i] > 0)
        def _():
            wg_s[1] = wgb_ref[0].astype(BF16)
            wu_s[1] = wub_ref[0].astype(BF16)
            wd_s[1] = wdb_ref[0].astype(BF16)

        x = x_ref[...].reshape(EXPERT_TM, D_MODEL).astype(BF16)

        def mlp(e):
            hid = _silu(jnp.dot(x, wg_s[e], preferred_element_type=F32)) * jnp.dot(x, wu_s[e], preferred_element_type=F32)
            return jnp.dot(hid.astype(BF16), wd_s[e], preferred_element_type=F32)

        o_ref[...] = _as_tiles(_pack_pair(mlp(0), mlp(1)))

    @pl.when(i >= nu_ref[0])
    def _():
        o_ref[...] = jnp.zeros_like(o_ref)


def _experts(x_slots, tile_ea, tile_eb, n_used, w_gate, w_up, w_down):
    n_slots = x_slots.shape[0]
    d = D_MODEL
    n_tiles = n_slots // EXPERT_TM
    tile = pl.BlockSpec((EXPERT_TM, SUBLANES, LANES), lambda i, *_: (i, 0, 0))
    first = jnp.ones((1,), jnp.int32)
    new_a = jnp.concatenate([first, (tile_ea[1:] != tile_ea[:-1]).astype(jnp.int32)])
    new_b = jnp.concatenate([first, (tile_eb[1:] != tile_eb[:-1]).astype(jnp.int32)])

    def wspec(shape, which):
        if which == 0:
            return pl.BlockSpec(shape, lambda i, ea, eb, *_: (ea[i], 0, 0))
        return pl.BlockSpec(shape, lambda i, ea, eb, *_: (eb[i], 0, 0))

    up = (1, d, D_EXPERT)
    down = (1, D_EXPERT, d)
    grid_spec = pltpu.PrefetchScalarGridSpec(
        num_scalar_prefetch=5,
        grid=(n_tiles,),
        in_specs=[
            tile,
            wspec(up, 0), wspec(up, 0), wspec(down, 0),
            wspec(up, 1), wspec(up, 1), wspec(down, 1),
        ],
        out_specs=tile,
        scratch_shapes=[pltpu.VMEM((2, d, D_EXPERT), BF16), pltpu.VMEM((2, d, D_EXPERT), BF16),
                        pltpu.VMEM((2, D_EXPERT, d), BF16)],
    )
    return pl.pallas_call(
        _expert_kernel,
        grid_spec=grid_spec,
        out_shape=jax.ShapeDtypeStruct((n_slots, SUBLANES, LANES), jnp.uint32),
        compiler_params=_params("arbitrary"),
        name="experts",
    )(tile_ea, tile_eb, n_used, new_a, new_b, x_slots, w_gate, w_up, w_down, w_gate, w_up, w_down)


def _moe_sum(y_tiles, w_ref):
    ya, yb = _unpack_pair(y_tiles.reshape(y_tiles.shape[0], D_MODEL))
    w = w_ref[...]
    return w[:, 0:1] * ya + w[:, 1:2] * yb


def _combine_kernel(pos_ref, h_ref, w_ref, m_ref, y_hbm, g_ref, b_ref, h2_ref, hb_ref, ybuf, sem, *, tm):
    i = pl.program_id(0)
    n = pl.num_programs(0)
    slot = i % 2

    @pl.when(i == 0)
    def _():
        _gather_rows(y_hbm, pos_ref, 0, ybuf.at[0], sem.at[0], tm)

    @pl.when(i + 1 < n)
    def _():
        _gather_rows(y_hbm, pos_ref, (i + 1) * tm, ybuf.at[1 - slot], sem.at[1 - slot], tm)

    _wait_rows(y_hbm, ybuf.at[slot], sem.at[slot], tm)
    h2 = _layer_norm(DN_ALPHA * h_ref[...] + _moe_sum(ybuf[slot], w_ref), g_ref[...], b_ref[...])
    h2_ref[...] = h2
    hb_ref[...] = (h2 * m_ref[...]).astype(BF16)


def _combine_out_kernel(pos_ref, h_ref, w_ref, y_hbm, g_ref, b_ref, out_hbm, ybuf, obuf, sem, osem, *, tm, nblk, n):
    s = pl.program_id(0)
    slot = s % 2
    put = functools.partial(_frame_block_copy, out_hbm, obuf, osem, blk=tm, nblk=nblk, to_hbm=True)

    @pl.when(s == 0)
    def _():
        _gather_rows(y_hbm, pos_ref, 0, ybuf.at[0], sem.at[0], tm)

    @pl.when(s + 1 < n)
    def _():
        _gather_rows(y_hbm, pos_ref, (s + 1) * tm, ybuf.at[1 - slot], sem.at[1 - slot], tm)

    @pl.when(s >= 2)
    def _():
        put(s - 2, slot, wait=True)

    _wait_rows(y_hbm, ybuf.at[slot], sem.at[slot], tm)
    obuf[slot] = _layer_norm(DN_ALPHA * h_ref[...] + _moe_sum(ybuf[slot], w_ref), g_ref[...], b_ref[...])
    put(s, slot, wait=False)

    @pl.when(s == n - 1)
    def _():
        if n >= 2:
            put(s - 1, 1 - slot, wait=True)
        put(s, slot, wait=True)


def _combine_out(h1, wcols, y_slots, pos, ln_g, ln_b, batch, seq):
    t, d = h1.shape
    frame = t // batch
    tm = _pick_tile(frame, 512, CHUNK)
    nblk = frame // tm
    n = t // tm
    row = pl.BlockSpec((tm, d), lambda i, pos: (i, 0))
    vec = pl.BlockSpec((1, d), lambda i, pos: (0, 0))
    grid_spec = pltpu.PrefetchScalarGridSpec(
        num_scalar_prefetch=1,
        grid=(n,),
        in_specs=[row, pl.BlockSpec((tm, LANES), lambda i, pos: (i, 0)), pl.BlockSpec(memory_space=pl.ANY), vec, vec],
        out_specs=pl.BlockSpec(memory_space=pl.ANY),
        scratch_shapes=[pltpu.VMEM((2, tm, SUBLANES, LANES), jnp.uint32), pltpu.VMEM((2, tm, d), F32),
                        pltpu.SemaphoreType.DMA((2,)), pltpu.SemaphoreType.DMA((2,))],
    )
    return pl.pallas_call(
        functools.partial(_combine_out_kernel, tm=tm, nblk=nblk, n=n),
        grid_spec=grid_spec,
        out_shape=jax.ShapeDtypeStruct((batch, seq, d), F32),
        compiler_params=_params("arbitrary"),
        name="combine_out",
    )(pos, h1, wcols, y_slots, ln_g, ln_b)


def _combine(h1, wcols, rowmask, y_slots, pos, ln_g, ln_b):
    t, d = h1.shape
    tm = _pick_tile(t, 512, 16)
    row = pl.BlockSpec((tm, d), lambda i, pos: (i, 0))
    vec = pl.BlockSpec((1, d), lambda i, pos: (0, 0))
    grid_spec = pltpu.PrefetchScalarGridSpec(
        num_scalar_prefetch=1,
        grid=(t // tm,),
        in_specs=[row, pl.BlockSpec((tm, LANES), lambda i, pos: (i, 0)), pl.BlockSpec((tm, 1), lambda i, pos: (i, 0)),
                  pl.BlockSpec(memory_space=pl.ANY), vec, vec],
        out_specs=[row, row],
        scratch_shapes=[pltpu.VMEM((2, tm, SUBLANES, LANES), jnp.uint32), pltpu.SemaphoreType.DMA((2,))],
    )
    return pl.pallas_call(
        functools.partial(_combine_kernel, tm=tm),
        grid_spec=grid_spec,
        out_shape=[jax.ShapeDtypeStruct((t, d), F32), jax.ShapeDtypeStruct((t, d), BF16)],
        compiler_params=_params("arbitrary"),
        name="combine",
    )(pos, h1, wcols, rowmask, y_slots, ln_g, ln_b)


_PAIR_A = np.array([p[0] for p in PAIRS], np.int32)
_PAIR_B = np.array([p[1] for p in PAIRS], np.int32)


def _dispatch_plan(route, t, layer):
    route = jnp.transpose(route, (1, 0, 2)).reshape(SUBLANES, t)
    cls = route[0].astype(jnp.int32)
    onehot = (cls[:, None] == jnp.arange(N_CLASSES, dtype=jnp.int32)[None, :]).astype(jnp.int32)
    csum = jnp.cumsum(onehot, axis=0)
    counts = csum[-1]
    rank = jnp.sum(onehot * csum, axis=1) - 1
    tiles_per = (counts + EXPERT_TM - 1) // EXPERT_TM
    tile_end = jnp.cumsum(tiles_per)
    cls_start = (tile_end - tiles_per) * EXPERT_TM
    pos = cls_start[cls] + rank
    n_tiles = -(-t // EXPERT_TM) + N_CLASSES
    n_used = tile_end[-1]
    tile_ids = jnp.minimum(jnp.arange(n_tiles, dtype=jnp.int32), n_used - 1)
    tile_cls = jnp.sum((tile_ids[:, None] >= tile_end[None, :]).astype(jnp.int32), axis=1)
    grp = tile_cls // len(PAIRS)
    pair = tile_cls % len(PAIRS)
    tile_ea = layer * N_EXPERTS + grp * EXPERTS_PER_GROUP + jnp.asarray(_PAIR_A)[pair]
    tile_eb = layer * N_EXPERTS + grp * EXPERTS_PER_GROUP + jnp.asarray(_PAIR_B)[pair]
    wcols = jnp.pad(route[1:3].T, ((0, 0), (0, LANES - 2)))
    zero_tile = jnp.maximum(tile_end - 1, 0).astype(jnp.int32)
    zero_on = (tiles_per > 0).astype(jnp.int32)
    return (pos.astype(jnp.int32), wcols, tile_ea, tile_eb, n_used.reshape(1).astype(jnp.int32),
            zero_tile, zero_on, n_tiles * EXPERT_TM)


def kernel(x, meta_tokens, ln_in_g, ln_in_b, w_router, w_in, conv_w, conv_b, dt_bias, a_log, d_skip,
           ssd_norm_w, w_ssd_out, w_pool, pool_scale, b_gate, w_out, ln1_g, ln1_b, w_exp_gate, w_exp_up,
           w_exp_down, ln2_g, ln2_b):
    batch, seq, d = x.shape
    frame = FRAME_HEAD + seq
    assert d == D_MODEL and frame % CHUNK == 0
    t = batch * frame
    depth = w_in.shape[0]

    head = jnp.concatenate([jnp.zeros((PAD_FRONT, d), x.dtype), meta_tokens.astype(x.dtype)], axis=0)
    rowmask = jnp.asarray(np.tile(np.arange(frame) >= PAD_FRONT, batch).astype(np.float32).reshape(t, 1))
    h, hb = _embed_ln(x, head, ln_in_g, ln_in_b)

    wr_t = w_router.T
    wr_hi = wr_t.astype(BF16)
    wr_lo = (wr_t - wr_hi.astype(F32)).astype(BF16)

    def pad_lanes(v):
        return jnp.pad(v, (0, LANES - v.shape[0])).reshape(1, LANES)

    wg_all = w_exp_gate.reshape(depth * N_EXPERTS, d, D_EXPERT)
    wu_all = w_exp_up.reshape(depth * N_EXPERTS, d, D_EXPERT)
    wd_all = w_exp_down.reshape(depth * N_EXPERTS, D_EXPERT, d)

    for l in range(depth):
        w = w_in[l]
        w_xbc = w[:, COL_XBC:COL_DT].astype(BF16)
        w_plain = jnp.concatenate([w[:, COL_Z:COL_XBC], w[:, COL_GATE:], w[:, COL_POOL:COL_GATE]], axis=1).astype(BF16)
        w_dt = jnp.pad(w[:, COL_DT:COL_POOL], ((0, 0), (0, LANES - SSD_HEADS))).astype(BF16)
        xact, plain, dt_raw = _in_proj(hb, w_xbc, w_plain, w_dt, conv_w[l], conv_b[l].reshape(1, SSD_XBC))

        yn = _ssd(
            xact, plain, dt_raw, pad_lanes(dt_bias[l]),
            pad_lanes(-jnp.exp(a_log[l])), jnp.repeat(d_skip[l], SSD_HEAD_DIM).reshape(1, SSD_INNER),
            ssd_norm_w[l].reshape(1, SSD_INNER), batch, frame)

        h1, route = _mix(
            yn, plain, h, w_ssd_out[l].astype(BF16), w_pool[l].astype(BF16), w_out[l].astype(BF16),
            b_gate[l].reshape(1, N_BRANCH * d), pool_scale[l].reshape(1, POOL_WIDTH),
            ln1_g[l].reshape(1, d), ln1_b[l].reshape(1, d), wr_hi, wr_lo, batch, frame)

        pos, wcols, tile_ea, tile_eb, n_used, zero_tile, zero_on, n_slots = _dispatch_plan(route, t, l)
        x_slots = _dispatch(h1, pos, zero_tile, zero_on, n_used, n_slots)
        y_slots = _experts(x_slots, tile_ea, tile_eb, n_used, wg_all, wu_all, wd_all)
        if l == depth - 1:
            return _combine_out(h1, wcols, y_slots, pos, ln2_g[l].reshape(1, d), ln2_b[l].reshape(1, d), batch, seq)
        h, hb = _combine(h1, wcols, rowmask, y_slots, pos, ln2_g[l].reshape(1, d), ln2_b[l].reshape(1, d))
```

```python
import functools

import jax
import jax.numpy as jnp
import numpy as np
from jax import lax
from jax.experimental import pallas as pl
from jax.experimental.pallas import tpu as pltpu

F32 = jnp.float32
BF16 = jnp.bfloat16

D_MODEL = 1024
CHUNK = 64
N_META = 16
PAD_FRONT = CHUNK - N_META
FRAME_HEAD = PAD_FRONT + N_META

SSD_INNER = 2048
SSD_HEAD_DIM = 64
SSD_HEADS = 32
SSD_GROUPS = 8
SSD_HPG = 4
SSD_STATE = 128
SSD_CONV = 4
SSD_GN = SSD_GROUPS * SSD_STATE
SSD_XBC = SSD_INNER + 2 * SSD_GN
SSD_GROUP_DIM = SSD_HPG * SSD_HEAD_DIM

POOL_WIDTH = 1024
POOL_WINDOWS = (2, 4, 8, 16)
POOL_GROUP_DIM = 256
POOL_HALO = 16

N_BRANCH = 2
COL_Z = 0
COL_XBC = COL_Z + SSD_INNER
COL_DT = COL_XBC + SSD_XBC
COL_POOL = COL_DT + SSD_HEADS
COL_GATE = COL_POOL + POOL_WIDTH
IN_COLS = COL_GATE + N_BRANCH * D_MODEL

N_EXPERTS = 16
N_EXPERT_GROUPS = 4
EXPERTS_PER_GROUP = 4
D_EXPERT = 512
PAIRS = ((0, 1), (0, 2), (0, 3), (1, 3), (1, 2), (3, 2))
N_CLASSES = N_EXPERT_GROUPS * len(PAIRS)

DEPTH = 2
DN_ALPHA = (2.0 * DEPTH) ** 0.25
LN_EPS = 1e-5
RMS_EPS = 1e-5

LANES = 128
SUBLANES = 8
VMEM_LIMIT = 56 * 1024 * 1024

PROJ_TN = 1024
PLAIN_COLS = SSD_INNER + N_BRANCH * D_MODEL + POOL_WIDTH
PLAIN_BLK_Z = 0
PLAIN_BLK_GATE = 1
PLAIN_BLK_POOL = (SSD_INNER + N_BRANCH * D_MODEL) // POOL_WIDTH

EXPERT_TM = 256


def _pick_tile(n, target, mult):
    best = None
    for t in range(mult, min(n, target) + 1, mult):
        if n % t == 0:
            best = t
    assert best is not None, (n, target, mult)
    return best


def _params(*sem):
    return pltpu.CompilerParams(dimension_semantics=sem, vmem_limit_bytes=VMEM_LIMIT)


def _sigmoid(x):
    return 0.5 + 0.5 * jnp.tanh(0.5 * x)


def _silu(x):
    hx = 0.5 * x
    return hx + hx * jnp.tanh(hx)


def _layer_norm(x, g, b):
    mu = jnp.mean(x, axis=-1, keepdims=True)
    xc = x - mu
    var = jnp.mean(xc * xc, axis=-1, keepdims=True)
    return xc * lax.rsqrt(var + LN_EPS) * g + b


def _frame_block_copy(hbm, buf, sem, step, slot, *, blk, nblk, to_hbm, wait):
    b = step // nblk
    j = step - b * nblk

    def run(vm, hb):
        cp = pltpu.make_async_copy(vm, hb, sem.at[slot]) if to_hbm else pltpu.make_async_copy(hb, vm, sem.at[slot])
        if wait:
            cp.wait()
        else:
            cp.start()

    if blk > FRAME_HEAD:
        @pl.when(j == 0)
        def _():
            n = blk - FRAME_HEAD
            run(buf.at[slot, pl.ds(FRAME_HEAD, n)], hbm.at[b, pl.ds(0, n)])

    @pl.when(j > 0)
    def _():
        start = pl.multiple_of(j * blk - FRAME_HEAD, CHUNK)
        run(buf.at[slot], hbm.at[b, pl.ds(start, blk)])


def _embed_kernel(x_hbm, head_ref, g_ref, b_ref, h_ref, hb_ref, xbuf, sem, *, blk, nblk, n):
    s = pl.program_id(0)
    slot = s % 2
    copy = functools.partial(_frame_block_copy, x_hbm, xbuf, sem, blk=blk, nblk=nblk, to_hbm=False)

    @pl.when(s == 0)
    def _():
        copy(0, 0, wait=False)

    @pl.when(s + 1 < n)
    def _():
        copy(s + 1, 1 - slot, wait=False)

    copy(s, slot, wait=True)
    first = s % nblk == 0

    @pl.when(first)
    def _():
        xbuf[slot, 0:FRAME_HEAD, :] = head_ref[...]

    y = _layer_norm(xbuf[slot], g_ref[...], b_ref[...])
    h_ref[...] = y
    row = lax.broadcasted_iota(jnp.int32, (blk, 1), 0)
    keep = jnp.logical_or(jnp.logical_not(first), row >= PAD_FRONT)
    hb_ref[...] = jnp.where(keep, y, 0.0).astype(BF16)


def _embed_ln(x, head, g, b):
    batch, seq, d = x.shape
    frame = FRAME_HEAD + seq
    blk = _pick_tile(frame, 832, CHUNK)
    nblk = frame // blk
    n = batch * nblk
    row = pl.BlockSpec((blk, d), lambda s: (s, 0))
    vec = pl.BlockSpec((1, d), lambda s: (0, 0))
    return pl.pallas_call(
        functools.partial(_embed_kernel, blk=blk, nblk=nblk, n=n),
        grid=(n,),
        in_specs=[pl.BlockSpec(memory_space=pl.ANY), pl.BlockSpec((FRAME_HEAD, d), lambda s: (0, 0)), vec, vec],
        out_specs=[row, row],
        out_shape=[jax.ShapeDtypeStruct((batch * frame, d), F32), jax.ShapeDtypeStruct((batch * frame, d), BF16)],
        scratch_shapes=[pltpu.VMEM((2, blk, d), F32), pltpu.SemaphoreType.DMA((2,))],
        compiler_params=_params("arbitrary"),
        name="embed_ln",
    )(x, head, g.reshape(1, d), b.reshape(1, d))


CONV_TILES = SSD_XBC // PROJ_TN


def _in_proj_kernel(x_ref, wx_ref, wp_ref, wdt_ref, cw_ref, cb_ref, xo_ref, po_ref, dt_ref, tail_ref, *, slab):
    i = pl.program_id(0)
    j = pl.program_id(1)
    tm = x_ref.shape[0]

    @pl.when(j == 0)
    def _():
        dt_ref[...] = jnp.dot(x_ref[...], wdt_ref[...], preferred_element_type=F32)

    @pl.when(j >= CONV_TILES)
    def _():
        po_ref[...] = jnp.dot(x_ref[...], wp_ref[...], preferred_element_type=F32).astype(BF16)

    @pl.when(j < CONV_TILES)
    def _():
        jt = jnp.minimum(j, CONV_TILES - 1)

        @pl.when(i == 0)
        def _():
            tail_ref[jt] = jnp.zeros((SUBLANES, PROJ_TN), F32)

        prev8 = tail_ref[jt]
        w = wx_ref[...]
        wp = wp_ref[...]
        sub = lax.broadcasted_iota(jnp.int32, (1, SUBLANES, PROJ_TN), 1)
        cw = [cw_ref[k:k + 1, :].reshape(1, 1, PROJ_TN) for k in range(SSD_CONV)]
        cb = cb_ref[...].reshape(1, 1, PROJ_TN)

        def shift_rows(v3, head8, s):
            r = pltpu.roll(v3, s, axis=1)
            above = jnp.concatenate([pltpu.roll(head8, s, axis=0)[None], r[:-1]], axis=0)
            return jnp.where(sub < s, above, r)

        for s in range(tm // slab):
            rows = slice(s * slab, (s + 1) * slab)
            acc = jnp.dot(x_ref[rows, :], w, preferred_element_type=F32)
            x0 = acc.reshape(slab // SUBLANES, SUBLANES, PROJ_TN)
            x2 = shift_rows(x0, prev8, 2)
            odd = cw[2] * x0 + cw[0] * x2
            odd_head = cw[2][0] * prev8 + cw[0][0] * pltpu.roll(prev8, 2, axis=0)
            conv = cb + cw[3] * x0 + cw[1] * x2 + shift_rows(odd, odd_head, 1)
            xo_ref[rows, :] = _silu(conv).reshape(slab, PROJ_TN).astype(BF16)
            prev8 = acc[slab - SUBLANES:]
            po_ref[rows, :] = jnp.dot(x_ref[rows, :], wp, preferred_element_type=F32).astype(BF16)
        tail_ref[jt] = prev8


def _in_proj(hb, w_xbc, w_plain, w_dt, conv_w, conv_b):
    t, d = hb.shape
    tm = _pick_tile(t, 2080, 16)
    slab = _pick_tile(tm, 256, 16)

    def conv_col(i, j):
        return (0, jnp.minimum(j, CONV_TILES - 1))

    return pl.pallas_call(
        functools.partial(_in_proj_kernel, slab=slab),
        grid=(t // tm, PLAIN_COLS // PROJ_TN),
        in_specs=[
            pl.BlockSpec((tm, d), lambda i, j: (i, 0)),
            pl.BlockSpec((d, PROJ_TN), conv_col),
            pl.BlockSpec((d, PROJ_TN), lambda i, j: (0, j)),
            pl.BlockSpec((d, LANES), lambda i, j: (0, 0)),
            pl.BlockSpec((SSD_CONV, PROJ_TN), conv_col),
            pl.BlockSpec((1, PROJ_TN), conv_col),
        ],
        out_specs=[
            pl.BlockSpec((tm, PROJ_TN), lambda i, j: (i, jnp.minimum(j, CONV_TILES - 1))),
            pl.BlockSpec((tm, PROJ_TN), lambda i, j: (i, j)),
            pl.BlockSpec((tm, LANES), lambda i, j: (i, 0)),
        ],
        out_shape=[jax.ShapeDtypeStruct((t, SSD_XBC), BF16), jax.ShapeDtypeStruct((t, PLAIN_COLS), BF16),
                   jax.ShapeDtypeStruct((t, LANES), F32)],
        scratch_shapes=[pltpu.VMEM((CONV_TILES, SUBLANES, PROJ_TN), F32)],
        compiler_params=_params("arbitrary", "arbitrary"),
        name="in_proj",
    )(hb, w_xbc, w_plain, w_dt, conv_w, conv_b)


def _pair_select(lane_lo, col, h0):
    return jnp.where(lane_lo, col[:, h0:h0 + 1], col[:, h0 + 1:h0 + 2])


def _ssd_kernel(xbc_ref, z_ref, dt_ref, dtb_ref, a_ref, dsk_ref, nw_ref, o_ref, prev_ref, y_ref, *, tb):
    j = pl.program_id(1)

    @pl.when(j == 0)
    def _():
        prev_ref[...] = jnp.zeros_like(prev_ref)

    row_i = lax.broadcasted_iota(jnp.int32, (CHUNK, LANES), 0)
    row_c = lax.broadcasted_iota(jnp.int32, (CHUNK, 1), 0)
    lane_i = lax.broadcasted_iota(jnp.int32, (CHUNK, LANES), 1)
    lane_lo = lane_i < SSD_HEAD_DIM
    causal2 = row_i >= jnp.where(lane_lo, lane_i, lane_i - SSD_HEAD_DIM)
    lane_lo_row = lax.broadcasted_iota(jnp.int32, (1, LANES), 1) < SSD_HEAD_DIM

    def chunk(c, carry):
        r0 = pl.multiple_of(c * CHUNK, CHUNK)
        rows = pl.ds(r0, CHUNK)
        valid = (j * tb + r0 + row_c) >= PAD_FRONT

        dtr = dt_ref[rows, :] + dtb_ref[...]
        dt = jnp.maximum(dtr, 0.0) + jnp.log1p(jnp.exp(-jnp.abs(dtr)))
        dt = jnp.where(valid, dt, 0.0)
        a_cs = dt * a_ref[...]
        for s in (1, 2, 4, 8, 16, 32):
            a_cs = a_cs + jnp.where(row_i >= s, pltpu.roll(a_cs, s, axis=0), 0.0)
        a_last = a_cs[CHUNK - 1:CHUNK, :]
        dtw = dt * jnp.exp(a_last - a_cs)
        cdec = jnp.exp(a_last)
        a_t = jnp.concatenate([a_cs, a_cs], axis=0).T
        dt_t = jnp.concatenate([dt, dt], axis=0).T

        def bmat(g):
            return xbc_ref[rows, SSD_INNER + g * SSD_STATE:SSD_INNER + (g + 1) * SSD_STATE]

        def cmat(g):
            o0 = SSD_INNER + SSD_GN + g * SSD_STATE
            return xbc_ref[rows, o0:o0 + SSD_STATE]

        cb2s, yoffs = [], []
        for g in range(SSD_GROUPS):
            bm, cm = bmat(g), cmat(g)
            cb2s.append(lax.dot_general(cm, jnp.concatenate([bm, bm], axis=0),
                                        (((1,), (1,)), ((), ())), preferred_element_type=F32))
            yoffs.append(jnp.dot(cm, prev_ref[g].astype(BF16), preferred_element_type=F32))

        for g in range(SSD_GROUPS):
            for q in range(SSD_HPG // 2):
                h0 = g * SSD_HPG + 2 * q
                ps = slice(g * SSD_GROUP_DIM + q * LANES, g * SSD_GROUP_DIM + (q + 1) * LANES)
                xh_b = xbc_ref[rows, ps]
                a_col = _pair_select(lane_lo, a_cs, h0)
                a_row = jnp.where(lane_lo_row, a_t[h0:h0 + 1, :], a_t[h0 + 1:h0 + 2, :])
                dt_row = jnp.where(lane_lo_row, dt_t[h0:h0 + 1, :], dt_t[h0 + 1:h0 + 2, :])
                decay = jnp.exp(jnp.where(causal2, a_col - a_row, -jnp.inf))
                m = (cb2s[g] * decay * dt_row).astype(BF16)
                zero = jnp.zeros_like(xh_b)
                rhs = jnp.concatenate([jnp.where(lane_lo, xh_b, zero), jnp.where(lane_lo, zero, xh_b)], axis=0)
                y_diag = jnp.dot(m, rhs, preferred_element_type=F32)
                y_off = yoffs[g][:, q * LANES:(q + 1) * LANES] * jnp.exp(a_col)
                y_ref[:, ps] = y_diag + y_off + dsk_ref[:, ps] * xh_b.astype(F32)

        for g in range(SSD_GROUPS):
            xs_parts, cdec_parts = [], []
            for q in range(SSD_HPG // 2):
                h0 = g * SSD_HPG + 2 * q
                ps = slice(g * SSD_GROUP_DIM + q * LANES, g * SSD_GROUP_DIM + (q + 1) * LANES)
                xs_parts.append((xbc_ref[rows, ps].astype(F32) * _pair_select(lane_lo, dtw, h0)).astype(BF16))
                cdec_parts.append(jnp.where(lane_lo_row, cdec[:, h0:h0 + 1], cdec[:, h0 + 1:h0 + 2]))
            xs_dec = jnp.concatenate(xs_parts, axis=1)
            st = lax.dot_general(bmat(g), xs_dec, (((0,), (0,)), ((), ())), preferred_element_type=F32)
            prev_ref[g] = prev_ref[g] * jnp.concatenate(cdec_parts, axis=1) + st

        for g in range(SSD_GROUPS):
            gs = slice(g * SSD_GROUP_DIM, (g + 1) * SSD_GROUP_DIM)
            yg = y_ref[:, gs] * _silu(z_ref[rows, gs].astype(F32))
            ms = jnp.mean(yg * yg, axis=-1, keepdims=True)
            o_ref[rows, gs] = (yg * lax.rsqrt(ms + RMS_EPS) * nw_ref[:, gs]).astype(BF16)
        return carry

    lax.fori_loop(0, tb // CHUNK, chunk, 0)


def _ssd(xact, plain, dt_raw, dt_bias, a_neg, d_skip_ch, norm_w, batch, frame):
    t = xact.shape[0]
    tb = _pick_tile(frame, 832, CHUNK)
    nblk = frame // tb

    def vec(n):
        return pl.BlockSpec((1, n), lambda b, j: (0, 0))

    return pl.pallas_call(
        functools.partial(_ssd_kernel, tb=tb),
        grid=(batch, nblk),
        in_specs=[
            pl.BlockSpec((tb, SSD_XBC), lambda b, j: (b * nblk + j, 0)),
            pl.BlockSpec((tb, SSD_INNER), lambda b, j: (b * nblk + j, PLAIN_BLK_Z)),
            pl.BlockSpec((tb, LANES), lambda b, j: (b * nblk + j, 0)),
            vec(LANES), vec(LANES), vec(SSD_INNER), vec(SSD_INNER),
        ],
        out_specs=pl.BlockSpec((tb, SSD_INNER), lambda b, j: (b * nblk + j, 0)),
        out_shape=jax.ShapeDtypeStruct((t, SSD_INNER), BF16),
        scratch_shapes=[
            pltpu.VMEM((SSD_GROUPS, SSD_STATE, SSD_GROUP_DIM), F32),
            pltpu.VMEM((CHUNK, SSD_INNER), F32),
        ],
        compiler_params=_params("parallel", "arbitrary"),
        name="ssd",
    )(xact, plain, dt_raw, dt_bias, a_neg, d_skip_ch, norm_w)


def _route(logits):
    mx = jnp.max(logits, axis=0, keepdims=True)
    ex = jnp.exp(logits - mx)
    sc = ex / jnp.sum(ex, axis=0, keepdims=True)
    rows = [sc[e:e + 1, :] for e in range(N_EXPERTS)]
    best = None
    for g in range(N_EXPERT_GROUPS):
        a, b, c, d = rows[4 * g:4 * g + 4]
        hi1, lo1 = jnp.maximum(a, b), jnp.minimum(a, b)
        hi2, lo2 = jnp.maximum(c, d), jnp.minimum(c, d)
        gsc = jnp.maximum(hi1, hi2) + jnp.maximum(jnp.minimum(hi1, hi2), jnp.maximum(lo1, lo2))
        if best is None:
            best, sel = gsc, jnp.zeros_like(gsc)
        else:
            better = gsc > best
            sel = jnp.where(better, float(g), sel)
            best = jnp.maximum(best, gsc)
    v = []
    for i in range(EXPERTS_PER_GROUP):
        vi = rows[i]
        for g in range(1, N_EXPERT_GROUPS):
            vi = jnp.where(sel == float(g), rows[4 * g + i], vi)
        v.append(vi)

    def first_max(u):
        m = jnp.maximum(jnp.maximum(u[0], u[1]), jnp.maximum(u[2], u[3]))
        idx = jnp.where(u[0] == m, 0.0, jnp.where(u[1] == m, 1.0, jnp.where(u[2] == m, 2.0, 3.0)))
        return m, idx

    m1, i1 = first_max(v)
    u = [jnp.where(i1 == float(i), -1.0, v[i]) for i in range(EXPERTS_PER_GROUP)]
    m2, i2 = first_max(u)
    den = m1 + m2
    w1, w2 = m1 / den, m2 / den
    lo = jnp.minimum(i1, i2)
    hi = jnp.maximum(i1, i2)
    w_lo = jnp.where(i1 < i2, w1, w2)
    w_hi = jnp.where(i1 < i2, w2, w1)
    pair = jnp.where(lo == 0.0, hi - 1.0, jnp.where(lo == 1.0, jnp.where(hi == 3.0, 3.0, 4.0), 5.0))
    swap = pair == 5.0
    w_a = jnp.where(swap, w_hi, w_lo)
    w_b = jnp.where(swap, w_lo, w_hi)
    cls = sel * float(len(PAIRS)) + pair
    zero = jnp.zeros_like(cls)
    return jnp.concatenate([cls, w_a, w_b, zero, zero, zero, zero, zero], axis=0)


def _mix_kernel(yn_ref, gate_ref, pool_ref, h_ref, wso_ref, wp_ref, wo_ref, bg_ref, ps_ref, g_ref, b_ref,
                wr_hi_ref, wr_lo_ref, h1_ref, route_ref, halo_ref, *, tb):
    j = pl.program_id(1)

    @pl.when(j == 0)
    def _():
        halo_ref[0:POOL_HALO, :] = jnp.zeros((POOL_HALO, POOL_WIDTH), F32)

    halo_ref[POOL_HALO:POOL_HALO + tb, :] = pool_ref[...].astype(F32)
    half = (tb // 2) // LANES * LANES
    subs = [(0, half), (half, tb - half)] if half >= 2 * LANES else [(0, tb)]
    nt = (((1,), (1,)), ((), ()))

    for r, n in subs:
        gates = _sigmoid(gate_ref[r:r + n, :].astype(F32) + bg_ref[...])

        tpos = (j * tb + r - PAD_FRONT + 1 + lax.broadcasted_iota(jnp.int32, (n, 1), 0)).astype(F32)
        pools = []
        for gi, win in enumerate(POOL_WINDOWS):
            cs = slice(gi * POOL_GROUP_DIM, (gi + 1) * POOL_GROUP_DIM)
            s = halo_ref[r:r + n + POOL_HALO, cs]
            step = 1
            while step < win:
                s = s[step:] + s[:-step]
                step *= 2
            wsum = s[s.shape[0] - n:]
            cnt = jnp.clip(tpos, 1.0, float(win))
            pooled = wsum / cnt - halo_ref[POOL_HALO + r:POOL_HALO + r + n, cs]
            pools.append(jnp.dot(pooled.astype(BF16), wp_ref[gi], preferred_element_type=F32))
        y_pool = jnp.concatenate(pools, axis=1) * ps_ref[...]

        y_ssd = jnp.dot(yn_ref[r:r + n, :], wso_ref[...], preferred_element_type=F32)
        mixed =(gates[:, :D_MODEL] * y_ssd + gates[:, D_MODEL:] * y_pool).astype(BF16)
        out = jnp.dot(mixed, wo_ref[...], preferred_element_type=F32)
        h1 = _layer_norm(DN_ALPHA * h_ref[r:r + n, :] + out, g_ref[...], b_ref[...])
        h1_ref[r:r + n, :] = h1

        h_hi = h1.astype(BF16)
        h_lo = (h1 - h_hi.astype(F32)).astype(BF16)
        logits = (lax.dot_general(wr_hi_ref[...], h_hi, nt, preferred_element_type=F32)
                  + lax.dot_general(wr_lo_ref[...], h_hi, nt, preferred_element_type=F32)
                  + lax.dot_general(wr_hi_ref[...], h_lo, nt, preferred_element_type=F32))
        route_ref[0, :, r:r + n] = _route(logits)

    halo_ref[0:POOL_HALO, :] = halo_ref[tb:tb + POOL_HALO, :]


def _mix(yn, plain, h, w_ssd_out, w_pool, w_out, b_gate, pool_scale, ln_g, ln_b, wr_hi, wr_lo, batch, frame):
    t = yn.shape[0]
    tb = _pick_tile(frame, 832, CHUNK)
    nblk = frame // tb
    d = D_MODEL

    def full(shape):
        return pl.BlockSpec(shape, lambda b, j: (0,) * len(shape))

    return pl.pallas_call(
        functools.partial(_mix_kernel, tb=tb),
        grid=(batch, nblk),
        in_specs=[
            pl.BlockSpec((tb, SSD_INNER), lambda b, j: (b * nblk + j, 0)),
            pl.BlockSpec((tb, N_BRANCH * d), lambda b, j: (b * nblk + j, PLAIN_BLK_GATE)),
            pl.BlockSpec((tb, POOL_WIDTH), lambda b, j: (b * nblk + j, PLAIN_BLK_POOL)),
            pl.BlockSpec((tb, d), lambda b, j: (b * nblk + j, 0)),
            full((SSD_INNER, d)), full((len(POOL_WINDOWS), POOL_GROUP_DIM, POOL_GROUP_DIM)), full((d, d)),
            full((1, N_BRANCH * d)), full((1, POOL_WIDTH)), full((1, d)), full((1, d)),
            full((N_EXPERTS, d)), full((N_EXPERTS, d)),
        ],
        out_specs=[
            pl.BlockSpec((tb, d), lambda b, j: (b * nblk + j, 0)),
            pl.BlockSpec((1, SUBLANES, tb), lambda b, j: (b * nblk + j, 0, 0)),
        ],
        out_shape=[jax.ShapeDtypeStruct((t, d), F32),
                   jax.ShapeDtypeStruct((batch * nblk, SUBLANES, tb), F32)],
        scratch_shapes=[pltpu.VMEM((POOL_HALO + tb, POOL_WIDTH), F32)],
        compiler_params=_params("parallel", "arbitrary"),
        name="mix",
    )(yn, plain, plain, h, w_ssd_out, w_pool, w_out, b_gate, pool_scale, ln_g, ln_b, wr_hi, wr_lo)


def _as_tiles(v):
    return v.reshape(v.shape[0], SUBLANES, LANES)


def _row_copies(hbm, idx_ref, base, buf, sem, n, *, to_hbm, wait):
    def body(g, carry):
        for u in range(SUBLANES):
            r = g * SUBLANES + u
            row = 0 if wait else idx_ref[base + r]
            vm = buf.at[r]
            hb = hbm.at[row]
            cp = pltpu.make_async_copy(vm, hb, sem) if to_hbm else pltpu.make_async_copy(hb, vm, sem)
            if wait:
                cp.wait()
            else:
                cp.start()
        return carry
    lax.fori_loop(0, n // SUBLANES, body, 0)


def _gather_rows(src_hbm, idx_ref, base, buf, sem, n):
    _row_copies(src_hbm, idx_ref, base, buf, sem, n, to_hbm=False, wait=False)


def _wait_rows(src_hbm, buf, sem, n):
    _row_copies(src_hbm, None, 0, buf, sem, n, to_hbm=False, wait=True)


def _scatter_rows(buf, idx_ref, base, dst_hbm, sem, n):
    _row_copies(dst_hbm, idx_ref, base, buf, sem, n, to_hbm=True, wait=False)


def _wait_scattered(buf, dst_hbm, sem, n):
    _row_copies(dst_hbm, None, 0, buf, sem, n, to_hbm=True, wait=True)


def _dispatch_kernel(pos_ref, zt_ref, zon_ref, nu_ref, h_ref, x_hbm, stage, zbuf, sem, zsem, *,
                     tm, n, n_tiles):
    i = pl.program_id(0)
    slot = i % 2

    @pl.when(i == 0)
    def _():
        zbuf[...] = jnp.zeros_like(zbuf)

        def zero_tile(tile):
            row0 = pl.multiple_of(tile * EXPERT_TM, EXPERT_TM)
            cp = pltpu.make_async_copy(zbuf, x_hbm.at[pl.ds(row0, EXPERT_TM)], zsem)
            cp.start()
            cp.wait()

        for c in range(N_CLASSES):
            pl.when(zon_ref[c] > 0)(functools.partial(zero_tile, zt_ref[c]))
            pl.when(nu_ref[0] + c < n_tiles)(functools.partial(zero_tile, nu_ref[0] + c))

    @pl.when(i >= 2)
    def _():
        _wait_scattered(stage.at[slot], x_hbm, sem.at[slot], tm)

    stage[slot] = _as_tiles(h_ref[...])
    _scatter_rows(stage.at[slot], pos_ref, i * tm, x_hbm, sem.at[slot], tm)

    @pl.when(i == n - 1)
    def _():
        if n >= 2:
            _wait_scattered(stage.at[1 - slot], x_hbm, sem.at[1 - slot], tm)
        _wait_scattered(stage.at[slot], x_hbm, sem.at[slot], tm)


def _dispatch(h1, pos, zero_tile, zero_on, n_used, n_slots):
    t, d = h1.shape
    assert d == SUBLANES * LANES
    tm = _pick_tile(t, 512, 16)
    grid_spec = pltpu.PrefetchScalarGridSpec(
        num_scalar_prefetch=4,
        grid=(t // tm,),
        in_specs=[pl.BlockSpec((tm, d), lambda i, *_: (i, 0))],
        out_specs=pl.BlockSpec(memory_space=pl.ANY),
        scratch_shapes=[pltpu.VMEM((2, tm, SUBLANES, LANES), F32), pltpu.VMEM((EXPERT_TM, SUBLANES, LANES), F32),
                        pltpu.SemaphoreType.DMA((2,)), pltpu.SemaphoreType.DMA(())],
    )
    return pl.pallas_call(
        functools.partial(_dispatch_kernel, tm=tm, n=t // tm, n_tiles=n_slots // EXPERT_TM),
        grid_spec=grid_spec,
        out_shape=jax.ShapeDtypeStruct((n_slots, SUBLANES, LANES), F32),
        compiler_params=_params("arbitrary"),
        name="dispatch",
    )(pos, zero_tile, zero_on, n_used, h1)


def _pack_pair(a, b):
    a32 = lax.bitcast_convert_type(a.astype(BF16).astype(F32), jnp.uint32)
    b32 = lax.bitcast_convert_type(b.astype(BF16).astype(F32), jnp.uint32)
    return a32 | (b32 >> 16)


def _unpack_pair(w):
    a = lax.bitcast_convert_type(w & jnp.uint32(0xFFFF0000), F32)
    b = lax.bitcast_convert_type(w << 16, F32)
    return a, b


def _expert_kernel(ea_ref, eb_ref, nu_ref, newa_ref, newb_ref, x_ref, wga_ref, wua_ref, wda_ref,
                   wgb_ref, wub_ref, wdb_ref, o_ref, wg_s, wu_s, wd_s):
    i = pl.program_id(0)

    @pl.when(i < nu_ref[0])
    def _():
        @pl.when(newa_ref[i] > 0)
        def _():
            wg_s[0] = wga_ref[0].astype(BF16)
            wu_s[0] = wua_ref[0].astype(BF16)
            wd_s[0] = wda_ref[0].astype(BF16)

        @pl.when(newb_ref[i] > 0)
        def _():
            wg_s[1] = wgb_ref[0].astype(BF16)
            wu_s[1] = wub_ref[0].astype(BF16)
            wd_s[1] = wdb_ref[0].astype(BF16)

        x = x_ref[...].reshape(EXPERT_TM, D_MODEL).astype(BF16)

        def mlp(e):
            hid = _silu(jnp.dot(x, wg_s[e], preferred_element_type=F32)) * jnp.dot(x, wu_s[e], preferred_element_type=F32)
            return jnp.dot(hid.astype(BF16), wd_s[e], preferred_element_type=F32)

        o_ref[...] = _as_tiles(_pack_pair(mlp(0), mlp(1)))

    @pl.when(i >= nu_ref[0])
    def _():
        o_ref[...] = jnp.zeros_like(o_ref)


def _experts(x_slots, tile_ea, tile_eb, n_used, w_gate, w_up, w_down):
    n_slots = x_slots.shape[0]
    d = D_MODEL
    n_tiles = n_slots // EXPERT_TM
    tile = pl.BlockSpec((EXPERT_TM, SUBLANES, LANES), lambda i, *_: (i, 0, 0))
    first = jnp.ones((1,), jnp.int32)
    new_a = jnp.concatenate([first, (tile_ea[1:] != tile_ea[:-1]).astype(jnp.int32)])
    new_b = jnp.concatenate([first, (tile_eb[1:] != tile_eb[:-1]).astype(jnp.int32)])

    def wspec(shape, which):
        if which == 0:
            return pl.BlockSpec(shape, lambda i, ea, eb, *_: (ea[i], 0, 0))
        return pl.BlockSpec(shape, lambda i, ea, eb, *_: (eb[i], 0, 0))

    up = (1, d, D_EXPERT)
    down = (1, D_EXPERT, d)
    grid_spec = pltpu.PrefetchScalarGridSpec(
        num_scalar_prefetch=5,
        grid=(n_tiles,),
        in_specs=[
            tile,
            wspec(up, 0), wspec(up, 0), wspec(down, 0),
            wspec(up, 1), wspec(up, 1), wspec(down, 1),
        ],
        out_specs=tile,
        scratch_shapes=[pltpu.VMEM((2, d, D_EXPERT), BF16), pltpu.VMEM((2, d, D_EXPERT), BF16),
                        pltpu.VMEM((2, D_EXPERT, d), BF16)],
    )
    return pl.pallas_call(
        _expert_kernel,
        grid_spec=grid_spec,
        out_shape=jax.ShapeDtypeStruct((n_slots, SUBLANES, LANES), jnp.uint32),
        compiler_params=_params("arbitrary"),
        name="experts",
    )(tile_ea, tile_eb, n_used, new_a, new_b, x_slots, w_gate, w_up, w_down, w_gate, w_up, w_down)


def _moe_sum(y_tiles, w_ref):
    ya, yb = _unpack_pair(y_tiles.reshape(y_tiles.shape[0], D_MODEL))
    w = w_ref[...]
    return w[:, 0:1] * ya + w[:, 1:2] * yb


def _combine_kernel(pos_ref, h_ref, w_ref, m_ref, y_hbm, g_ref, b_ref, h2_ref, hb_ref, ybuf, sem, *, tm):
    i = pl.program_id(0)
    n = pl.num_programs(0)
    slot = i % 2

    @pl.when(i == 0)
    def _():
        _gather_rows(y_hbm, pos_ref, 0, ybuf.at[0], sem.at[0], tm)

    @pl.when(i + 1 < n)
    def _():
        _gather_rows(y_hbm, pos_ref, (i + 1) * tm, ybuf.at[1 - slot], sem.at[1 - slot], tm)

    _wait_rows(y_hbm, ybuf.at[slot], sem.at[slot], tm)
    h2 = _layer_norm(DN_ALPHA * h_ref[...] + _moe_sum(ybuf[slot], w_ref), g_ref[...], b_ref[...])
    h2_ref[...] = h2
    hb_ref[...] = (h2 * m_ref[...]).astype(BF16)


def _combine_out_kernel(pos_ref, h_ref, w_ref, y_hbm, g_ref, b_ref, out_hbm, ybuf, obuf, sem, osem, *, tm, nblk, n):
    s = pl.program_id(0)
    slot = s % 2
    put = functools.partial(_frame_block_copy, out_hbm, obuf, osem, blk=tm, nblk=nblk, to_hbm=True)

    @pl.when(s == 0)
    def _():
        _gather_rows(y_hbm, pos_ref, 0, ybuf.at[0], sem.at[0], tm)

    @pl.when(s + 1 < n)
    def _():
        _gather_rows(y_hbm, pos_ref, (s + 1) * tm, ybuf.at[1 - slot], sem.at[1 - slot], tm)

    @pl.when(s >= 2)
    def _():
        put(s - 2, slot, wait=True)

    _wait_rows(y_hbm, ybuf.at[slot], sem.at[slot], tm)
    obuf[slot] = _layer_norm(DN_ALPHA * h_ref[...] + _moe_sum(ybuf[slot], w_ref), g_ref[...], b_ref[...])
    put(s, slot, wait=False)

    @pl.when(s == n - 1)
    def _():
        if n >= 2:
            put(s - 1, 1 - slot, wait=True)
        put(s, slot, wait=True)


def _combine_out(h1, wcols, y_slots, pos, ln_g, ln_b, batch, seq):
    t, d = h1.shape
    frame = t // batch
    tm = _pick_tile(frame, 512, CHUNK)
    nblk = frame // tm
    n = t // tm
    row = pl.BlockSpec((tm, d), lambda i, pos: (i, 0))
    vec = pl.BlockSpec((1, d), lambda i, pos: (0, 0))
    grid_spec = pltpu.PrefetchScalarGridSpec(
        num_scalar_prefetch=1,
        grid=(n,),
        in_specs=[row, pl.BlockSpec((tm, LANES), lambda i, pos: (i, 0)), pl.BlockSpec(memory_space=pl.ANY), vec, vec],
        out_specs=pl.BlockSpec(memory_space=pl.ANY),
        scratch_shapes=[pltpu.VMEM((2, tm, SUBLANES, LANES), jnp.uint32), pltpu.VMEM((2, tm, d), F32),
                        pltpu.SemaphoreType.DMA((2,)), pltpu.SemaphoreType.DMA((2,))],
    )
    return pl.pallas_call(
        functools.partial(_combine_out_kernel, tm=tm, nblk=nblk, n=n),
        grid_spec=grid_spec,
        out_shape=jax.ShapeDtypeStruct((batch, seq, d), F32),
        compiler_params=_params("arbitrary"),
        name="combine_out",
    )(pos, h1, wcols, y_slots, ln_g, ln_b)


def _combine(h1, wcols, rowmask, y_slots, pos, ln_g, ln_b):
    t, d = h1.shape
    tm = _pick_tile(t, 512, 16)
    row = pl.BlockSpec((tm, d), lambda i, pos: (i, 0))
    vec = pl.BlockSpec((1, d), lambda i, pos: (0, 0))
    grid_spec = pltpu.PrefetchScalarGridSpec(
        num_scalar_prefetch=1,
        grid=(t // tm,),
        in_specs=[row, pl.BlockSpec((tm, LANES), lambda i, pos: (i, 0)), pl.BlockSpec((tm, 1), lambda i, pos: (i, 0)),
                  pl.BlockSpec(memory_space=pl.ANY), vec, vec],
        out_specs=[row, row],
        scratch_shapes=[pltpu.VMEM((2, tm, SUBLANES, LANES), jnp.uint32), pltpu.SemaphoreType.DMA((2,))],
    )
    return pl.pallas_call(
        functools.partial(_combine_kernel, tm=tm),
        grid_spec=grid_spec,
        out_shape=[jax.ShapeDtypeStruct((t, d), F32), jax.ShapeDtypeStruct((t, d), BF16)],
        compiler_params=_params("arbitrary"),
        name="combine",
    )(pos, h1, wcols, rowmask, y_slots, ln_g, ln_b)


_PAIR_A = np.array([p[0] for p in PAIRS], np.int32)
_PAIR_B = np.array([p[1] for p in PAIRS], np.int32)


def _dispatch_plan(route, t, layer):
    route = jnp.transpose(route, (1, 0, 2)).reshape(SUBLANES, t)
    cls = route[0].astype(jnp.int32)
    onehot = (cls[:, None] == jnp.arange(N_CLASSES, dtype=jnp.int32)[None, :]).astype(jnp.int32)
    csum = jnp.cumsum(onehot, axis=0)
    counts = csum[-1]
    rank = jnp.sum(onehot * csum, axis=1) - 1
    tiles_per = (counts + EXPERT_TM - 1) // EXPERT_TM
    tile_end = jnp.cumsum(tiles_per)
    cls_start = (tile_end - tiles_per) * EXPERT_TM
    pos = cls_start[cls] + rank
    n_tiles = -(-t // EXPERT_TM) + N_CLASSES
    n_used = tile_end[-1]
    tile_ids = jnp.minimum(jnp.arange(n_tiles, dtype=jnp.int32), n_used - 1)
    tile_cls = jnp.sum((tile_ids[:, None] >= tile_end[None, :]).astype(jnp.int32), axis=1)
    grp = tile_cls // len(PAIRS)
    pair = tile_cls % len(PAIRS)
    tile_ea = layer * N_EXPERTS + grp * EXPERTS_PER_GROUP + jnp.asarray(_PAIR_A)[pair]
    tile_eb = layer * N_EXPERTS + grp * EXPERTS_PER_GROUP + jnp.asarray(_PAIR_B)[pair]
    wcols = jnp.pad(route[1:3].T, ((0, 0), (0, LANES - 2)))
    zero_tile = jnp.maximum(tile_end - 1, 0).astype(jnp.int32)
    zero_on = (tiles_per > 0).astype(jnp.int32)
    return (pos.astype(jnp.int32), wcols, tile_ea, tile_eb, n_used.reshape(1).astype(jnp.int32),
            zero_tile, zero_on, n_tiles * EXPERT_TM)


def kernel(x, meta_tokens, ln_in_g, ln_in_b, w_router, w_in, conv_w, conv_b, dt_bias, a_log, d_skip,
           ssd_norm_w, w_ssd_out, w_pool, pool_scale, b_gate, w_out, ln1_g, ln1_b, w_exp_gate, w_exp_up,
           w_exp_down, ln2_g, ln2_b):
    batch, seq, d = x.shape
    frame = FRAME_HEAD + seq
    assert d == D_MODEL and frame % CHUNK == 0
    t = batch * frame
    depth = w_in.shape[0]

    head = jnp.concatenate([jnp.zeros((PAD_FRONT, d), x.dtype), meta_tokens.astype(x.dtype)], axis=0)
    rowmask = jnp.asarray(np.tile(np.arange(frame) >= PAD_FRONT, batch).astype(np.float32).reshape(t, 1))
    h, hb = _embed_ln(x, head, ln_in_g, ln_in_b)

    wr_t = w_router.T
    wr_hi = wr_t.astype(BF16)
    wr_lo = (wr_t - wr_hi.astype(F32)).astype(BF16)

    def pad_lanes(v):
        return jnp.pad(v, (0, LANES - v.shape[0])).reshape(1, LANES)

    wg_all = w_exp_gate.reshape(depth * N_EXPERTS, d, D_EXPERT)
    wu_all = w_exp_up.reshape(depth * N_EXPERTS, d, D_EXPERT)
    wd_all = w_exp_down.reshape(depth * N_EXPERTS, D_EXPERT, d)

    for l in range(depth):
        w = w_in[l]
        w_xbc = w[:, COL_XBC:COL_DT].astype(BF16)
        w_plain = jnp.concatenate([w[:, COL_Z:COL_XBC], w[:, COL_GATE:], w[:, COL_POOL:COL_GATE]], axis=1).astype(BF16)
        w_dt = jnp.pad(w[:, COL_DT:COL_POOL], ((0, 0), (0, LANES - SSD_HEADS))).astype(BF16)
        xact, plain, dt_raw = _in_proj(hb, w_xbc, w_plain, w_dt, conv_w[l], conv_b[l].reshape(1, SSD_XBC))

        yn = _ssd(
            xact, plain, dt_raw, pad_lanes(dt_bias[l]),
            pad_lanes(-jnp.exp(a_log[l])), jnp.repeat(d_skip[l], SSD_HEAD_DIM).reshape(1, SSD_INNER),
            ssd_norm_w[l].reshape(1, SSD_INNER), batch, frame)

        h1, route = _mix(
            yn, plain, h, w_ssd_out[l].astype(BF16), w_pool[l].astype(BF16), w_out[l].astype(BF16),
            b_gate[l].reshape(1, N_BRANCH * d), pool_scale[l].reshape(1, POOL_WIDTH),
            ln1_g[l].reshape(1, d), ln1_b[l].reshape(1, d), wr_hi, wr_lo, batch, frame)

        pos, wcols, tile_ea, tile_eb, n_used, zero_tile, zero_on, n_slots = _dispatch_plan(route, t, l)
        x_slots = _dispatch(h1, pos, zero_tile, zero_on, n_used, n_slots)
        y_slots = _experts(x_slots, tile_ea, tile_eb, n_used, wg_all, wu_all, wd_all)
        if l == depth - 1:
            return _combine_out(h1, wcols, y_slots, pos, ln2_g[l].reshape(1, d), ln2_b[l].reshape(1, d), batch, seq)
        h, hb = _combine(h1, wcols, rowmask, y_slots, pos, ln2_g[l].reshape(1, d), ln2_b[l].reshape(1, d))
```

```python
import functools

import jax
import jax.numpy as jnp
import numpy as np
from jax import lax
from jax.experimental import pallas as pl
from jax.experimental.pallas import tpu as pltpu

F32 = jnp.float32
BF16 = jnp.bfloat16

D_MODEL = 1024
CHUNK = 64
N_META = 16
PAD_FRONT = CHUNK - N_META
FRAME_HEAD = PAD_FRONT + N_META

SSD_INNER = 2048
SSD_HEAD_DIM = 64
SSD_HEADS = 32
SSD_GROUPS = 8
SSD_HPG = 4
SSD_STATE = 128
SSD_CONV = 4
SSD_GN = SSD_GROUPS * SSD_STATE
SSD_XBC = SSD_INNER + 2 * SSD_GN
SSD_GROUP_DIM = SSD_HPG * SSD_HEAD_DIM

POOL_WIDTH = 1024
POOL_WINDOWS = (2, 4, 8, 16)
POOL_GROUP_DIM = 256
POOL_HALO = 16

N_BRANCH = 2
COL_Z = 0
COL_XBC = COL_Z + SSD_INNER
COL_DT = COL_XBC + SSD_XBC
COL_POOL = COL_DT + SSD_HEADS
COL_GATE = COL_POOL + POOL_WIDTH
IN_COLS = COL_GATE + N_BRANCH * D_MODEL

N_EXPERTS = 16
N_EXPERT_GROUPS = 4
EXPERTS_PER_GROUP = 4
D_EXPERT = 512
PAIRS = ((0, 1), (0, 2), (0, 3), (1, 3), (1, 2), (3, 2))
N_CLASSES = N_EXPERT_GROUPS * len(PAIRS)

DEPTH = 2
DN_ALPHA = (2.0 * DEPTH) ** 0.25
LN_EPS = 1e-5
RMS_EPS = 1e-5

LANES = 128
SUBLANES = 8
VMEM_LIMIT = 56 * 1024 * 1024

PROJ_TN = 1024
PLAIN_COLS = SSD_INNER + N_BRANCH * D_MODEL + POOL_WIDTH
PLAIN_BLK_Z = 0
PLAIN_BLK_GATE = 1
PLAIN_BLK_POOL = (SSD_INNER + N_BRANCH * D_MODEL) // POOL_WIDTH

EXPERT_TM = 256


def _pick_tile(n, target, mult):
    best = None
    for t in range(mult, min(n, target) + 1, mult):
        if n % t == 0:
            best = t
    assert best is not None, (n, target, mult)
    return best


def _params(*sem):
    return pltpu.CompilerParams(dimension_semantics=sem, vmem_limit_bytes=VMEM_LIMIT)


def _sigmoid(x):
    return 0.5 + 0.5 * jnp.tanh(0.5 * x)


def _silu(x):
    hx = 0.5 * x
    return hx + hx * jnp.tanh(hx)


def _layer_norm(x, g, b):
    mu = jnp.mean(x, axis=-1, keepdims=True)
    xc = x - mu
    var = jnp.mean(xc * xc, axis=-1, keepdims=True)
    return xc * lax.rsqrt(var + LN_EPS) * g + b


def _frame_block_copy(hbm, buf, sem, step, slot, *, blk, nblk, to_hbm, wait):
    b = step // nblk
    j = step - b * nblk

    def run(vm, hb):
        cp = pltpu.make_async_copy(vm, hb, sem.at[slot]) if to_hbm else pltpu.make_async_copy(hb, vm, sem.at[slot])
        if wait:
            cp.wait()
        else:
            cp.start()

    if blk > FRAME_HEAD:
        @pl.when(j == 0)
        def _():
            n = blk - FRAME_HEAD
            run(buf.at[slot, pl.ds(FRAME_HEAD, n)], hbm.at[b, pl.ds(0, n)])

    @pl.when(j > 0)
    def _():
        start = pl.multiple_of(j * blk - FRAME_HEAD, CHUNK)
        run(buf.at[slot], hbm.at[b, pl.ds(start, blk)])


def _embed_kernel(x_hbm, head_ref, g_ref, b_ref, h_ref, hb_ref, xbuf, sem, *, blk, nblk, n):
    s = pl.program_id(0)
    slot = s % 2
    copy = functools.partial(_frame_block_copy, x_hbm, xbuf, sem, blk=blk, nblk=nblk, to_hbm=False)

    @pl.when(s == 0)
    def _():
        copy(0, 0, wait=False)

    @pl.when(s + 1 < n)
    def _():
        copy(s + 1, 1 - slot, wait=False)

    copy(s, slot, wait=True)
    first = s % nblk == 0

    @pl.when(first)
    def _():
        xbuf[slot, 0:FRAME_HEAD, :] = head_ref[...]

    y = _layer_norm(xbuf[slot], g_ref[...], b_ref[...])
    h_ref[...] = y
    row = lax.broadcasted_iota(jnp.int32, (blk, 1), 0)
    keep = jnp.logical_or(jnp.logical_not(first), row >= PAD_FRONT)
    hb_ref[...] = jnp.where(keep, y, 0.0).astype(BF16)


def _embed_ln(x, head, g, b):
    batch, seq, d = x.shape
    frame = FRAME_HEAD + seq
    blk = _pick_tile(frame, 832, CHUNK)
    nblk = frame // blk
    n = batch * nblk
    row = pl.BlockSpec((blk, d), lambda s: (s, 0))
    vec = pl.BlockSpec((1, d), lambda s: (0, 0))
    return pl.pallas_call(
        functools.partial(_embed_kernel, blk=blk, nblk=nblk, n=n),
        grid=(n,),
        in_specs=[pl.BlockSpec(memory_space=pl.ANY), pl.BlockSpec((FRAME_HEAD, d), lambda s: (0, 0)), vec, vec],
        out_specs=[row, row],
        out_shape=[jax.ShapeDtypeStruct((batch * frame, d), F32), jax.ShapeDtypeStruct((batch * frame, d), BF16)],
        scratch_shapes=[pltpu.VMEM((2, blk, d), F32), pltpu.SemaphoreType.DMA((2,))],
        compiler_params=_params("arbitrary"),
        name="embed_ln",
    )(x, head, g.reshape(1, d), b.reshape(1, d))


CONV_TILES = SSD_XBC // PROJ_TN


def _in_proj_kernel(x_ref, wx_ref, wp_ref, wdt_ref, cw_ref, cb_ref, xo_ref, po_ref, dt_ref, tail_ref, *, slab):
    i = pl.program_id(0)
    j = pl.program_id(1)
    tm = x_ref.shape[0]

    @pl.when(j == 0)
    def _():
        dt_ref[...] = jnp.dot(x_ref[...], wdt_ref[...], preferred_element_type=F32)

    @pl.when(j >= CONV_TILES)
    def _():
        po_ref[...] = jnp.dot(x_ref[...], wp_ref[...], preferred_element_type=F32).astype(BF16)

    @pl.when(j < CONV_TILES)
    def _():
        jt = jnp.minimum(j, CONV_TILES - 1)

        @pl.when(i == 0)
        def _():
            tail_ref[jt] = jnp.zeros((SUBLANES, PROJ_TN), F32)

        prev8 = tail_ref[jt]
        w = wx_ref[...]
        wp = wp_ref[...]
        sub = lax.broadcasted_iota(jnp.int32, (1, SUBLANES, PROJ_TN), 1)
        cw = [cw_ref[k:k + 1, :].reshape(1, 1, PROJ_TN) for k in range(SSD_CONV)]
        cb = cb_ref[...].reshape(1, 1, PROJ_TN)

        def shift_rows(v3, head8, s):
            r = pltpu.roll(v3, s, axis=1)
            above = jnp.concatenate([pltpu.roll(head8, s, axis=0)[None], r[:-1]], axis=0)
            return jnp.where(sub < s, above, r)

        for s in range(tm // slab):
            rows = slice(s * slab, (s + 1) * slab)
            acc = jnp.dot(x_ref[rows, :], w, preferred_element_type=F32)
            x0 = acc.reshape(slab // SUBLANES, SUBLANES, PROJ_TN)
            x2 = shift_rows(x0, prev8, 2)
            odd = cw[2] * x0 + cw[0] * x2
            odd_head = cw[2][0] * prev8 + cw[0][0] * pltpu.roll(prev8, 2, axis=0)
            conv = cb + cw[3] * x0 + cw[1] * x2 + shift_rows(odd, odd_head, 1)
            xo_ref[rows, :] = _silu(conv).reshape(slab, PROJ_TN).astype(BF16)
            prev8 = acc[slab - SUBLANES:]
            po_ref[rows, :] = jnp.dot(x_ref[rows, :], wp, preferred_element_type=F32).astype(BF16)
        tail_ref[jt] = prev8


def _in_proj(hb, w_xbc, w_plain, w_dt, conv_w, conv_b):
    t, d = hb.shape
    tm = _pick_tile(t, 2080, 16)
    slab = _pick_tile(tm, 256, 16)

    def conv_col(i, j):
        return (0, jnp.minimum(j, CONV_TILES - 1))

    return pl.pallas_call(
        functools.partial(_in_proj_kernel, slab=slab),
        grid=(t // tm, PLAIN_COLS // PROJ_TN),
        in_specs=[
            pl.BlockSpec((tm, d), lambda i, j: (i, 0)),
            pl.BlockSpec((d, PROJ_TN), conv_col),
            pl.BlockSpec((d, PROJ_TN), lambda i, j: (0, j)),
            pl.BlockSpec((d, LANES), lambda i, j: (0, 0)),
            pl.BlockSpec((SSD_CONV, PROJ_TN), conv_col),
            pl.BlockSpec((1, PROJ_TN), conv_col),
        ],
        out_specs=[
            pl.BlockSpec((tm, PROJ_TN), lambda i, j: (i, jnp.minimum(j, CONV_TILES - 1))),
            pl.BlockSpec((tm, PROJ_TN), lambda i, j: (i, j)),
            pl.BlockSpec((tm, LANES), lambda i, j: (i, 0)),
        ],
        out_shape=[jax.ShapeDtypeStruct((t, SSD_XBC), BF16), jax.ShapeDtypeStruct((t, PLAIN_COLS), BF16),
                   jax.ShapeDtypeStruct((t, LANES), F32)],
        scratch_shapes=[pltpu.VMEM((CONV_TILES, SUBLANES, PROJ_TN), F32)],
        compiler_params=_params("arbitrary", "arbitrary"),
        name="in_proj",
    )(hb, w_xbc, w_plain, w_dt, conv_w, conv_b)


def _pair_select(lane_lo, col, h0):
    return jnp.where(lane_lo, col[:, h0:h0 + 1], col[:, h0 + 1:h0 + 2])


def _ssd_kernel(xbc_ref, z_ref, dt_ref, dtb_ref, a_ref, dsk_ref, nw_ref, o_ref, prev_ref, y_ref, *, tb):
    j = pl.program_id(1)

    @pl.when(j == 0)
    def _():
        prev_ref[...] = jnp.zeros_like(prev_ref)

    row_i = lax.broadcasted_iota(jnp.int32, (CHUNK, LANES), 0)
    row_c = lax.broadcasted_iota(jnp.int32, (CHUNK, 1), 0)
    lane_i = lax.broadcasted_iota(jnp.int32, (CHUNK, LANES), 1)
    lane_lo = lane_i < SSD_HEAD_DIM
    causal2 = row_i >= jnp.where(lane_lo, lane_i, lane_i - SSD_HEAD_DIM)
    lane_lo_row = lax.broadcasted_iota(jnp.int32, (1, LANES), 1) < SSD_HEAD_DIM

    def chunk(c, carry):
        r0 = pl.multiple_of(c * CHUNK, CHUNK)
        rows = pl.ds(r0, CHUNK)
        valid = (j * tb + r0 + row_c) >= PAD_FRONT

        dtr = dt_ref[rows, :] + dtb_ref[...]
        dt = jnp.maximum(dtr, 0.0) + jnp.log1p(jnp.exp(-jnp.abs(dtr)))
        dt = jnp.where(valid, dt, 0.0)
        a_cs = dt * a_ref[...]
        for s in (1, 2, 4, 8, 16, 32):
            a_cs = a_cs + jnp.where(row_i >= s, pltpu.roll(a_cs, s, axis=0), 0.0)
        a_last = a_cs[CHUNK - 1:CHUNK, :]
        dtw = dt * jnp.exp(a_last - a_cs)
        cdec = jnp.exp(a_last)
        a_t = jnp.concatenate([a_cs, a_cs], axis=0).T
        dt_t = jnp.concatenate([dt, dt], axis=0).T

        def bmat(g):
            return xbc_ref[rows, SSD_INNER + g * SSD_STATE:SSD_INNER + (g + 1) * SSD_STATE]

        def cmat(g):
            o0 = SSD_INNER + SSD_GN + g * SSD_STATE
            return xbc_ref[rows, o0:o0 + SSD_STATE]

        cb2s, yoffs = [], []
        for g in range(SSD_GROUPS):
            bm, cm = bmat(g), cmat(g)
            cb2s.append(lax.dot_general(cm, jnp.concatenate([bm, bm], axis=0),
                                        (((1,), (1,)), ((), ())), preferred_element_type=F32))
            yoffs.append(jnp.dot(cm, prev_ref[g].astype(BF16), preferred_element_type=F32))

        for g in range(SSD_GROUPS):
            for q in range(SSD_HPG // 2):
                h0 = g * SSD_HPG + 2 * q
                ps = slice(g * SSD_GROUP_DIM + q * LANES, g * SSD_GROUP_DIM + (q + 1) * LANES)
                xh_b = xbc_ref[rows, ps]
                a_col = _pair_select(lane_lo, a_cs, h0)
                a_row = jnp.where(lane_lo_row, a_t[h0:h0 + 1, :], a_t[h0 + 1:h0 + 2, :])
                dt_row = jnp.where(lane_lo_row, dt_t[h0:h0 + 1, :], dt_t[h0 + 1:h0 + 2, :])
                decay = jnp.exp(jnp.where(causal2, a_col - a_row, -jnp.inf))
                m = (cb2s[g] * decay * dt_row).astype(BF16)
                zero = jnp.zeros_like(xh_b)
                rhs = jnp.concatenate([jnp.where(lane_lo, xh_b, zero), jnp.where(lane_lo, zero, xh_b)], axis=0)
                y_diag = jnp.dot(m, rhs, preferred_element_type=F32)
                y_off = yoffs[g][:, q * LANES:(q + 1) * LANES] * jnp.exp(a_col)
                y_ref[:, ps] = y_diag + y_off + dsk_ref[:, ps] * xh_b.astype(F32)

        for g in range(SSD_GROUPS):
            xs_parts, cdec_parts = [], []
            for q in range(SSD_HPG // 2):
                h0 = g * SSD_HPG + 2 * q
                ps = slice(g * SSD_GROUP_DIM + q * LANES, g * SSD_GROUP_DIM + (q + 1) * LANES)
                xs_parts.append((xbc_ref[rows, ps].astype(F32) * _pair_select(lane_lo, dtw, h0)).astype(BF16))
                cdec_parts.append(jnp.where(lane_lo_row, cdec[:, h0:h0 + 1], cdec[:, h0 + 1:h0 + 2]))
            xs_dec = jnp.concatenate(xs_parts, axis=1)
            st = lax.dot_general(bmat(g), xs_dec, (((0,), (0,)), ((), ())), preferred_element_type=F32)
            prev_ref[g] = prev_ref[g] * jnp.concatenate(cdec_parts, axis=1) + st

        for g in range(SSD_GROUPS):
            gs = slice(g * SSD_GROUP_DIM, (g + 1) * SSD_GROUP_DIM)
            yg = y_ref[:, gs] * _silu(z_ref[rows, gs].astype(F32))
            ms = jnp.mean(yg * yg, axis=-1, keepdims=True)
            o_ref[rows, gs] = (yg * lax.rsqrt(ms + RMS_EPS) * nw_ref[:, gs]).astype(BF16)
        return carry

    lax.fori_loop(0, tb // CHUNK, chunk, 0)


def _ssd(xact, plain, dt_raw, dt_bias, a_neg, d_skip_ch, norm_w, batch, frame):
    t = xact.shape[0]
    tb = _pick_tile(frame, 832, CHUNK)
    nblk = frame // tb

    def vec(n):
        return pl.BlockSpec((1, n), lambda b, j: (0, 0))

    return pl.pallas_call(
        functools.partial(_ssd_kernel, tb=tb),
        grid=(batch, nblk),
        in_specs=[
            pl.BlockSpec((tb, SSD_XBC), lambda b, j: (b * nblk + j, 0)),
            pl.BlockSpec((tb, SSD_INNER), lambda b, j: (b * nblk + j, PLAIN_BLK_Z)),
            pl.BlockSpec((tb, LANES), lambda b, j: (b * nblk + j, 0)),
            vec(LANES), vec(LANES), vec(SSD_INNER), vec(SSD_INNER),
        ],
        out_specs=pl.BlockSpec((tb, SSD_INNER), lambda b, j: (b * nblk + j, 0)),
        out_shape=jax.ShapeDtypeStruct((t, SSD_INNER), BF16),
        scratch_shapes=[
            pltpu.VMEM((SSD_GROUPS, SSD_STATE, SSD_GROUP_DIM), F32),
            pltpu.VMEM((CHUNK, SSD_INNER), F32),
        ],
        compiler_params=_params("parallel", "arbitrary"),
        name="ssd",
    )(xact, plain, dt_raw, dt_bias, a_neg, d_skip_ch, norm_w)


def _route(logits):
    mx = jnp.max(logits, axis=0, keepdims=True)
    ex = jnp.exp(logits - mx)
    sc = ex / jnp.sum(ex, axis=0, keepdims=True)
    rows = [sc[e:e + 1, :] for e in range(N_EXPERTS)]
    best = None
    for g in range(N_EXPERT_GROUPS):
        a, b, c, d = rows[4 * g:4 * g + 4]
        hi1, lo1 = jnp.maximum(a, b), jnp.minimum(a, b)
        hi2, lo2 = jnp.maximum(c, d), jnp.minimum(c, d)
        gsc = jnp.maximum(hi1, hi2) + jnp.maximum(jnp.minimum(hi1, hi2), jnp.maximum(lo1, lo2))
        if best is None:
            best, sel = gsc, jnp.zeros_like(gsc)
        else:
            better = gsc > best
            sel = jnp.where(better, float(g), sel)
            best = jnp.maximum(best, gsc)
    v = []
    for i in range(EXPERTS_PER_GROUP):
        vi = rows[i]
        for g in range(1, N_EXPERT_GROUPS):
            vi = jnp.where(sel == float(g), rows[4 * g + i], vi)
        v.append(vi)

    def first_max(u):
        m = jnp.maximum(jnp.maximum(u[0], u[1]), jnp.maximum(u[2], u[3]))
        idx = jnp.where(u[0] == m, 0.0, jnp.where(u[1] == m, 1.0, jnp.where(u[2] == m, 2.0, 3.0)))
        return m, idx

    m1, i1 = first_max(v)
    u = [jnp.where(i1 == float(i), -1.0, v[i]) for i in range(EXPERTS_PER_GROUP)]
    m2, i2 = first_max(u)
    den = m1 + m2
    w1, w2 = m1 / den, m2 / den
    lo = jnp.minimum(i1, i2)
    hi = jnp.maximum(i1, i2)
    w_lo = jnp.where(i1 < i2, w1, w2)
    w_hi = jnp.where(i1 < i2, w2, w1)
    pair = jnp.where(lo == 0.0, hi - 1.0, jnp.where(lo == 1.0, jnp.where(hi == 3.0, 3.0, 4.0), 5.0))
    swap = pair == 5.0
    w_a = jnp.where(swap, w_hi, w_lo)
    w_b = jnp.where(swap, w_lo, w_hi)
    cls = sel * float(len(PAIRS)) + pair
    zero = jnp.zeros_like(cls)
    return jnp.concatenate([cls, w_a, w_b, zero, zero, zero, zero, zero], axis=0)


def _mix_kernel(yn_ref, gate_ref, pool_ref, h_ref, wso_ref, wp_ref, wo_ref, bg_ref, ps_ref, g_ref, b_ref,
                wr_hi_ref, wr_lo_ref, h1_ref, route_ref, halo_ref, *, tb):
    j = pl.program_id(1)

    @pl.when(j == 0)
    def _():
        halo_ref[0:POOL_HALO, :] = jnp.zeros((POOL_HALO, POOL_WIDTH), F32)

    halo_ref[POOL_HALO:POOL_HALO + tb, :] = pool_ref[...].astype(F32)
    half = (tb // 2) // LANES * LANES
    subs = [(0, half), (half, tb - half)] if half >= 2 * LANES else [(0, tb)]
    nt = (((1,), (1,)), ((), ()))

    for r, n in subs:
        gates = _sigmoid(gate_ref[r:r + n, :].astype(F32) + bg_ref[...])

        tpos = (j * tb + r - PAD_FRONT + 1 + lax.broadcasted_iota(jnp.int32, (n, 1), 0)).astype(F32)
        pools = []
        for gi, win in enumerate(POOL_WINDOWS):
            cs = slice(gi * POOL_GROUP_DIM, (gi + 1) * POOL_GROUP_DIM)
            s = halo_ref[r:r + n + POOL_HALO, cs]
            step = 1
            while step < win:
                s = s[step:] + s[:-step]
                step *= 2
            wsum = s[s.shape[0] - n:]
            cnt = jnp.clip(tpos, 1.0, float(win))
            pooled = wsum / cnt - halo_ref[POOL_HALO + r:POOL_HALO + r + n, cs]
            pools.append(jnp.dot(pooled.astype(BF16), wp_ref[gi], preferred_element_type=F32))
        y_pool = jnp.concatenate(pools, axis=1) * ps_ref[...]

        y_ssd = jnp.dot(yn_ref[r:r + n, :], wso_ref[...], preferred_element_type=F32)
        mixed =(gates[:, :D_MODEL] * y_ssd + gates[:, D_MODEL:] * y_pool).astype(BF16)
        out = jnp.dot(mixed, wo_ref[...], preferred_element_type=F32)
        h1 = _layer_norm(DN_ALPHA * h_ref[r:r + n, :] + out, g_ref[...], b_ref[...])
        h1_ref[r:r + n, :] = h1

        h_hi = h1.astype(BF16)
        h_lo = (h1 - h_hi.astype(F32)).astype(BF16)
        logits = (lax.dot_general(wr_hi_ref[...], h_hi, nt, preferred_element_type=F32)
                  + lax.dot_general(wr_lo_ref[...], h_hi, nt, preferred_element_type=F32)
                  + lax.dot_general(wr_hi_ref[...], h_lo, nt, preferred_element_type=F32))
        route_ref[0, :, r:r + n] = _route(logits)

    halo_ref[0:POOL_HALO, :] = halo_ref[tb:tb + POOL_HALO, :]


def _mix(yn, plain, h, w_ssd_out, w_pool, w_out, b_gate, pool_scale, ln_g, ln_b, wr_hi, wr_lo, batch, frame):
    t = yn.shape[0]
    tb = _pick_tile(frame, 832, CHUNK)
    nblk = frame // tb
    d = D_MODEL

    def full(shape):
        return pl.BlockSpec(shape, lambda b, j: (0,) * len(shape))

    return pl.pallas_call(
        functools.partial(_mix_kernel, tb=tb),
        grid=(batch, nblk),
        in_specs=[
            pl.BlockSpec((tb, SSD_INNER), lambda b, j: (b * nblk + j, 0)),
            pl.BlockSpec((tb, N_BRANCH * d), lambda b, j: (b * nblk + j, PLAIN_BLK_GATE)),
            pl.BlockSpec((tb, POOL_WIDTH), lambda b, j: (b * nblk + j, PLAIN_BLK_POOL)),
            pl.BlockSpec((tb, d), lambda b, j: (b * nblk + j, 0)),
            full((SSD_INNER, d)), full((len(POOL_WINDOWS), POOL_GROUP_DIM, POOL_GROUP_DIM)), full((d, d)),
            full((1, N_BRANCH * d)), full((1, POOL_WIDTH)), full((1, d)), full((1, d)),
            full((N_EXPERTS, d)), full((N_EXPERTS, d)),
        ],
        out_specs=[
            pl.BlockSpec((tb, d), lambda b, j: (b * nblk + j, 0)),
            pl.BlockSpec((1, SUBLANES, tb), lambda b, j: (b * nblk + j, 0, 0)),
        ],
        out_shape=[jax.ShapeDtypeStruct((t, d), F32),
                   jax.ShapeDtypeStruct((batch * nblk, SUBLANES, tb), F32)],
        scratch_shapes=[pltpu.VMEM((POOL_HALO + tb, POOL_WIDTH), F32)],
        compiler_params=_params("parallel", "arbitrary"),
        name="mix",
    )(yn, plain, plain, h, w_ssd_out, w_pool, w_out, b_gate, pool_scale, ln_g, ln_b, wr_hi, wr_lo)


def _as_tiles(v):
    return v.reshape(v.shape[0], SUBLANES, LANES)


def _row_copies(hbm, idx_ref, base, buf, sem, n, *, to_hbm, wait):
    def body(g, carry):
        for u in range(SUBLANES):
            r = g * SUBLANES + u
            row = 0 if wait else idx_ref[base + r]
            vm = buf.at[r]
            hb = hbm.at[row]
            cp = pltpu.make_async_copy(vm, hb, sem) if to_hbm else pltpu.make_async_copy(hb, vm, sem)
            if wait:
                cp.wait()
            else:
                cp.start()
        return carry
    lax.fori_loop(0, n // SUBLANES, body, 0)


def _gather_rows(src_hbm, idx_ref, base, buf, sem, n):
    _row_copies(src_hbm, idx_ref, base, buf, sem, n, to_hbm=False, wait=False)


def _wait_rows(src_hbm, buf, sem, n):
    _row_copies(src_hbm, None, 0, buf, sem, n, to_hbm=False, wait=True)


def _scatter_rows(buf, idx_ref, base, dst_hbm, sem, n):
    _row_copies(dst_hbm, idx_ref, base, buf, sem, n, to_hbm=True, wait=False)


def _wait_scattered(buf, dst_hbm, sem, n):
    _row_copies(dst_hbm, None, 0, buf, sem, n, to_hbm=True, wait=True)


def _dispatch_kernel(pos_ref, zt_ref, zon_ref, nu_ref, h_ref, x_hbm, stage, zbuf, sem, zsem, *,
                     tm, n, n_tiles):
    i = pl.program_id(0)
    slot = i % 2

    @pl.when(i == 0)
    def _():
        zbuf[...] = jnp.zeros_like(zbuf)

        def zero_tile(tile):
            row0 = pl.multiple_of(tile * EXPERT_TM, EXPERT_TM)
            cp = pltpu.make_async_copy(zbuf, x_hbm.at[pl.ds(row0, EXPERT_TM)], zsem)
            cp.start()
            cp.wait()

        for c in range(N_CLASSES):
            pl.when(zon_ref[c] > 0)(functools.partial(zero_tile, zt_ref[c]))
            pl.when(nu_ref[0] + c < n_tiles)(functools.partial(zero_tile, nu_ref[0] + c))

    @pl.when(i >= 2)
    def _():
        _wait_scattered(stage.at[slot], x_hbm, sem.at[slot], tm)

    stage[slot] = _as_tiles(h_ref[...])
    _scatter_rows(stage.at[slot], pos_ref, i * tm, x_hbm, sem.at[slot], tm)

    @pl.when(i == n - 1)
    def _():
        if n >= 2:
            _wait_scattered(stage.at[1 - slot], x_hbm, sem.at[1 - slot], tm)
        _wait_scattered(stage.at[slot], x_hbm, sem.at[slot], tm)


def _dispatch(h1, pos, zero_tile, zero_on, n_used, n_slots):
    t, d = h1.shape
    assert d == SUBLANES * LANES
    tm = _pick_tile(t, 1040, 16)
    grid_spec = pltpu.PrefetchScalarGridSpec(
        num_scalar_prefetch=4,
        grid=(t // tm,),
        in_specs=[pl.BlockSpec((tm, d), lambda i, *_: (i, 0))],
        out_specs=pl.BlockSpec(memory_space=pl.ANY),
        scratch_shapes=[pltpu.VMEM((2, tm, SUBLANES, LANES), F32), pltpu.VMEM((EXPERT_TM, SUBLANES, LANES), F32),
                        pltpu.SemaphoreType.DMA((2,)), pltpu.SemaphoreType.DMA(())],
    )
    return pl.pallas_call(
        functools.partial(_dispatch_kernel, tm=tm, n=t // tm, n_tiles=n_slots // EXPERT_TM),
        grid_spec=grid_spec,
        out_shape=jax.ShapeDtypeStruct((n_slots, SUBLANES, LANES), F32),
        compiler_params=_params("arbitrary"),
        name="dispatch",
    )(pos, zero_tile, zero_on, n_used, h1)


def _pack_pair(a, b):
    a32 = lax.bitcast_convert_type(a.astype(BF16).astype(F32), jnp.uint32)
    b32 = lax.bitcast_convert_type(b.astype(BF16).astype(F32), jnp.uint32)
    return a32 | (b32 >> 16)


def _unpack_pair(w):
    a = lax.bitcast_convert_type(w & jnp.uint32(0xFFFF0000), F32)
    b = lax.bitcast_convert_type(w << 16, F32)
    return a, b


def _expert_kernel(ea_ref, eb_ref, nu_ref, newa_ref, newb_ref, x_ref, wga_ref, wua_ref, wda_ref,
                   wgb_ref, wub_ref, wdb_ref, o_ref, wg_s, wu_s, wd_s):
    i = pl.program_id(0)

    @pl.when(i < nu_ref[0])
    def _():
        @pl.when(newa_ref[i] > 0)
        def _():
            wg_s[0] = wga_ref[0].astype(BF16)
            wu_s[0] = wua_ref[0].astype(BF16)
            wd_s[0] = wda_ref[0].astype(BF16)

        @pl.when(newb_ref[i] > 0)
        def _():
            wg_s[1] = wgb_ref[0].astype(BF16)
            wu_s[1] = wub_ref[0].astype(BF16)
            wd_s[1] = wdb_ref[0].astype(BF16)

        x = x_ref[...].reshape(EXPERT_TM, D_MODEL).astype(BF16)

        def mlp(e):
            hid = _silu(jnp.dot(x, wg_s[e], preferred_element_type=F32)) * jnp.dot(x, wu_s[e], preferred_element_type=F32)
            return jnp.dot(hid.astype(BF16), wd_s[e], preferred_element_type=F32)

        o_ref[...] = _as_tiles(_pack_pair(mlp(0), mlp(1)))

    @pl.when(i >= nu_ref[0])
    def _():
        o_ref[...] = jnp.zeros_like(o_ref)


def _experts(x_slots, tile_ea, tile_eb, n_used, w_gate, w_up, w_down):
    n_slots = x_slots.shape[0]
    d = D_MODEL
    n_tiles = n_slots // EXPERT_TM
    tile = pl.BlockSpec((EXPERT_TM, SUBLANES, LANES), lambda i, *_: (i, 0, 0))
    first = jnp.ones((1,), jnp.int32)
    new_a = jnp.concatenate([first, (tile_ea[1:] != tile_ea[:-1]).astype(jnp.int32)])
    new_b = jnp.concatenate([first, (tile_eb[1:] != tile_eb[:-1]).astype(jnp.int32)])

    def wspec(shape, which):
        if which == 0:
            return pl.BlockSpec(shape, lambda i, ea, eb, *_: (ea[i], 0, 0))
        return pl.BlockSpec(shape, lambda i, ea, eb, *_: (eb[i], 0, 0))

    up = (1, d, D_EXPERT)
    down = (1, D_EXPERT, d)
    grid_spec = pltpu.PrefetchScalarGridSpec(
        num_scalar_prefetch=5,
        grid=(n_tiles,),
        in_specs=[
            tile,
            wspec(up, 0), wspec(up, 0), wspec(down, 0),
            wspec(up, 1), wspec(up, 1), wspec(down, 1),
        ],
        out_specs=tile,
        scratch_shapes=[pltpu.VMEM((2, d, D_EXPERT), BF16), pltpu.VMEM((2, d, D_EXPERT), BF16),
                        pltpu.VMEM((2, D_EXPERT, d), BF16)],
    )
    return pl.pallas_call(
        _expert_kernel,
        grid_spec=grid_spec,
        out_shape=jax.ShapeDtypeStruct((n_slots, SUBLANES, LANES), jnp.uint32),
        compiler_params=_params("arbitrary"),
        name="experts",
    )(tile_ea, tile_eb, n_used, new_a, new_b, x_slots, w_gate, w_up, w_down, w_gate, w_up, w_down)


def _moe_sum(y_tiles, w_ref):
    ya, yb = _unpack_pair(y_tiles.reshape(y_tiles.shape[0], D_MODEL))
    w = w_ref[...]
    return w[:, 0:1] * ya + w[:, 1:2] * yb


def _combine_kernel(pos_ref, h_ref, w_ref, m_ref, y_hbm, g_ref, b_ref, h2_ref, hb_ref, ybuf, sem, *, tm):
    i = pl.program_id(0)
    n = pl.num_programs(0)
    slot = i % 2

    @pl.when(i == 0)
    def _():
        _gather_rows(y_hbm, pos_ref, 0, ybuf.at[0], sem.at[0], tm)

    @pl.when(i + 1 < n)
    def _():
        _gather_rows(y_hbm, pos_ref, (i + 1) * tm, ybuf.at[1 - slot], sem.at[1 - slot], tm)

    _wait_rows(y_hbm, ybuf.at[slot], sem.at[slot], tm)
    h2 = _layer_norm(DN_ALPHA * h_ref[...] + _moe_sum(ybuf[slot], w_ref), g_ref[...], b_ref[...])
    h2_ref[...] = h2
    hb_ref[...] = (h2 * m_ref[...]).astype(BF16)


def _combine_out_kernel(pos_ref, h_ref, w_ref, y_hbm, g_ref, b_ref, out_hbm, ybuf, obuf, sem, osem, *, tm, nblk, n):
    s = pl.program_id(0)
    slot = s % 2
    put = functools.partial(_frame_block_copy, out_hbm, obuf, osem, blk=tm, nblk=nblk, to_hbm=True)

    @pl.when(s == 0)
    def _():
        _gather_rows(y_hbm, pos_ref, 0, ybuf.at[0], sem.at[0], tm)

    @pl.when(s + 1 < n)
    def _():
        _gather_rows(y_hbm, pos_ref, (s + 1) * tm, ybuf.at[1 - slot], sem.at[1 - slot], tm)

    @pl.when(s >= 2)
    def _():
        put(s - 2, slot, wait=True)

    _wait_rows(y_hbm, ybuf.at[slot], sem.at[slot], tm)
    obuf[slot] = _layer_norm(DN_ALPHA * h_ref[...] + _moe_sum(ybuf[slot], w_ref), g_ref[...], b_ref[...])
    put(s, slot, wait=False)

    @pl.when(s == n - 1)
    def _():
        if n >= 2:
            put(s - 1, 1 - slot, wait=True)
        put(s, slot, wait=True)


def _combine_out(h1, wcols, y_slots, pos, ln_g, ln_b, batch, seq):
    t, d = h1.shape
    frame = t // batch
    tm = _pick_tile(frame, 832, CHUNK)
    nblk = frame // tm
    n = t // tm
    row = pl.BlockSpec((tm, d), lambda i, pos: (i, 0))
    vec = pl.BlockSpec((1, d), lambda i, pos: (0, 0))
    grid_spec = pltpu.PrefetchScalarGridSpec(
        num_scalar_prefetch=1,
        grid=(n,),
        in_specs=[row, pl.BlockSpec((tm, LANES), lambda i, pos: (i, 0)), pl.BlockSpec(memory_space=pl.ANY), vec, vec],
        out_specs=pl.BlockSpec(memory_space=pl.ANY),
        scratch_shapes=[pltpu.VMEM((2, tm, SUBLANES, LANES), jnp.uint32), pltpu.VMEM((2, tm, d), F32),
                        pltpu.SemaphoreType.DMA((2,)), pltpu.SemaphoreType.DMA((2,))],
    )
    return pl.pallas_call(
        functools.partial(_combine_out_kernel, tm=tm, nblk=nblk, n=n),
        grid_spec=grid_spec,
        out_shape=jax.ShapeDtypeStruct((batch, seq, d), F32),
        compiler_params=_params("arbitrary"),
        name="combine_out",
    )(pos, h1, wcols, y_slots, ln_g, ln_b)


def _combine(h1, wcols, rowmask, y_slots, pos, ln_g, ln_b):
    t, d = h1.shape
    tm = _pick_tile(t, 1040, 16)
    row = pl.BlockSpec((tm, d), lambda i, pos: (i, 0))
    vec = pl.BlockSpec((1, d), lambda i, pos: (0, 0))
    grid_spec = pltpu.PrefetchScalarGridSpec(
        num_scalar_prefetch=1,
        grid=(t // tm,),
        in_specs=[row, pl.BlockSpec((tm, LANES), lambda i, pos: (i, 0)), pl.BlockSpec((tm, 1), lambda i, pos: (i, 0)),
                  pl.BlockSpec(memory_space=pl.ANY), vec, vec],
        out_specs=[row, row],
        scratch_shapes=[pltpu.VMEM((2, tm, SUBLANES, LANES), jnp.uint32), pltpu.SemaphoreType.DMA((2,))],
    )
    return pl.pallas_call(
        functools.partial(_combine_kernel, tm=tm),
        grid_spec=grid_spec,
        out_shape=[jax.ShapeDtypeStruct((t, d), F32), jax.ShapeDtypeStruct((t, d), BF16)],
        compiler_params=_params("arbitrary"),
        name="combine",
    )(pos, h1, wcols, rowmask, y_slots, ln_g, ln_b)


_PAIR_A = np.array([p[0] for p in PAIRS], np.int32)
_PAIR_B = np.array([p[1] for p in PAIRS], np.int32)


def _dispatch_plan(route, t, layer):
    route = jnp.transpose(route, (1, 0, 2)).reshape(SUBLANES, t)
    cls = route[0].astype(jnp.int32)
    onehot = (cls[:, None] == jnp.arange(N_CLASSES, dtype=jnp.int32)[None, :]).astype(jnp.int32)
    csum = jnp.cumsum(onehot, axis=0)
    counts = csum[-1]
    rank = jnp.sum(onehot * csum, axis=1) - 1
    tiles_per = (counts + EXPERT_TM - 1) // EXPERT_TM
    tile_end = jnp.cumsum(tiles_per)
    cls_start = (tile_end - tiles_per) * EXPERT_TM
    pos = cls_start[cls] + rank
    n_tiles = -(-t // EXPERT_TM) + N_CLASSES
    n_used = tile_end[-1]
    tile_ids = jnp.minimum(jnp.arange(n_tiles, dtype=jnp.int32), n_used - 1)
    tile_cls = jnp.sum((tile_ids[:, None] >= tile_end[None, :]).astype(jnp.int32), axis=1)
    grp = tile_cls // len(PAIRS)
    pair = tile_cls % len(PAIRS)
    tile_ea = layer * N_EXPERTS + grp * EXPERTS_PER_GROUP + jnp.asarray(_PAIR_A)[pair]
    tile_eb = layer * N_EXPERTS + grp * EXPERTS_PER_GROUP + jnp.asarray(_PAIR_B)[pair]
    wcols = jnp.pad(route[1:3].T, ((0, 0), (0, LANES - 2)))
    zero_tile = jnp.maximum(tile_end - 1, 0).astype(jnp.int32)
    zero_on = (tiles_per > 0).astype(jnp.int32)
    return (pos.astype(jnp.int32), wcols, tile_ea, tile_eb, n_used.reshape(1).astype(jnp.int32),
            zero_tile, zero_on, n_tiles * EXPERT_TM)


def kernel(x, meta_tokens, ln_in_g, ln_in_b, w_router, w_in, conv_w, conv_b, dt_bias, a_log, d_skip,
           ssd_norm_w, w_ssd_out, w_pool, pool_scale, b_gate, w_out, ln1_g, ln1_b, w_exp_gate, w_exp_up,
           w_exp_down, ln2_g, ln2_b):
    batch, seq, d = x.shape
    frame = FRAME_HEAD + seq
    assert d == D_MODEL and frame % CHUNK == 0
    t = batch * frame
    depth = w_in.shape[0]

    head = jnp.concatenate([jnp.zeros((PAD_FRONT, d), x.dtype), meta_tokens.astype(x.dtype)], axis=0)
    rowmask = jnp.asarray(np.tile(np.arange(frame) >= PAD_FRONT, batch).astype(np.float32).reshape(t, 1))
    h, hb = _embed_ln(x, head, ln_in_g, ln_in_b)

    wr_t = w_router.T
    wr_hi = wr_t.astype(BF16)
    wr_lo = (wr_t - wr_hi.astype(F32)).astype(BF16)

    def pad_lanes(v):
        return jnp.pad(v, (0, LANES - v.shape[0])).reshape(1, LANES)

    wg_all = w_exp_gate.reshape(depth * N_EXPERTS, d, D_EXPERT)
    wu_all = w_exp_up.reshape(depth * N_EXPERTS, d, D_EXPERT)
    wd_all = w_exp_down.reshape(depth * N_EXPERTS, D_EXPERT, d)

    for l in range(depth):
        w = w_in[l]
        w_xbc = w[:, COL_XBC:COL_DT].astype(BF16)
        w_plain = jnp.concatenate([w[:, COL_Z:COL_XBC], w[:, COL_GATE:], w[:, COL_POOL:COL_GATE]], axis=1).astype(BF16)
        w_dt = jnp.pad(w[:, COL_DT:COL_POOL], ((0, 0), (0, LANES - SSD_HEADS))).astype(BF16)
        xact, plain, dt_raw = _in_proj(hb, w_xbc, w_plain, w_dt, conv_w[l], conv_b[l].reshape(1, SSD_XBC))

        yn = _ssd(
            xact, plain, dt_raw, pad_lanes(dt_bias[l]),
            pad_lanes(-jnp.exp(a_log[l])), jnp.repeat(d_skip[l], SSD_HEAD_DIM).reshape(1, SSD_INNER),
            ssd_norm_w[l].reshape(1, SSD_INNER), batch, frame)

        h1, route = _mix(
            yn, plain, h, w_ssd_out[l].astype(BF16), w_pool[l].astype(BF16), w_out[l].astype(BF16),
            b_gate[l].reshape(1, N_BRANCH * d), pool_scale[l].reshape(1, POOL_WIDTH),
            ln1_g[l].reshape(1, d), ln1_b[l].reshape(1, d), wr_hi, wr_lo, batch, frame)

        pos, wcols, tile_ea, tile_eb, n_used, zero_tile, zero_on, n_slots = _dispatch_plan(route, t, l)
        x_slots = _dispatch(h1, pos, zero_tile, zero_on, n_used, n_slots)
        y_slots = _experts(x_slots, tile_ea, tile_eb, n_used, wg_all, wu_all, wd_all)
        if l == depth - 1:
            return _combine_out(h1, wcols, y_slots, pos, ln2_g[l].reshape(1, d), ln2_b[l].reshape(1, d), batch, seq)
        h, hb = _combine(h1, wcols, rowmask, y_slots, pos, ln2_g[l].reshape(1, d), ln2_b[l].reshape(1, d))
```

```python
import functools

import jax
import jax.numpy as jnp
import numpy as np
from jax import lax
from jax.experimental import pallas as pl
from jax.experimental.pallas import tpu as pltpu

F32 = jnp.float32
BF16 = jnp.bfloat16

D_MODEL = 1024
CHUNK = 64
N_META = 16
PAD_FRONT = CHUNK - N_META
FRAME_HEAD = PAD_FRONT + N_META

SSD_INNER = 2048
SSD_HEAD_DIM = 64
SSD_HEADS = 32
SSD_GROUPS = 8
SSD_HPG = 4
SSD_STATE = 128
SSD_CONV = 4
SSD_GN = SSD_GROUPS * SSD_STATE
SSD_XBC = SSD_INNER + 2 * SSD_GN
SSD_GROUP_DIM = SSD_HPG * SSD_HEAD_DIM

POOL_WIDTH = 1024
POOL_WINDOWS = (2, 4, 8, 16)
POOL_GROUP_DIM = 256
POOL_HALO = 16

N_BRANCH = 2
COL_Z = 0
COL_XBC = COL_Z + SSD_INNER
COL_DT = COL_XBC + SSD_XBC
COL_POOL = COL_DT + SSD_HEADS
COL_GATE = COL_POOL + POOL_WIDTH
IN_COLS = COL_GATE + N_BRANCH * D_MODEL

N_EXPERTS = 16
N_EXPERT_GROUPS = 4
EXPERTS_PER_GROUP = 4
D_EXPERT = 512
PAIRS = ((0, 1), (0, 2), (0, 3), (1, 3), (1, 2), (3, 2))
N_CLASSES = N_EXPERT_GROUPS * len(PAIRS)

DEPTH = 2
DN_ALPHA = (2.0 * DEPTH) ** 0.25
LN_EPS = 1e-5
RMS_EPS = 1e-5

LANES = 128
SUBLANES = 8
VMEM_LIMIT = 56 * 1024 * 1024

PROJ_TN = 1024
PLAIN_COLS = SSD_INNER + N_BRANCH * D_MODEL + POOL_WIDTH
PLAIN_BLK_Z = 0
PLAIN_BLK_GATE = 1
PLAIN_BLK_POOL = (SSD_INNER + N_BRANCH * D_MODEL) // POOL_WIDTH

EXPERT_TM = 256


def _pick_tile(n, target, mult):
    best = None
    for t in range(mult, min(n, target) + 1, mult):
        if n % t == 0:
            best = t
    assert best is not None, (n, target, mult)
    return best


def _params(*sem):
    return pltpu.CompilerParams(dimension_semantics=sem, vmem_limit_bytes=VMEM_LIMIT)


def _sigmoid(x):
    return 0.5 + 0.5 * jnp.tanh(0.5 * x)


def _silu(x):
    hx = 0.5 * x
    return hx + hx * jnp.tanh(hx)


def _layer_norm(x, g, b):
    mu = jnp.mean(x, axis=-1, keepdims=True)
    xc = x - mu
    var = jnp.mean(xc * xc, axis=-1, keepdims=True)
    return xc * lax.rsqrt(var + LN_EPS) * g + b


def _frame_block_copy(hbm, buf, sem, step, slot, *, blk, nblk, to_hbm, wait):
    b = step // nblk
    j = step - b * nblk

    def run(vm, hb):
        cp = pltpu.make_async_copy(vm, hb, sem.at[slot]) if to_hbm else pltpu.make_async_copy(hb, vm, sem.at[slot])
        if wait:
            cp.wait()
        else:
            cp.start()

    if blk > FRAME_HEAD:
        @pl.when(j == 0)
        def _():
            n = blk - FRAME_HEAD
            run(buf.at[slot, pl.ds(FRAME_HEAD, n)], hbm.at[b, pl.ds(0, n)])

    @pl.when(j > 0)
    def _():
        start = pl.multiple_of(j * blk - FRAME_HEAD, CHUNK)
        run(buf.at[slot], hbm.at[b, pl.ds(start, blk)])


def _embed_kernel(x_hbm, head_ref, g_ref, b_ref, h_ref, hb_ref, xbuf, sem, *, blk, nblk, n):
    s = pl.program_id(0)
    slot = s % 2
    copy = functools.partial(_frame_block_copy, x_hbm, xbuf, sem, blk=blk, nblk=nblk, to_hbm=False)

    @pl.when(s == 0)
    def _():
        copy(0, 0, wait=False)

    @pl.when(s + 1 < n)
    def _():
        copy(s + 1, 1 - slot, wait=False)

    copy(s, slot, wait=True)
    first = s % nblk == 0

    @pl.when(first)
    def _():
        xbuf[slot, 0:FRAME_HEAD, :] = head_ref[...]

    y = _layer_norm(xbuf[slot], g_ref[...], b_ref[...])
    h_ref[...] = y
    row = lax.broadcasted_iota(jnp.int32, (blk, 1), 0)
    keep = jnp.logical_or(jnp.logical_not(first), row >= PAD_FRONT)
    hb_ref[...] = jnp.where(keep, y, 0.0).astype(BF16)


def _embed_ln(x, head, g, b):
    batch, seq, d = x.shape
    frame = FRAME_HEAD + seq
    blk = _pick_tile(frame, 832, CHUNK)
    nblk = frame // blk
    n = batch * nblk
    row = pl.BlockSpec((blk, d), lambda s: (s, 0))
    vec = pl.BlockSpec((1, d), lambda s: (0, 0))
    return pl.pallas_call(
        functools.partial(_embed_kernel, blk=blk, nblk=nblk, n=n),
        grid=(n,),
        in_specs=[pl.BlockSpec(memory_space=pl.ANY), pl.BlockSpec((FRAME_HEAD, d), lambda s: (0, 0)), vec, vec],
        out_specs=[row, row],
        out_shape=[jax.ShapeDtypeStruct((batch * frame, d), F32), jax.ShapeDtypeStruct((batch * frame, d), BF16)],
        scratch_shapes=[pltpu.VMEM((2, blk, d), F32), pltpu.SemaphoreType.DMA((2,))],
        compiler_params=_params("arbitrary"),
        name="embed_ln",
    )(x, head, g.reshape(1, d), b.reshape(1, d))


CONV_TILES = SSD_XBC // PROJ_TN


def _in_proj_kernel(x_ref, wx_ref, wp_ref, wdt_ref, cw_ref, cb_ref, xo_ref, po_ref, dt_ref, tail_ref, *, slab):
    i = pl.program_id(0)
    j = pl.program_id(1)
    tm = x_ref.shape[0]

    @pl.when(j == 0)
    def _():
        dt_ref[...] = jnp.dot(x_ref[...], wdt_ref[...], preferred_element_type=F32)

    @pl.when(j >= CONV_TILES)
    def _():
        po_ref[...] = jnp.dot(x_ref[...], wp_ref[...], preferred_element_type=F32).astype(BF16)

    @pl.when(j < CONV_TILES)
    def _():
        jt = jnp.minimum(j, CONV_TILES - 1)

        @pl.when(i == 0)
        def _():
            tail_ref[jt] = jnp.zeros((SUBLANES, PROJ_TN), F32)

        prev8 = tail_ref[jt]
        w = wx_ref[...]
        wp = wp_ref[...]
        sub = lax.broadcasted_iota(jnp.int32, (1, SUBLANES, PROJ_TN), 1)
        cw = [cw_ref[k:k + 1, :].reshape(1, 1, PROJ_TN) for k in range(SSD_CONV)]
        cb = cb_ref[...].reshape(1, 1, PROJ_TN)

        def shift_rows(v3, head8, s):
            r = pltpu.roll(v3, s, axis=1)
            above = jnp.concatenate([pltpu.roll(head8, s, axis=0)[None], r[:-1]], axis=0)
            return jnp.where(sub < s, above, r)

        for s in range(tm // slab):
            rows = slice(s * slab, (s + 1) * slab)
            acc = jnp.dot(x_ref[rows, :], w, preferred_element_type=F32)
            x0 = acc.reshape(slab // SUBLANES, SUBLANES, PROJ_TN)
            x2 = shift_rows(x0, prev8, 2)
            odd = cw[2] * x0 + cw[0] * x2
            odd_head = cw[2][0] * prev8 + cw[0][0] * pltpu.roll(prev8, 2, axis=0)
            conv = cb + cw[3] * x0 + cw[1] * x2 + shift_rows(odd, odd_head, 1)
            xo_ref[rows, :] = _silu(conv).reshape(slab, PROJ_TN).astype(BF16)
            prev8 = acc[slab - SUBLANES:]
            po_ref[rows, :] = jnp.dot(x_ref[rows, :], wp, preferred_element_type=F32).astype(BF16)
        tail_ref[jt] = prev8


def _in_proj(hb, w_xbc, w_plain, w_dt, conv_w, conv_b):
    t, d = hb.shape
    tm = _pick_tile(t, 2080, 16)
    slab = _pick_tile(tm, 256, 16)

    def conv_col(i, j):
        return (0, jnp.minimum(j, CONV_TILES - 1))

    return pl.pallas_call(
        functools.partial(_in_proj_kernel, slab=slab),
        grid=(t // tm, PLAIN_COLS // PROJ_TN),
        in_specs=[
            pl.BlockSpec((tm, d), lambda i, j: (i, 0)),
            pl.BlockSpec((d, PROJ_TN), conv_col),
            pl.BlockSpec((d, PROJ_TN), lambda i, j: (0, j)),
            pl.BlockSpec((d, LANES), lambda i, j: (0, 0)),
            pl.BlockSpec((SSD_CONV, PROJ_TN), conv_col),
            pl.BlockSpec((1, PROJ_TN), conv_col),
        ],
        out_specs=[
            pl.BlockSpec((tm, PROJ_TN), lambda i, j: (i, jnp.minimum(j, CONV_TILES - 1))),
            pl.BlockSpec((tm, PROJ_TN), lambda i, j: (i, j)),
            pl.BlockSpec((tm, LANES), lambda i, j: (i, 0)),
        ],
        out_shape=[jax.ShapeDtypeStruct((t, SSD_XBC), BF16), jax.ShapeDtypeStruct((t, PLAIN_COLS), BF16),
                   jax.ShapeDtypeStruct((t, LANES), F32)],
        scratch_shapes=[pltpu.VMEM((CONV_TILES, SUBLANES, PROJ_TN), F32)],
        compiler_params=_params("arbitrary", "arbitrary"),
        name="in_proj",
    )(hb, w_xbc, w_plain, w_dt, conv_w, conv_b)


def _pair_select(lane_lo, col, h0):
    return jnp.where(lane_lo, col[:, h0:h0 + 1], col[:, h0 + 1:h0 + 2])


def _ssd_kernel(xbc_ref, z_ref, dt_ref, dtb_ref, a_ref, dsk_ref, nw_ref, o_ref, prev_ref, y_ref, *, tb):
    j = pl.program_id(1)

    @pl.when(j == 0)
    def _():
        prev_ref[...] = jnp.zeros_like(prev_ref)

    row_i = lax.broadcasted_iota(jnp.int32, (CHUNK, LANES), 0)
    row_c = lax.broadcasted_iota(jnp.int32, (CHUNK, 1), 0)
    lane_i = lax.broadcasted_iota(jnp.int32, (CHUNK, LANES), 1)
    lane_lo = lane_i < SSD_HEAD_DIM
    causal2 = row_i >= jnp.where(lane_lo, lane_i, lane_i - SSD_HEAD_DIM)
    lane_lo_row = lax.broadcasted_iota(jnp.int32, (1, LANES), 1) < SSD_HEAD_DIM

    def chunk(c, carry):
        r0 = pl.multiple_of(c * CHUNK, CHUNK)
        rows = pl.ds(r0, CHUNK)
        valid = (j * tb + r0 + row_c) >= PAD_FRONT

        dtr = dt_ref[rows, :] + dtb_ref[...]
        dt = jnp.maximum(dtr, 0.0) + jnp.log1p(jnp.exp(-jnp.abs(dtr)))
        dt = jnp.where(valid, dt, 0.0)
        a_cs = dt * a_ref[...]
        for s in (1, 2, 4, 8, 16, 32):
            a_cs = a_cs + jnp.where(row_i >= s, pltpu.roll(a_cs, s, axis=0), 0.0)
        a_last = a_cs[CHUNK - 1:CHUNK, :]
        dtw = dt * jnp.exp(a_last - a_cs)
        cdec = jnp.exp(a_last)
        a_t = jnp.concatenate([a_cs, a_cs], axis=0).T
        dt_t = jnp.concatenate([dt, dt], axis=0).T

        def bmat(g):
            return xbc_ref[rows, SSD_INNER + g * SSD_STATE:SSD_INNER + (g + 1) * SSD_STATE]

        def cmat(g):
            o0 = SSD_INNER + SSD_GN + g * SSD_STATE
            return xbc_ref[rows, o0:o0 + SSD_STATE]

        cb2s, yoffs = [], []
        for g in range(SSD_GROUPS):
            bm, cm = bmat(g), cmat(g)
            cb2s.append(lax.dot_general(cm, jnp.concatenate([bm, bm], axis=0),
                                        (((1,), (1,)), ((), ())), preferred_element_type=F32))
            yoffs.append(jnp.dot(cm, prev_ref[g].astype(BF16), preferred_element_type=F32))

        for g in range(SSD_GROUPS):
            for q in range(SSD_HPG // 2):
                h0 = g * SSD_HPG + 2 * q
                ps = slice(g * SSD_GROUP_DIM + q * LANES, g * SSD_GROUP_DIM + (q + 1) * LANES)
                xh_b = xbc_ref[rows, ps]
                a_col = _pair_select(lane_lo, a_cs, h0)
                a_row = jnp.where(lane_lo_row, a_t[h0:h0 + 1, :], a_t[h0 + 1:h0 + 2, :])
                dt_row = jnp.where(lane_lo_row, dt_t[h0:h0 + 1, :], dt_t[h0 + 1:h0 + 2, :])
                decay = jnp.exp(jnp.where(causal2, a_col - a_row, -jnp.inf))
                m = (cb2s[g] * decay * dt_row).astype(BF16)
                zero = jnp.zeros_like(xh_b)
                rhs = jnp.concatenate([jnp.where(lane_lo, xh_b, zero), jnp.where(lane_lo, zero, xh_b)], axis=0)
                y_diag = jnp.dot(m, rhs, preferred_element_type=F32)
                y_off = yoffs[g][:, q * LANES:(q + 1) * LANES] * jnp.exp(a_col)
                y_ref[:, ps] = y_diag + y_off + dsk_ref[:, ps] * xh_b.astype(F32)

        for g in range(SSD_GROUPS):
            xs_parts, cdec_parts = [], []
            for q in range(SSD_HPG // 2):
                h0 = g * SSD_HPG + 2 * q
                ps = slice(g * SSD_GROUP_DIM + q * LANES, g * SSD_GROUP_DIM + (q + 1) * LANES)
                xs_parts.append((xbc_ref[rows, ps].astype(F32) * _pair_select(lane_lo, dtw, h0)).astype(BF16))
                cdec_parts.append(jnp.where(lane_lo_row, cdec[:, h0:h0 + 1], cdec[:, h0 + 1:h0 + 2]))
            xs_dec = jnp.concatenate(xs_parts, axis=1)
            st = lax.dot_general(bmat(g), xs_dec, (((0,), (0,)), ((), ())), preferred_element_type=F32)
            prev_ref[g] = prev_ref[g] * jnp.concatenate(cdec_parts, axis=1) + st

        for g in range(SSD_GROUPS):
            gs = slice(g * SSD_GROUP_DIM, (g + 1) * SSD_GROUP_DIM)
            yg = y_ref[:, gs] * _silu(z_ref[rows, gs].astype(F32))
            ms = jnp.mean(yg * yg, axis=-1, keepdims=True)
            o_ref[rows, gs] = (yg * lax.rsqrt(ms + RMS_EPS) * nw_ref[:, gs]).astype(BF16)
        return carry

    lax.fori_loop(0, tb // CHUNK, chunk, 0)


def _ssd(xact, plain, dt_raw, dt_bias, a_neg, d_skip_ch, norm_w, batch, frame):
    t = xact.shape[0]
    tb = _pick_tile(frame, 832, CHUNK)
    nblk = frame // tb

    def vec(n):
        return pl.BlockSpec((1, n), lambda b, j: (0, 0))

    return pl.pallas_call(
        functools.partial(_ssd_kernel, tb=tb),
        grid=(batch, nblk),
        in_specs=[
            pl.BlockSpec((tb, SSD_XBC), lambda b, j: (b * nblk + j, 0)),
            pl.BlockSpec((tb, SSD_INNER), lambda b, j: (b * nblk + j, PLAIN_BLK_Z)),
            pl.BlockSpec((tb, LANES), lambda b, j: (b * nblk + j, 0)),
            vec(LANES), vec(LANES), vec(SSD_INNER), vec(SSD_INNER),
        ],
        out_specs=pl.BlockSpec((tb, SSD_INNER), lambda b, j: (b * nblk + j, 0)),
        out_shape=jax.ShapeDtypeStruct((t, SSD_INNER), BF16),
        scratch_shapes=[
            pltpu.VMEM((SSD_GROUPS, SSD_STATE, SSD_GROUP_DIM), F32),
            pltpu.VMEM((CHUNK, SSD_INNER), F32),
        ],
        compiler_params=_params("parallel", "arbitrary"),
        name="ssd",
    )(xact, plain, dt_raw, dt_bias, a_neg, d_skip_ch, norm_w)


def _route(logits):
    mx = jnp.max(logits, axis=0, keepdims=True)
    ex = jnp.exp(logits - mx)
    sc = ex / jnp.sum(ex, axis=0, keepdims=True)
    rows = [sc[e:e + 1, :] for e in range(N_EXPERTS)]
    best = None
    for g in range(N_EXPERT_GROUPS):
        a, b, c, d = rows[4 * g:4 * g + 4]
        hi1, lo1 = jnp.maximum(a, b), jnp.minimum(a, b)
        hi2, lo2 = jnp.maximum(c, d), jnp.minimum(c, d)
        gsc = jnp.maximum(hi1, hi2) + jnp.maximum(jnp.minimum(hi1, hi2), jnp.maximum(lo1, lo2))
        if best is None:
            best, sel = gsc, jnp.zeros_like(gsc)
        else:
            better = gsc > best
            sel = jnp.where(better, float(g), sel)
            best = jnp.maximum(best, gsc)
    v = []
    for i in range(EXPERTS_PER_GROUP):
        vi = rows[i]
        for g in range(1, N_EXPERT_GROUPS):
            vi = jnp.where(sel == float(g), rows[4 * g + i], vi)
        v.append(vi)

    def first_max(u):
        m = jnp.maximum(jnp.maximum(u[0], u[1]), jnp.maximum(u[2], u[3]))
        idx = jnp.where(u[0] == m, 0.0, jnp.where(u[1] == m, 1.0, jnp.where(u[2] == m, 2.0, 3.0)))
        return m, idx

    m1, i1 = first_max(v)
    u = [jnp.where(i1 == float(i), -1.0, v[i]) for i in range(EXPERTS_PER_GROUP)]
    m2, i2 = first_max(u)
    den = m1 + m2
    w1, w2 = m1 / den, m2 / den
    lo = jnp.minimum(i1, i2)
    hi = jnp.maximum(i1, i2)
    w_lo = jnp.where(i1 < i2, w1, w2)
    w_hi = jnp.where(i1 < i2, w2, w1)
    pair = jnp.where(lo == 0.0, hi - 1.0, jnp.where(lo == 1.0, jnp.where(hi == 3.0, 3.0, 4.0), 5.0))
    swap = pair == 5.0
    w_a = jnp.where(swap, w_hi, w_lo)
    w_b = jnp.where(swap, w_lo, w_hi)
    cls = sel * float(len(PAIRS)) + pair
    zero = jnp.zeros_like(cls)
    return jnp.concatenate([cls, w_a, w_b, zero, zero, zero, zero, zero], axis=0)


def _mix_kernel(yn_ref, gate_ref, pool_ref, h_ref, wso_ref, wp_ref, wo_ref, bg_ref, ps_ref, g_ref, b_ref,
                wr_hi_ref, wr_lo_ref, h1_ref, route_ref, halo_ref, *, tb):
    j = pl.program_id(1)

    @pl.when(j == 0)
    def _():
        halo_ref[0:POOL_HALO, :] = jnp.zeros((POOL_HALO, POOL_WIDTH), F32)

    halo_ref[POOL_HALO:POOL_HALO + tb, :] = pool_ref[...].astype(F32)
    half = (tb // 2) // LANES * LANES
    subs = [(0, half), (half, tb - half)] if half >= 2 * LANES else [(0, tb)]
    nt = (((1,), (1,)), ((), ()))

    for r, n in subs:
        gates = _sigmoid(gate_ref[r:r + n, :].astype(F32) + bg_ref[...])

        tpos = (j * tb + r - PAD_FRONT + 1 + lax.broadcasted_iota(jnp.int32, (n, 1), 0)).astype(F32)
        pools = []
        for gi, win in enumerate(POOL_WINDOWS):
            cs = slice(gi * POOL_GROUP_DIM, (gi + 1) * POOL_GROUP_DIM)
            s = halo_ref[r:r + n + POOL_HALO, cs]
            step = 1
            while step < win:
                s = s[step:] + s[:-step]
                step *= 2
            wsum = s[s.shape[0] - n:]
            cnt = jnp.clip(tpos, 1.0, float(win))
            pooled = wsum / cnt - halo_ref[POOL_HALO + r:POOL_HALO + r + n, cs]
            pools.append(jnp.dot(pooled.astype(BF16), wp_ref[gi], preferred_element_type=F32))
        y_pool = jnp.concatenate(pools, axis=1) * ps_ref[...]

        y_ssd = jnp.dot(yn_ref[r:r + n, :], wso_ref[...], preferred_element_type=F32)
        mixed =(gates[:, :D_MODEL] * y_ssd + gates[:, D_MODEL:] * y_pool).astype(BF16)
        out = jnp.dot(mixed, wo_ref[...], preferred_element_type=F32)
        h1 = _layer_norm(DN_ALPHA * h_ref[r:r + n, :] + out, g_ref[...], b_ref[...])
        h1_ref[r:r + n, :] = h1

        h_hi = h1.astype(BF16)
        h_lo = (h1 - h_hi.astype(F32)).astype(BF16)
        logits = (lax.dot_general(wr_hi_ref[...], h_hi, nt, preferred_element_type=F32)
                  + lax.dot_general(wr_lo_ref[...], h_hi, nt, preferred_element_type=F32)
                  + lax.dot_general(wr_hi_ref[...], h_lo, nt, preferred_element_type=F32))
        route_ref[0, :, r:r + n] = _route(logits)

    halo_ref[0:POOL_HALO, :] = halo_ref[tb:tb + POOL_HALO, :]


def _mix(yn, plain, h, w_ssd_out, w_pool, w_out, b_gate, pool_scale, ln_g, ln_b, wr_hi, wr_lo, batch, frame):
    t = yn.shape[0]
    tb = _pick_tile(frame, 832, CHUNK)
    nblk = frame // tb
    d = D_MODEL

    def full(shape):
        return pl.BlockSpec(shape, lambda b, j: (0,) * len(shape))

    return pl.pallas_call(
        functools.partial(_mix_kernel, tb=tb),
        grid=(batch, nblk),
        in_specs=[
            pl.BlockSpec((tb, SSD_INNER), lambda b, j: (b * nblk + j, 0)),
            pl.BlockSpec((tb, N_BRANCH * d), lambda b, j: (b * nblk + j, PLAIN_BLK_GATE)),
            pl.BlockSpec((tb, POOL_WIDTH), lambda b, j: (b * nblk + j, PLAIN_BLK_POOL)),
            pl.BlockSpec((tb, d), lambda b, j: (b * nblk + j, 0)),
            full((SSD_INNER, d)), full((len(POOL_WINDOWS), POOL_GROUP_DIM, POOL_GROUP_DIM)), full((d, d)),
            full((1, N_BRANCH * d)), full((1, POOL_WIDTH)), full((1, d)), full((1, d)),
            full((N_EXPERTS, d)), full((N_EXPERTS, d)),
        ],
        out_specs=[
            pl.BlockSpec((tb, d), lambda b, j: (b * nblk + j, 0)),
            pl.BlockSpec((1, SUBLANES, tb), lambda b, j: (b * nblk + j, 0, 0)),
        ],
        out_shape=[jax.ShapeDtypeStruct((t, d), F32),
                   jax.ShapeDtypeStruct((batch * nblk, SUBLANES, tb), F32)],
        scratch_shapes=[pltpu.VMEM((POOL_HALO + tb, POOL_WIDTH), F32)],
        compiler_params=_params("parallel", "arbitrary"),
        name="mix",
    )(yn, plain, plain, h, w_ssd_out, w_pool, w_out, b_gate, pool_scale, ln_g, ln_b, wr_hi, wr_lo)


def _as_tiles(v):
    return v.reshape(v.shape[0], SUBLANES, LANES)


def _row_copies(hbm, idx_ref, base, buf, sem, n, *, to_hbm, wait):
    def body(g, carry):
        for u in range(SUBLANES):
            r = g * SUBLANES + u
            row = 0 if wait else idx_ref[base + r]
            vm = buf.at[r]
            hb = hbm.at[row]
            cp = pltpu.make_async_copy(vm, hb, sem) if to_hbm else pltpu.make_async_copy(hb, vm, sem)
            if wait:
                cp.wait()
            else:
                cp.start()
        return carry
    lax.fori_loop(0, n // SUBLANES, body, 0)


def _gather_rows(src_hbm, idx_ref, base, buf, sem, n):
    _row_copies(src_hbm, idx_ref, base, buf, sem, n, to_hbm=False, wait=False)


def _wait_rows(src_hbm, buf, sem, n):
    _row_copies(src_hbm, None, 0, buf, sem, n, to_hbm=False, wait=True)


def _scatter_rows(buf, idx_ref, base, dst_hbm, sem, n):
    _row_copies(dst_hbm, idx_ref, base, buf, sem, n, to_hbm=True, wait=False)


def _wait_scattered(buf, dst_hbm, sem, n):
    _row_copies(dst_hbm, None, 0, buf, sem, n, to_hbm=True, wait=True)


def _dispatch_kernel(pos_ref, zt_ref, zon_ref, nu_ref, h_ref, x_hbm, stage, zbuf, sem, zsem, *,
                     tm, n, n_tiles):
    i = pl.program_id(0)
    slot = i % 2

    @pl.when(i == 0)
    def _():
        zbuf[...] = jnp.zeros_like(zbuf)

        def zero_tile(tile):
            row0 = pl.multiple_of(tile * EXPERT_TM, EXPERT_TM)
            cp = pltpu.make_async_copy(zbuf, x_hbm.at[pl.ds(row0, EXPERT_TM)], zsem)
            cp.start()
            cp.wait()

        for c in range(N_CLASSES):
            pl.when(zon_ref[c] > 0)(functools.partial(zero_tile, zt_ref[c]))
            pl.when(nu_ref[0] + c < n_tiles)(functools.partial(zero_tile, nu_ref[0] + c))

    @pl.when(i >= 2)
    def _():
        _wait_scattered(stage.at[slot], x_hbm, sem.at[slot], tm)

    stage[slot] = _as_tiles(h_ref[...])
    _scatter_rows(stage.at[slot], pos_ref, i * tm, x_hbm, sem.at[slot], tm)

    @pl.when(i == n - 1)
    def _():
        if n >= 2:
            _wait_scattered(stage.at[1 - slot], x_hbm, sem.at[1 - slot], tm)
        _wait_scattered(stage.at[slot], x_hbm, sem.at[slot], tm)


def _dispatch(h1, pos, zero_tile, zero_on, n_used, n_slots):
    t, d = h1.shape
    assert d == SUBLANES * LANES
    tm = _pick_tile(t, 256, 16)
    grid_spec = pltpu.PrefetchScalarGridSpec(
        num_scalar_prefetch=4,
        grid=(t // tm,),
        in_specs=[pl.BlockSpec((tm, d), lambda i, *_: (i, 0))],
        out_specs=pl.BlockSpec(memory_space=pl.ANY),
        scratch_shapes=[pltpu.VMEM((2, tm, SUBLANES, LANES), F32), pltpu.VMEM((EXPERT_TM, SUBLANES, LANES), F32),
                        pltpu.SemaphoreType.DMA((2,)), pltpu.SemaphoreType.DMA(())],
    )
    return pl.pallas_call(
        functools.partial(_dispatch_kernel, tm=tm, n=t // tm, n_tiles=n_slots // EXPERT_TM),
        grid_spec=grid_spec,
        out_shape=jax.ShapeDtypeStruct((n_slots, SUBLANES, LANES), F32),
        compiler_params=_params("arbitrary"),
        name="dispatch",
    )(pos, zero_tile, zero_on, n_used, h1)


def _pack_pair(a, b):
    a32 = lax.bitcast_convert_type(a.astype(BF16).astype(F32), jnp.uint32)
    b32 = lax.bitcast_convert_type(b.astype(BF16).astype(F32), jnp.uint32)
    return a32 | (b32 >> 16)


def _unpack_pair(w):
    a = lax.bitcast_convert_type(w & jnp.uint32(0xFFFF0000), F32)
    b = lax.bitcast_convert_type(w << 16, F32)
    return a, b


def _expert_kernel(ea_ref, eb_ref, nu_ref, newa_ref, newb_ref, x_ref, wga_ref, wua_ref, wda_ref,
                   wgb_ref, wub_ref, wdb_ref, o_ref, wg_s, wu_s, wd_s):
    i = pl.program_id(0)

    @pl.when(i < nu_ref[0])
    def _():
        @pl.when(newa_ref[i] > 0)
        def _():
            wg_s[0] = wga_ref[0].astype(BF16)
            wu_s[0] = wua_ref[0].astype(BF16)
            wd_s[0] = wda_ref[0].astype(BF16)

        @pl.when(newb_ref[i] > 0)
        def _():
            wg_s[1] = wgb_ref[0].astype(BF16)
            wu_s[1] = wub_ref[0].astype(BF16)
            wd_s[1] = wdb_ref[0].astype(BF16)

        x = x_ref[...].reshape(EXPERT_TM, D_MODEL).astype(BF16)

        def mlp(e):
            hid = _silu(jnp.dot(x, wg_s[e], preferred_element_type=F32)) * jnp.dot(x, wu_s[e], preferred_element_type=F32)
            return jnp.dot(hid.astype(BF16), wd_s[e], preferred_element_type=F32)

        o_ref[...] = _as_tiles(_pack_pair(mlp(0), mlp(1)))

    @pl.when(i >= nu_ref[0])
    def _():
        o_ref[...] = jnp.zeros_like(o_ref)


def _experts(x_slots, tile_ea, tile_eb, n_used, w_gate, w_up, w_down):
    n_slots = x_slots.shape[0]
    d = D_MODEL
    n_tiles = n_slots // EXPERT_TM
    tile = pl.BlockSpec((EXPERT_TM, SUBLANES, LANES), lambda i, *_: (i, 0, 0))
    first = jnp.ones((1,), jnp.int32)
    new_a = jnp.concatenate([first, (tile_ea[1:] != tile_ea[:-1]).astype(jnp.int32)])
    new_b = jnp.concatenate([first, (tile_eb[1:] != tile_eb[:-1]).astype(jnp.int32)])

    def wspec(shape, which):
        if which == 0:
            return pl.BlockSpec(shape, lambda i, ea, eb, *_: (ea[i], 0, 0))
        return pl.BlockSpec(shape, lambda i, ea, eb, *_: (eb[i], 0, 0))

    up = (1, d, D_EXPERT)
    down = (1, D_EXPERT, d)
    grid_spec = pltpu.PrefetchScalarGridSpec(
        num_scalar_prefetch=5,
        grid=(n_tiles,),
        in_specs=[
            tile,
            wspec(up, 0), wspec(up, 0), wspec(down, 0),
            wspec(up, 1), wspec(up, 1), wspec(down, 1),
        ],
        out_specs=tile,
        scratch_shapes=[pltpu.VMEM((2, d, D_EXPERT), BF16), pltpu.VMEM((2, d, D_EXPERT), BF16),
                        pltpu.VMEM((2, D_EXPERT, d), BF16)],
    )
    return pl.pallas_call(
        _expert_kernel,
        grid_spec=grid_spec,
        out_shape=jax.ShapeDtypeStruct((n_slots, SUBLANES, LANES), jnp.uint32),
        compiler_params=_params("arbitrary"),
        name="experts",
    )(tile_ea, tile_eb, n_used, new_a, new_b, x_slots, w_gate, w_up, w_down, w_gate, w_up, w_down)


def _moe_sum(y_tiles, w_ref):
    ya, yb = _unpack_pair(y_tiles.reshape(y_tiles.shape[0], D_MODEL))
    w = w_ref[...]
    return w[:, 0:1] * ya + w[:, 1:2] * yb


def _combine_kernel(pos_ref, h_ref, w_ref, m_ref, y_hbm, g_ref, b_ref, h2_ref, hb_ref, ybuf, sem, *, tm):
    i = pl.program_id(0)
    n = pl.num_programs(0)
    slot = i % 2

    @pl.when(i == 0)
    def _():
        _gather_rows(y_hbm, pos_ref, 0, ybuf.at[0], sem.at[0], tm)

    @pl.when(i + 1 < n)
    def _():
        _gather_rows(y_hbm, pos_ref, (i + 1) * tm, ybuf.at[1 - slot], sem.at[1 - slot], tm)

    _wait_rows(y_hbm, ybuf.at[slot], sem.at[slot], tm)
    h2 = _layer_norm(DN_ALPHA * h_ref[...] + _moe_sum(ybuf[slot], w_ref), g_ref[...], b_ref[...])
    h2_ref[...] = h2
    hb_ref[...] = (h2 * m_ref[...]).astype(BF16)


def _combine_out_kernel(pos_ref, h_ref, w_ref, y_hbm, g_ref, b_ref, out_hbm, ybuf, obuf, sem, osem, *, tm, nblk, n):
    s = pl.program_id(0)
    slot = s % 2
    put = functools.partial(_frame_block_copy, out_hbm, obuf, osem, blk=tm, nblk=nblk, to_hbm=True)

    @pl.when(s == 0)
    def _():
        _gather_rows(y_hbm, pos_ref, 0, ybuf.at[0], sem.at[0], tm)

    @pl.when(s + 1 < n)
    def _():
        _gather_rows(y_hbm, pos_ref, (s + 1) * tm, ybuf.at[1 - slot], sem.at[1 - slot], tm)

    @pl.when(s >= 2)
    def _():
        put(s - 2, slot, wait=True)

    _wait_rows(y_hbm, ybuf.at[slot], sem.at[slot], tm)
    obuf[slot] = _layer_norm(DN_ALPHA * h_ref[...] + _moe_sum(ybuf[slot], w_ref), g_ref[...], b_ref[...])
    put(s, slot, wait=False)

    @pl.when(s == n - 1)
    def _():
        if n >= 2:
            put(s - 1, 1 - slot, wait=True)
        put(s, slot, wait=True)


def _combine_out(h1, wcols, y_slots, pos, ln_g, ln_b, batch, seq):
    t, d = h1.shape
    frame = t // batch
    tm = _pick_tile(frame, 512, CHUNK)
    nblk = frame // tm
    n = t // tm
    row = pl.BlockSpec((tm, d), lambda i, pos: (i, 0))
    vec = pl.BlockSpec((1, d), lambda i, pos: (0, 0))
    grid_spec = pltpu.PrefetchScalarGridSpec(
        num_scalar_prefetch=1,
        grid=(n,),
        in_specs=[row, pl.BlockSpec((tm, LANES), lambda i, pos: (i, 0)), pl.BlockSpec(memory_space=pl.ANY), vec, vec],
        out_specs=pl.BlockSpec(memory_space=pl.ANY),
        scratch_shapes=[pltpu.VMEM((2, tm, SUBLANES, LANES), jnp.uint32), pltpu.VMEM((2, tm, d), F32),
                        pltpu.SemaphoreType.DMA((2,)), pltpu.SemaphoreType.DMA((2,))],
    )
    return pl.pallas_call(
        functools.partial(_combine_out_kernel, tm=tm, nblk=nblk, n=n),
        grid_spec=grid_spec,
        out_shape=jax.ShapeDtypeStruct((batch, seq, d), F32),
        compiler_params=_params("arbitrary"),
        name="combine_out",
    )(pos, h1, wcols, y_slots, ln_g, ln_b)


def _combine(h1, wcols, rowmask, y_slots, pos, ln_g, ln_b):
    t, d = h1.shape
    tm = _pick_tile(t, 256, 16)
    row = pl.BlockSpec((tm, d), lambda i, pos: (i, 0))
    vec = pl.BlockSpec((1, d), lambda i, pos: (0, 0))
    grid_spec = pltpu.PrefetchScalarGridSpec(
        num_scalar_prefetch=1,
        grid=(t // tm,),
        in_specs=[row, pl.BlockSpec((tm, LANES), lambda i, pos: (i, 0)), pl.BlockSpec((tm, 1), lambda i, pos: (i, 0)),
                  pl.BlockSpec(memory_space=pl.ANY), vec, vec],
        out_specs=[row, row],
        scratch_shapes=[pltpu.VMEM((2, tm, SUBLANES, LANES), jnp.uint32), pltpu.SemaphoreType.DMA((2,))],
    )
    return pl.pallas_call(
        functools.partial(_combine_kernel, tm=tm),
        grid_spec=grid_spec,
        out_shape=[jax.ShapeDtypeStruct((t, d), F32), jax.ShapeDtypeStruct((t, d), BF16)],
        compiler_params=_params("arbitrary"),
        name="combine",
    )(pos, h1, wcols, rowmask, y_slots, ln_g, ln_b)


_PAIR_A = np.array([p[0] for p in PAIRS], np.int32)
_PAIR_B = np.array([p[1] for p in PAIRS], np.int32)


def _dispatch_plan(route, t, layer):
    route = jnp.transpose(route, (1, 0, 2)).reshape(SUBLANES, t)
    cls = route[0].astype(jnp.int32)
    onehot = (cls[:, None] == jnp.arange(N_CLASSES, dtype=jnp.int32)[None, :]).astype(jnp.int32)
    csum = jnp.cumsum(onehot, axis=0)
    counts = csum[-1]
    rank = jnp.sum(onehot * csum, axis=1) - 1
    tiles_per = (counts + EXPERT_TM - 1) // EXPERT_TM
    tile_end = jnp.cumsum(tiles_per)
    cls_start = (tile_end - tiles_per) * EXPERT_TM
    pos = cls_start[cls] + rank
    n_tiles = -(-t // EXPERT_TM) + N_CLASSES
    n_used = tile_end[-1]
    tile_ids = jnp.minimum(jnp.arange(n_tiles, dtype=jnp.int32), n_used - 1)
    tile_cls = jnp.sum((tile_ids[:, None] >= tile_end[None, :]).astype(jnp.int32), axis=1)
    grp = tile_cls // len(PAIRS)
    pair = tile_cls % len(PAIRS)
    tile_ea = layer * N_EXPERTS + grp * EXPERTS_PER_GROUP + jnp.asarray(_PAIR_A)[pair]
    tile_eb = layer * N_EXPERTS + grp * EXPERTS_PER_GROUP + jnp.asarray(_PAIR_B)[pair]
    wcols = jnp.pad(route[1:3].T, ((0, 0), (0, LANES - 2)))
    zero_tile = jnp.maximum(tile_end - 1, 0).astype(jnp.int32)
    zero_on = (tiles_per > 0).astype(jnp.int32)
    return (pos.astype(jnp.int32), wcols, tile_ea, tile_eb, n_used.reshape(1).astype(jnp.int32),
            zero_tile, zero_on, n_tiles * EXPERT_TM)


def kernel(x, meta_tokens, ln_in_g, ln_in_b, w_router, w_in, conv_w, conv_b, dt_bias, a_log, d_skip,
           ssd_norm_w, w_ssd_out, w_pool, pool_scale, b_gate, w_out, ln1_g, ln1_b, w_exp_gate, w_exp_up,
           w_exp_down, ln2_g, ln2_b):
    batch, seq, d = x.shape
    frame = FRAME_HEAD + seq
    assert d == D_MODEL and frame % CHUNK == 0
    t = batch * frame
    depth = w_in.shape[0]

    head = jnp.concatenate([jnp.zeros((PAD_FRONT, d), x.dtype), meta_tokens.astype(x.dtype)], axis=0)
    rowmask = jnp.asarray(np.tile(np.arange(frame) >= PAD_FRONT, batch).astype(np.float32).reshape(t, 1))
    h, hb = _embed_ln(x, head, ln_in_g, ln_in_b)

    wr_t = w_router.T
    wr_hi = wr_t.astype(BF16)
    wr_lo = (wr_t - wr_hi.astype(F32)).astype(BF16)

    def pad_lanes(v):
        return jnp.pad(v, (0, LANES - v.shape[0])).reshape(1, LANES)

    wg_all = w_exp_gate.reshape(depth * N_EXPERTS, d, D_EXPERT)
    wu_all = w_exp_up.reshape(depth * N_EXPERTS, d, D_EXPERT)
    wd_all = w_exp_down.reshape(depth * N_EXPERTS, D_EXPERT, d)

    for l in range(depth):
        w = w_in[l]
        w_xbc = w[:, COL_XBC:COL_DT].astype(BF16)
        w_plain = jnp.concatenate([w[:, COL_Z:COL_XBC], w[:, COL_GATE:], w[:, COL_POOL:COL_GATE]], axis=1).astype(BF16)
        w_dt = jnp.pad(w[:, COL_DT:COL_POOL], ((0, 0), (0, LANES - SSD_HEADS))).astype(BF16)
        xact, plain, dt_raw = _in_proj(hb, w_xbc, w_plain, w_dt, conv_w[l], conv_b[l].reshape(1, SSD_XBC))

        yn = _ssd(
            xact, plain, dt_raw, pad_lanes(dt_bias[l]),
            pad_lanes(-jnp.exp(a_log[l])), jnp.repeat(d_skip[l], SSD_HEAD_DIM).reshape(1, SSD_INNER),
            ssd_norm_w[l].reshape(1, SSD_INNER), batch, frame)

        h1, route = _mix(
            yn, plain, h, w_ssd_out[l].astype(BF16), w_pool[l].astype(BF16), w_out[l].astype(BF16),
            b_gate[l].reshape(1, N_BRANCH * d), pool_scale[l].reshape(1, POOL_WIDTH),
            ln1_g[l].reshape(1, d), ln1_b[l].reshape(1, d), wr_hi, wr_lo, batch, frame)

        pos, wcols, tile_ea, tile_eb, n_used, zero_tile, zero_on, n_slots = _dispatch_plan(route, t, l)
        x_slots = _dispatch(h1, pos, zero_tile, zero_on, n_used, n_slots)
        y_slots = _experts(x_slots, tile_ea, tile_eb, n_used, wg_all, wu_all, wd_all)
        if l == depth - 1:
            return _combine_out(h1, wcols, y_slots, pos, ln2_g[l].reshape(1, d), ln2_b[l].reshape(1, d), batch, seq)
        h, hb = _combine(h1, wcols, rowmask, y_slots, pos, ln2_g[l].reshape(1, d), ln2_b[l].reshape(1, d))
```
